```python
import jax, jax.numpy as jnp
from jax import lax
import numpy as np

D_MODEL = 2048
BATCH = 2
SEQ = 4096
DEPTH = 2
DEC_BATCH = 128
DEC_SEQ = 4
PAST_LEN = 8192
PAGE_SIZE = 128

N_ATT = (DEPTH + 1) // 2
N_REC = DEPTH // 2
EPS = 1e-6

POOL_WINDOWS = (2, 4, 8, 16)
N_POOL_GROUPS = len(POOL_WINDOWS)
D_POOL = D_MODEL // 2
POOL_GROUP = D_POOL // N_POOL_GROUPS
POOL_STATE = max(POOL_WINDOWS) - 1

N_HEADS = 16
Q_LORA = D_MODEL // 4
KV_LORA = D_MODEL // 4
NOPE_DIM = 128
ROPE_DIM = 64
V_DIM = 128
ROPE_THETA = 10000.0
MLA_SCALE = (NOPE_DIM + ROPE_DIM) ** -0.5
Q_BLOCK = 128

HG_HEADS = 16
HG_KDIM = D_MODEL // HG_HEADS
HG_VDIM = D_MODEL // HG_HEADS
HG_CHUNK = 64

D_FF = 5632
N_EXPERTS = 8
TOP_K = 2
D_FF_EXPERT = 2816

kernel_name = 'hybrid_pool_mla_hgrn2_moe_decode_step'


def rmsnorm(x, w):
    xf = x.astype(jnp.float32)
    y = xf * lax.rsqrt(jnp.mean(xf * xf, axis=-1, keepdims=True) + EPS)
    return (y * w.astype(jnp.float32)).astype(x.dtype)


def swiglu(x, w_gu, w_down):
    g, u = jnp.split(x @ w_gu, 2, axis=-1)
    return (jax.nn.silu(g) * u) @ w_down


def rope_tables(pos):
    inv = ROPE_THETA ** (-jnp.arange(0, ROPE_DIM, 2, dtype=jnp.float32) / ROPE_DIM)
    ang = pos.astype(jnp.float32)[:, None] * inv[None, :]
    return jnp.cos(ang), jnp.sin(ang)


def apply_rope(x, cos, sin):
    x1, x2 = jnp.split(x.astype(jnp.float32), 2, axis=-1)
    return jnp.concatenate([x1 * cos - x2 * sin, x2 * cos + x1 * sin], axis=-1).astype(x.dtype)


def pool_mix(u, past, pool_w, pool_scale, start):
    L = u.shape[1]
    ext = jnp.concatenate([past, u], axis=1).astype(jnp.float32)
    cs = jnp.cumsum(ext, axis=1)
    cs = jnp.concatenate([jnp.zeros_like(cs[:, :1]), cs], axis=1)
    pos = start + jnp.arange(L)
    outs = []
    for g, w in enumerate(POOL_WINDOWS):
        lo, hi = g * POOL_GROUP, (g + 1) * POOL_GROUP
        win_sum = cs[:, POOL_STATE + 1:, lo:hi] - cs[:, POOL_STATE + 1 - w:POOL_STATE + 1 - w + L, lo:hi]
        cnt = jnp.minimum(pos + 1, w).astype(jnp.float32)[None, :, None]
        d = win_sum / cnt - ext[:, POOL_STATE:, lo:hi]
        outs.append(jnp.einsum('blc,cd->bld', d, pool_w[g].astype(jnp.float32)))
    return (jnp.concatenate(outs, axis=-1) * pool_scale.astype(jnp.float32)).astype(u.dtype)


def mla_attention(q_abs, q_pe, qpos, segs):
    B, L, H, C = q_abs.shape
    qb = Q_BLOCK if L % Q_BLOCK == 0 else L
    nb = L // qb

    def blocks(t):
        return t.reshape((B, nb, qb) + t.shape[2:]).swapaxes(0, 1)

    def one_block(args):
        qa, qp, qpos_b = args
        scores = []
        for c_seg, r_seg, kpos in segs:
            s = (jnp.einsum('bqhc,bkc->bhqk', qa, c_seg)
                 + jnp.einsum('bqhr,bkr->bhqk', qp, r_seg)).astype(jnp.float32) * MLA_SCALE
            scores.append(jnp.where(kpos[None, :] <= qpos_b[:, None], s, -jnp.inf))
        p = jax.nn.softmax(jnp.concatenate(scores, axis=-1), axis=-1)
        ctx, off = 0.0, 0
        for c_seg, _, kpos in segs:
            n = kpos.shape[0]
            ctx = ctx + jnp.einsum('bhqk,bkc->bqhc', p[..., off:off + n].astype(c_seg.dtype), c_seg)
            off += n
        return ctx

    ctx = lax.map(one_block, (blocks(q_abs), blocks(q_pe), qpos.reshape(nb, qb)))
    return ctx.swapaxes(0, 1).reshape(B, L, H, C)


def gla_chunked(q, k, v, logf, S0):
    B, L, H, _ = q.shape
    C = HG_CHUNK if L % HG_CHUNK == 0 else L
    n = L // C

    def to_chunks(t):
        return t.astype(jnp.float32).reshape(B, n, C, H, -1).transpose(1, 0, 3, 2, 4)

    causal = jnp.tril(jnp.ones((C, C), dtype=bool))

    def step(S, inp):
        qc, kc, vc, gc = inp
        b = jnp.cumsum(gc, axis=2)
        o_inter = jnp.einsum('bhtk,bhkv->bhtv', qc * jnp.exp(b), S)
        diff = b[:, :, :, None, :] - b[:, :, None, :, :]
        decay = jnp.exp(jnp.where(causal[:, :, None], diff, -jnp.inf))
        A = jnp.einsum('bhtk,bhsk,bhtsk->bhts', qc, kc, decay)
        o_intra = jnp.einsum('bhts,bhsv->bhtv', A, vc)
        b_last = b[:, :, -1:, :]
        S_new = jnp.exp(b_last[:, :, 0, :])[..., None] * S + jnp.einsum(
            'bhsk,bhsv->bhkv', kc * jnp.exp(b_last - b), vc)
        return S_new, o_inter + o_intra

    S, o = lax.scan(step, S0, (to_chunks(q), to_chunks(k), to_chunks(v), to_chunks(logf)))
    return o.transpose(1, 0, 3, 2, 4).reshape(B, L, H, -1), S


def hgrn2_mix(xn, S0, lb, w_in, hg_norm, w_out):
    B, L, _ = xn.shape
    hk, hv = HG_HEADS * HG_KDIM, HG_HEADS * HG_VDIM
    q, f, i, g = jnp.split(xn @ w_in, (hk, 2 * hk, 2 * hk + hv), axis=-1)
    logf = jnp.logaddexp(jnp.log(lb), jnp.log1p(-lb) + jax.nn.log_sigmoid(f.astype(jnp.float32)))
    k = -jnp.expm1(logf)

    def heads(t):
        return t.reshape(B, L, HG_HEADS, -1)

    o, S = gla_chunked(heads(jax.nn.silu(q)), heads(k), heads(i), heads(logf), S0.astype(jnp.float32))
    o = rmsnorm(o, hg_norm) * jax.nn.silu(heads(g)).astype(jnp.float32)
    return o.reshape(B, L, hv).astype(xn.dtype) @ w_out, S


def moe_swiglu(xn, w_router, b_router, w_exp_gu, w_exp_down):
    logits = (xn @ w_router).astype(jnp.float32) + b_router.astype(jnp.float32)
    top_v, top_i = lax.top_k(logits, TOP_K)
    gates = jax.nn.softmax(top_v, axis=-1)
    comb = jnp.sum(jax.nn.one_hot(top_i, N_EXPERTS, dtype=jnp.float32) * gates[..., None], axis=-2)
    y = jnp.zeros(xn.shape, jnp.float32)
    for e in range(N_EXPERTS):
        y = y + swiglu(xn, w_exp_gu[e], w_exp_down[e]).astype(jnp.float32) * comb[..., e:e + 1]
    return y


def even_layer(h, pool_past, kv_past, start, p):
    (norm_mix, w_in, q_norm, w_q_b, kv_norm, w_kv_b, pool_w, pool_scale, w_out, norm_ffn, w_gu, w_down) = p
    B, L, _ = h.shape
    hn = rmsnorm(h, norm_mix)
    u, q_lat, kv_lat, k_raw = jnp.split(
        hn @ w_in, (D_POOL, D_POOL + Q_LORA, D_POOL + Q_LORA + KV_LORA), axis=-1)
    pool_out = pool_mix(u, pool_past, pool_w, pool_scale, start)
    pool_new = jnp.concatenate([pool_past, u], axis=1)[:, -POOL_STATE:]
    pos = start + jnp.arange(L)
    cos, sin = rope_tables(pos)
    q = (rmsnorm(q_lat, q_norm) @ w_q_b).reshape(B, L, N_HEADS, NOPE_DIM + ROPE_DIM)
    q_nope = q[..., :NOPE_DIM]
    q_pe = apply_rope(q[..., NOPE_DIM:], cos[:, None], sin[:, None])
    c_kv = rmsnorm(kv_lat, kv_norm)
    k_pe = apply_rope(k_raw, cos, sin)
    w_kv = w_kv_b.reshape(KV_LORA, N_HEADS, NOPE_DIM + V_DIM)
    w_uk, w_uv = w_kv[..., :NOPE_DIM], w_kv[..., NOPE_DIM:]
    q_abs = jnp.einsum('blhn,chn->blhc', q_nope, w_uk)
    segs = [(c_kv, k_pe, pos)]
    if kv_past is not None:
        segs = [(kv_past[0], kv_past[1], jnp.arange(start))] + segs
    ctx = mla_attention(q_abs, q_pe, pos, segs)
    attn_out = jnp.einsum('blhc,chv->blhv', ctx, w_uv).reshape(B, L, N_HEADS * V_DIM)
    mix = jnp.concatenate([pool_out, attn_out.astype(pool_out.dtype)], axis=-1) @ w_out
    h = h + mix.astype(h.dtype)
    h = h + swiglu(rmsnorm(h, norm_ffn), w_gu, w_down).astype(h.dtype)
    return h, pool_new, c_kv, k_pe


def odd_layer(h, S0, lb, p):
    (norm_mix, w_in, hg_norm, w_out, norm_ffn, w_router, b_router, w_exp_gu, w_exp_down) = p
    out, S = hgrn2_mix(rmsnorm(h, norm_mix), S0, lb, w_in, hg_norm, w_out)
    h = h + out.astype(h.dtype)
    h = h + moe_swiglu(rmsnorm(h, norm_ffn), w_router, b_router, w_exp_gu, w_exp_down).astype(h.dtype)
    return h, S.astype(h.dtype)


def setup_inputs(seed: int = 0) -> dict:
    key = jax.random.key(seed)
    ks = iter(jax.random.split(key, 40))
    f32 = jnp.float32

    def nrm(shape, scale=1.0):
        return jax.random.normal(next(ks), shape, f32) * scale

    def gain(shape, s=0.05):
        return 1.0 + nrm(shape, s)

    n_pages = PAST_LEN // PAGE_SIZE
    n_used = DEC_BATCH * n_pages
    n_phys = (n_used * 5) // 4
    page_table = jax.random.permutation(next(ks), n_phys)[:n_used].reshape(DEC_BATCH, n_pages).astype(jnp.int32)
    d_in_e = D_POOL + Q_LORA + KV_LORA + ROPE_DIM
    d_mix_e = D_POOL + N_HEADS * V_DIM
    d_in_o = 2 * HG_HEADS * HG_KDIM + 2 * HG_HEADS * HG_VDIM
    d_mix_o = HG_HEADS * HG_VDIM
    return {
        'x_prompt': nrm((BATCH, SEQ, D_MODEL)),
        'x_sample': nrm((DEC_BATCH, DEC_SEQ, D_MODEL)),
        'cache_ckv': nrm((N_ATT, n_phys, PAGE_SIZE, KV_LORA)),
        'cache_krope': nrm((N_ATT, n_phys, PAGE_SIZE, ROPE_DIM)),
        'page_table': page_table,
        'state_pool': nrm((N_ATT, DEC_BATCH, POOL_STATE, D_POOL)),
        'state_hgrn': nrm((N_REC, DEC_BATCH, HG_HEADS, HG_KDIM, HG_VDIM), 0.5),
        'norm_mix_e': gain((N_ATT, D_MODEL)),
        'w_in_e': nrm((N_ATT, D_MODEL, d_in_e), D_MODEL ** -0.5),
        'q_norm': gain((N_ATT, Q_LORA)),
        'w_q_b': nrm((N_ATT, Q_LORA, N_HEADS * (NOPE_DIM + ROPE_DIM)), Q_LORA ** -0.5),
        'kv_norm': gain((N_ATT, KV_LORA)),
        'w_kv_b': nrm((N_ATT, KV_LORA, N_HEADS * (NOPE_DIM + V_DIM)), KV_LORA ** -0.5),
        'pool_w': nrm((N_ATT, N_POOL_GROUPS, POOL_GROUP, POOL_GROUP), POOL_GROUP ** -0.5),
        'pool_scale': gain((N_ATT, D_POOL), 0.1),
        'w_out_e': nrm((N_ATT, d_mix_e, D_MODEL), d_mix_e ** -0.5),
        'norm_ffn_e': gain((N_ATT, D_MODEL)),
        'w_ffn_gu': nrm((N_ATT, D_MODEL, 2 * D_FF), D_MODEL ** -0.5),
        'w_ffn_down': nrm((N_ATT, D_FF, D_MODEL), D_FF ** -0.5),
        'norm_mix_o': gain((N_REC, D_MODEL)),
        'w_in_o': nrm((N_REC, D_MODEL, d_in_o), D_MODEL ** -0.5),
        'hg_lower_bound': nrm((DEPTH, HG_HEADS * HG_KDIM), 0.5),
        'hg_norm': gain((N_REC, HG_VDIM)),
        'w_out_o': nrm((N_REC, d_mix_o, D_MODEL), d_mix_o ** -0.5),
        'norm_ffn_o': gain((N_REC, D_MODEL)),
        'w_router': nrm((N_REC, D_MODEL, N_EXPERTS), D_MODEL ** -0.5),
        'b_router': nrm((N_REC, N_EXPERTS), 0.01),
        'w_exp_gu': nrm((N_REC, N_EXPERTS, D_MODEL, 2 * D_FF_EXPERT), D_MODEL ** -0.5),
        'w_exp_down': nrm((N_REC, N_EXPERTS, D_FF_EXPERT, D_MODEL), D_FF_EXPERT ** -0.5),
        'final_norm': gain((D_MODEL,)),
    }


def reference(x_prompt, x_sample, cache_ckv, cache_krope, page_table, state_pool, state_hgrn,
              norm_mix_e, w_in_e, q_norm, w_q_b, kv_norm, w_kv_b, pool_w, pool_scale, w_out_e,
              norm_ffn_e, w_ffn_gu, w_ffn_down,
              norm_mix_o, w_in_o, hg_lower_bound, hg_norm, w_out_o, norm_ffn_o, w_router, b_router,
              w_exp_gu, w_exp_down, final_norm):
    n_prompt = x_prompt.shape[0]
    n_dec = page_table.shape[0]
    lb_p = jax.nn.softmax(hg_lower_bound.astype(jnp.float32), axis=0)
    lower_bounds = jnp.cumsum(lb_p, axis=0) - lb_p[0]

    hp, hs = x_prompt, x_sample
    ckv_p, krope_p, pool_p, hgrn_p = [], [], [], []
    ckv_s, krope_s, pool_s, hgrn_s = [], [], [], []
    for l in range(DEPTH):
        a = l // 2
        if l % 2 == 0:
            p = (norm_mix_e[a], w_in_e[a], q_norm[a], w_q_b[a], kv_norm[a], w_kv_b[a], pool_w[a],
                 pool_scale[a], w_out_e[a], norm_ffn_e[a], w_ffn_gu[a], w_ffn_down[a])
            zero_pool = jnp.zeros((n_prompt, POOL_STATE, D_POOL), x_prompt.dtype)
            hp, pn, cn, rn = even_layer(hp, zero_pool, None, 0, p)
            ckv_p.append(cn); krope_p.append(rn); pool_p.append(pn)
            past = (cache_ckv[a, page_table].reshape(n_dec, PAST_LEN, KV_LORA),
                    cache_krope[a, page_table].reshape(n_dec, PAST_LEN, ROPE_DIM))
            hs, pn, cn, rn = even_layer(hs, state_pool[a], past, PAST_LEN, p)
            ckv_s.append(cn); krope_s.append(rn); pool_s.append(pn)
        else:
            p = (norm_mix_o[a], w_in_o[a], hg_norm[a], w_out_o[a], norm_ffn_o[a], w_router[a], b_router[a],
                 w_exp_gu[a], w_exp_down[a])
            zero_state = jnp.zeros((n_prompt, HG_HEADS, HG_KDIM, HG_VDIM), jnp.float32)
            hp, sn = odd_layer(hp, zero_state, lower_bounds[l], p)
            hgrn_p.append(sn)
            hs, sn = odd_layer(hs, state_hgrn[a], lower_bounds[l], p)
            hgrn_s.append(sn)

    y_prompt = rmsnorm(hp, final_norm)
    y_sample = rmsnorm(hs, final_norm)
    return (y_prompt, y_sample,
            jnp.stack(ckv_p), jnp.stack(krope_p), jnp.stack(pool_p), jnp.stack(hgrn_p),
            jnp.stack(ckv_s), jnp.stack(krope_s), jnp.stack(pool_s), jnp.stack(hgrn_s))
```

```python
import functools

import jax
import jax.numpy as jnp
from jax import lax
from jax.experimental import pallas as pl
from jax.experimental.pallas import tpu as pltpu

F32 = jnp.float32
BF16 = jnp.bfloat16

EPS = 1e-6
POOL_WINDOWS = (2, 4, 8, 16)
NOPE_DIM = 128
V_DIM = 128
ROPE_THETA = 10000.0
TOP_K = 2
LANES = 128
SUBLANES = 8
NEG_BIG = -1e30

LINEAR_TM = 512
FFN_TM = 512
FFN_TF = 512
MOE_TM = 512
MOE_TF = 256
FLASH_TQ = 128
FLASH_TK = 512
POOL_TL = 512
GLA_CT = 256
GLA_CHUNK = 64
DECODE_PAGES_PER_STEP = 8

NT_DIMS = (((1,), (1,)), ((), ()))
TN_DIMS = (((0,), (0,)), ((), ()))


def _cparams(vmem_mb):
    return pltpu.CompilerParams(vmem_limit_bytes=vmem_mb * 2 ** 20)


def _tile(n, pref, mult=SUBLANES):
    if n <= pref:
        return n
    for t in range(pref, 0, -1):
        if n % t == 0 and t % mult == 0:
            return t
    return n


def _rms(x, g):
    return x * lax.rsqrt(jnp.mean(x * x, axis=-1, keepdims=True) + EPS) * g


def _silu(x):
    return x / (1.0 + jnp.exp(-x))


def _dot(a, b):
    return jnp.dot(a, b, preferred_element_type=F32)


def _fused_linear(xs, ws, *, tm, epilogue, out_shapes, out_specs, gain=None, wspecs=None,
                  row_extras=(), const_extras=(), vmem_mb=48, name="linear"):
    T = xs[0].shape[0]
    grid = (T // tm,)
    n_x, n_w, n_r, n_c = len(xs), len(ws), len(row_extras), len(const_extras)

    def kern(*refs):
        pos = 0
        x_refs = refs[pos:pos + n_x]; pos += n_x
        gain_ref = None
        if gain is not None:
            gain_ref = refs[pos]; pos += 1
        w_refs = refs[pos:pos + n_w]; pos += n_w
        r_refs = refs[pos:pos + n_r]; pos += n_r
        c_refs = refs[pos:pos + n_c]; pos += n_c
        o_refs = refs[pos:]
        acc = None
        x0 = None
        for xr, wr in zip(x_refs, w_refs):
            xv = xr[...]
            if gain_ref is not None:
                x0 = xv
                xv = _rms(xv, gain_ref[...]).astype(BF16)
            d = _dot(xv, wr[...])
            acc = d if acc is None else acc + d
        epilogue(acc, x0, r_refs, c_refs, o_refs)

    in_specs = [pl.BlockSpec((tm, x.shape[1]), lambda i: (i, 0)) for x in xs]
    args = list(xs)
    if gain is not None:
        in_specs.append(pl.BlockSpec(gain.shape, lambda i: (0, 0)))
        args.append(gain)
    if wspecs is None:
        wspecs = [pl.BlockSpec(w.shape, lambda i: (0,) * w.ndim) for w in ws]
    in_specs += list(wspecs)
    args += list(ws)
    for r in row_extras:
        in_specs.append(pl.BlockSpec((tm, r.shape[1]), lambda i: (i, 0)))
        args.append(r)
    for c in const_extras:
        in_specs.append(pl.BlockSpec(c.shape, lambda i, nd=c.ndim: (0,) * nd))
        args.append(c)
    return pl.pallas_call(
        kern, grid=grid, in_specs=in_specs, out_specs=out_specs, out_shape=out_shapes,
        compiler_params=_cparams(vmem_mb), name=name)(*args)


def _even_in_proj(h, gain, w_ext, qn_g, kvn_g, cos_rows, sin_rows, dims):
    DP, QL, KL, ROPE = dims
    T, D = h.shape
    tm = _tile(T, LINEAR_TM)
    o_kr = DP + QL + KL

    def epilogue(acc, x0, r_refs, c_refs, o_refs):
        u_ref, qn_ref, ckv_ref, kpe_ref, kcat_ref = o_refs
        cos, sin = r_refs[0][...], r_refs[1][...]
        u_ref[...] = acc[:, :DP]
        qn_ref[...] = _rms(acc[:, DP:DP + QL], c_refs[0][...]).astype(BF16)
        ckv = _rms(acc[:, DP + QL:o_kr], c_refs[1][...])
        ckv_ref[...] = ckv
        kpe = acc[:, o_kr:o_kr + LANES] * cos + acc[:, o_kr + LANES:o_kr + 2 * LANES] * sin
        kpe_ref[...] = kpe[:, :ROPE]
        kcat_ref[:, :KL] = ckv.astype(BF16)
        kcat_ref[:, KL:] = kpe.astype(BF16)

    out_shapes = (jax.ShapeDtypeStruct((T, DP), F32), jax.ShapeDtypeStruct((T, QL), BF16),
                  jax.ShapeDtypeStruct((T, KL), F32), jax.ShapeDtypeStruct((T, ROPE), F32),
                  jax.ShapeDtypeStruct((T, KL + LANES), BF16))
    out_specs = tuple(pl.BlockSpec((tm, s.shape[1]), lambda i: (i, 0)) for s in out_shapes)
    return _fused_linear([h], [w_ext], tm=tm, gain=gain, epilogue=epilogue,
                         out_shapes=out_shapes, out_specs=out_specs,
                         row_extras=(cos_rows, sin_rows), const_extras=(qn_g, kvn_g),
                         name="even_in_proj")


def _q_proj(qn, wq_ext, wuk_t, cos_rows, sin_rows, H, KL, scale, head_major):
    T = qn.shape[0]
    tm = _tile(T, LINEAR_TM)
    QW = KL + LANES
    r0, r1 = H * NOPE_DIM, 2 * H * NOPE_DIM

    def epilogue(acc, x0, r_refs, c_refs, o_refs):
        (o_ref,) = o_refs
        cos, sin = r_refs[0][...], r_refs[1][...]
        wuk_ref = c_refs[0]
        for h in range(H):
            qn_h = acc[:, h * NOPE_DIM:(h + 1) * NOPE_DIM].astype(BF16)
            q_abs = (_dot(qn_h, wuk_ref[h]) * scale).astype(BF16)
            q_pe = ((acc[:, r0 + h * LANES:r0 + (h + 1) * LANES] * cos
                     + acc[:, r1 + h * LANES:r1 + (h + 1) * LANES] * sin) * scale).astype(BF16)
            if head_major:
                o_ref[h, :, :KL] = q_abs
                o_ref[h, :, KL:] = q_pe
            else:
                o_ref[:, h * QW:h * QW + KL] = q_abs
                o_ref[:, h * QW + KL:(h + 1) * QW] = q_pe

    if head_major:
        out_shape = jax.ShapeDtypeStruct((H, T, QW), BF16)
        out_spec = pl.BlockSpec((H, tm, QW), lambda i: (0, i, 0))
    else:
        out_shape = jax.ShapeDtypeStruct((T, H * QW), BF16)
        out_spec = pl.BlockSpec((tm, H * QW), lambda i: (i, 0))
    (out,) = _fused_linear([qn], [wq_ext], tm=tm, epilogue=epilogue, out_shapes=(out_shape,),
                           out_specs=(out_spec,), row_extras=(cos_rows, sin_rows),
                           const_extras=(wuk_t,), name="q_proj")
    return out


def _pool_group_out(ext_ref, base, tl, pos0, g, w, PG, pw_ref, scale_ref):
    lo, hi = g * PG, (g + 1) * PG
    x = ext_ref[base:base + tl, lo:hi]
    win = x
    for j in range(1, w):
        win = win + ext_ref[base - j:base - j + tl, lo:hi]
    if pos0 is None:
        d = win * (1.0 / w) - x
    else:
        pos = pos0 + lax.broadcasted_iota(jnp.int32, (tl, 1), 0)
        cnt = jnp.minimum(pos + 1, w).astype(F32)
        d = win / cnt - x
    return _dot(d.astype(BF16), pw_ref[g]) * scale_ref[:, lo:hi]


def _pool_prompt(u, past, pool_w, pool_scale, B, L):
    DP = u.shape[1]
    PS = past.shape[1]
    HALO = 16
    G = len(POOL_WINDOWS)
    PG = DP // G
    tl = _tile(L, POOL_TL, HALO)
    nl = L // tl

    def kern(u_ref, halo_ref, past_ref, pw_ref, sc_ref, o_ref, new_ref, ext_ref):
        i = pl.program_id(1)

        @pl.when(i == 0)
        def _():
            ext_ref[0:1, :] = jnp.zeros((1, DP), F32)
            ext_ref[HALO - PS:HALO, :] = past_ref[0]

        @pl.when(i > 0)
        def _():
            ext_ref[0:HALO, :] = halo_ref[...]

        ext_ref[HALO:HALO + tl, :] = u_ref[...]
        for g, w in enumerate(POOL_WINDOWS):
            o_ref[:, g * PG:(g + 1) * PG] = _pool_group_out(
                ext_ref, HALO, tl, i * tl, g, w, PG, pw_ref, sc_ref).astype(BF16)

        @pl.when(i == nl - 1)
        def _():
            new_ref[0] = ext_ref[HALO + tl - PS:HALO + tl, :]

    r = tl // HALO
    return pl.pallas_call(
        kern, grid=(B, nl),
        in_specs=[
            pl.BlockSpec((tl, DP), lambda b, i: (b * nl + i, 0)),
            pl.BlockSpec((HALO, DP), lambda b, i: (jnp.maximum((b * nl + i) * r - 1, 0), 0)),
            pl.BlockSpec((1, PS, DP), lambda b, i: (b, 0, 0)),
            pl.BlockSpec(pool_w.shape, lambda b, i: (0, 0, 0)),
            pl.BlockSpec(pool_scale.shape, lambda b, i: (0, 0)),
        ],
        out_specs=(pl.BlockSpec((tl, DP), lambda b, i: (b * nl + i, 0)),
                   pl.BlockSpec((1, PS, DP), lambda b, i: (b, 0, 0))),
        out_shape=(jax.ShapeDtypeStruct((B * L, DP), BF16), jax.ShapeDtypeStruct((B, PS, DP), F32)),
        scratch_shapes=[pltpu.VMEM((HALO + tl, DP), F32)],
        compiler_params=_cparams(40), name="pool_prompt")(u, u, past, pool_w, pool_scale)


def _pool_sample(ext_tm, pool_w, pool_scale, Ld, start):
    R, DB, DP = ext_tm.shape
    PS = R - Ld
    G = len(POOL_WINDOWS)
    PG = DP // G

    def kern(e_ref, pw_ref, sc_ref, o_ref):
        for t in range(Ld):
            for g, w in enumerate(POOL_WINDOWS):
                lo, hi = g * PG, (g + 1) * PG
                x = e_ref[PS + t, :, lo:hi]
                win = x
                for j in range(1, w):
                    win = win + e_ref[PS + t - j, :, lo:hi]
                cnt = float(min(start + t + 1, w))
                d = win / cnt - x
                o_ref[t, :, lo:hi] = (_dot(d.astype(BF16), pw_ref[g]) * sc_ref[:, lo:hi]).astype(BF16)

    return pl.pallas_call(
        kern, grid=(1,),
        in_specs=[pl.BlockSpec(ext_tm.shape, lambda i: (0, 0, 0)),
                  pl.BlockSpec(pool_w.shape, lambda i: (0, 0, 0)),
                  pl.BlockSpec(pool_scale.shape, lambda i: (0, 0))],
        out_specs=pl.BlockSpec((Ld, DB, DP), lambda i: (0, 0, 0)),
        out_shape=jax.ShapeDtypeStruct((Ld, DB, DP), BF16),
        compiler_params=_cparams(48), name="pool_sample")(ext_tm, pool_w, pool_scale)


def _flash_prompt(qcat, kcat, wuv, B, L, H, KL):
    QW = qcat.shape[2]
    tq = _tile(L, FLASH_TQ, 16)
    tk = _tile(L, FLASH_TK, 16)
    nq, nk = L // tq, L // tk
    R = H * tq

    def last_needed(qi):
        return (qi * tq + tq - 1) // tk

    def kern(q_ref, k_ref, wuv_ref, o_ref, m_sc, l_sc, acc_sc):
        qi, ki = pl.program_id(1), pl.program_id(2)

        @pl.when(ki == 0)
        def _():
            m_sc[...] = jnp.full((R, 1), NEG_BIG, F32)
            l_sc[...] = jnp.zeros((R, 1), F32)
            acc_sc[...] = jnp.zeros((R, KL), F32)

        @pl.when(ki <= last_needed(qi))
        def _():
            q = q_ref[...].reshape(R, QW)
            k = k_ref[...]
            s = lax.dot_general(q, k, NT_DIMS, preferred_element_type=F32)
            qpos = qi * tq + (lax.broadcasted_iota(jnp.int32, (R, tk), 0) % tq)
            kpos = ki * tk + lax.broadcasted_iota(jnp.int32, (R, tk), 1)
            s = jnp.where(kpos <= qpos, s, NEG_BIG)
            m_prev = m_sc[...]
            m_new = jnp.maximum(m_prev, jnp.max(s, axis=-1, keepdims=True))
            alpha = jnp.exp(m_prev - m_new)
            p = jnp.exp(s - m_new)
            l_sc[...] = alpha * l_sc[...] + jnp.sum(p, axis=-1, keepdims=True)
            acc_sc[...] = alpha * acc_sc[...] + _dot(p.astype(BF16), k[:, :KL])
            m_sc[...] = m_new

        @pl.when(ki == last_needed(qi))
        def _():
            ctx = (acc_sc[...] / l_sc[...]).astype(BF16)
            for h in range(H):
                o_ref[:, h * V_DIM:(h + 1) * V_DIM] = _dot(
                    ctx[h * tq:(h + 1) * tq], wuv_ref[h]).astype(BF16)

    return pl.pallas_call(
        kern, grid=(B, nq, nk),
        in_specs=[
            pl.BlockSpec((H, tq, QW), lambda b, qi, ki: (0, b * nq + qi, 0)),
            pl.BlockSpec((tk, QW), lambda b, qi, ki: (b * nk + jnp.minimum(ki, last_needed(qi)), 0)),
            pl.BlockSpec(wuv.shape, lambda b, qi, ki: (0, 0, 0)),
        ],
        out_specs=pl.BlockSpec((tq, H * V_DIM), lambda b, qi, ki: (b * nq + qi, 0)),
        out_shape=jax.ShapeDtypeStruct((B * L, H * V_DIM), BF16),
        scratch_shapes=[pltpu.VMEM((R, 1), F32), pltpu.VMEM((R, 1), F32), pltpu.VMEM((R, KL), F32)],
        compiler_params=_cparams(56), name="flash_prompt")(qcat, kcat, wuv)


def _decode_attention(page_table, q_s, knew, cache_ckv, cache_krope, a, H, Ld, KL, ROPE):
    DB, R, QW = q_s.shape
    KN = knew.shape[1]
    n_pages = page_table.shape[1]
    PG = cache_ckv.shape[2]
    PP = DECODE_PAGES_PER_STEP
    while n_pages % PP:
        PP //= 2
    nj = n_pages // PP

    def kern(pt_ref, q_ref, kn_ref, *rest):
        ck_refs, kr_refs = rest[:PP], rest[PP:2 * PP]
        o_ref, m_sc, l_sc, acc_sc = rest[2 * PP:]
        j = pl.program_id(1)

        @pl.when(j == 0)
        def _():
            m_sc[...] = jnp.full((R, 1), NEG_BIG, F32)
            l_sc[...] = jnp.zeros((R, 1), F32)
            acc_sc[...] = jnp.zeros((R, KL), F32)

        q = q_ref[0]
        qa, qp = q[:, :KL], q[:, KL:KL + ROPE]

        def update(s, v):
            m_prev = m_sc[...]
            m_new = jnp.maximum(m_prev, jnp.max(s, axis=-1, keepdims=True))
            alpha = jnp.exp(m_prev - m_new)
            p = jnp.exp(s - m_new)
            l_sc[...] = alpha * l_sc[...] + jnp.sum(p, axis=-1, keepdims=True)
            acc_sc[...] = alpha * acc_sc[...] + _dot(p.astype(BF16), v)
            m_sc[...] = m_new

        for p_i in range(PP):
            ck = ck_refs[p_i][...].astype(BF16)
            kr = kr_refs[p_i][...].astype(BF16)
            s = (lax.dot_general(qa, ck, NT_DIMS, preferred_element_type=F32)
                 + lax.dot_general(qp, kr, NT_DIMS, preferred_element_type=F32))
            update(s, ck)

        @pl.when(j == nj - 1)
        def _():
            kn = kn_ref[0]
            s = lax.dot_general(q, kn, NT_DIMS, preferred_element_type=F32)
            t_idx = lax.broadcasted_iota(jnp.int32, (R, KN), 0) // H
            s_idx = lax.broadcasted_iota(jnp.int32, (R, KN), 1)
            s = jnp.where(s_idx <= t_idx, s, NEG_BIG)
            update(s, kn[:, :KL])
            o_ref[0] = (acc_sc[...] / l_sc[...]).astype(BF16)

    def page_spec(width, p_i):
        return pl.BlockSpec((None, None, PG, width),
                            lambda b, j, pt, p_i=p_i: (a, pt[b, j * PP + p_i], 0, 0))

    in_specs = [pl.BlockSpec((1, R, QW), lambda b, j, pt: (b, 0, 0)),
                pl.BlockSpec((1, KN, QW), lambda b, j, pt: (b, 0, 0))]
    in_specs += [page_spec(KL, p_i) for p_i in range(PP)]
    in_specs += [page_spec(ROPE, p_i) for p_i in range(PP)]
    grid_spec = pltpu.PrefetchScalarGridSpec(
        num_scalar_prefetch=1, grid=(DB, nj), in_specs=in_specs,
        out_specs=pl.BlockSpec((1, R, KL), lambda b, j, pt: (b, 0, 0)),
        scratch_shapes=[pltpu.VMEM((R, 1), F32), pltpu.VMEM((R, 1), F32), pltpu.VMEM((R, KL), F32)])
    return pl.pallas_call(
        kern, grid_spec=grid_spec, out_shape=jax.ShapeDtypeStruct((DB, R, KL), BF16),
        compiler_params=_cparams(40), name="decode_attention")(
            page_table, q_s, knew, *([cache_ckv] * PP), *([cache_krope] * PP))


def _ctx_to_attn_out(ctx, wuv, H, KL):
    T = ctx.shape[0]
    tm = _tile(T, LINEAR_TM)

    def kern(c_ref, w_ref, o_ref):
        for h in range(H):
            o_ref[:, h * V_DIM:(h + 1) * V_DIM] = _dot(
                c_ref[:, h * KL:(h + 1) * KL], w_ref[h]).astype(BF16)

    return pl.pallas_call(
        kern, grid=(T // tm,),
        in_specs=[pl.BlockSpec((tm, H * KL), lambda i: (i, 0)),
                  pl.BlockSpec(wuv.shape, lambda i: (0, 0, 0))],
        out_specs=pl.BlockSpec((tm, H * V_DIM), lambda i: (i, 0)),
        out_shape=jax.ShapeDtypeStruct((T, H * V_DIM), BF16),
        compiler_params=_cparams(40), name="ctx_to_attn_out")(ctx, wuv)


def _linear_residual(xs, ws, res, name):
    T, D = res.shape
    tm = _tile(T, LINEAR_TM)

    def epilogue(acc, x0, r_refs, c_refs, o_refs):
        o_refs[0][...] = r_refs[0][...] + acc

    (out,) = _fused_linear(xs, ws, tm=tm, epilogue=epilogue,
                           out_shapes=(jax.ShapeDtypeStruct((T, D), F32),),
                           out_specs=(pl.BlockSpec((tm, D), lambda i: (i, 0)),),
                           row_extras=(res,), name=name)
    return out


def _ffn(h, gain, w_gu, w_down):
    T, D = h.shape
    F = w_down.shape[0]
    tm = _tile(T, FFN_TM)
    tf = _tile(F, FFN_TF, LANES)
    nf = F // tf

    def kern(x_ref, g_ref, wg_ref, wu_ref, wd_ref, o_ref, xn_sc):
        f = pl.program_id(1)

        @pl.when(f == 0)
        def _():
            x = x_ref[...]
            xn_sc[...] = _rms(x, g_ref[...]).astype(BF16)
            o_ref[...] = x

        xn = xn_sc[...]
        act = (_silu(_dot(xn, wg_ref[...])) * _dot(xn, wu_ref[...])).astype(BF16)
        o_ref[...] += _dot(act, wd_ref[...])

    return pl.pallas_call(
        kern, grid=(T // tm, nf),
        in_specs=[pl.BlockSpec((tm, D), lambda i, f: (i, 0)),
                  pl.BlockSpec(gain.shape, lambda i, f: (0, 0)),
                  pl.BlockSpec((D, tf), lambda i, f: (0, f)),
                  pl.BlockSpec((D, tf), lambda i, f: (0, nf + f)),
                  pl.BlockSpec((tf, D), lambda i, f: (f, 0))],
        out_specs=pl.BlockSpec((tm, D), lambda i, f: (i, 0)),
        out_shape=jax.ShapeDtypeStruct((T, D), F32),
        scratch_shapes=[pltpu.VMEM((tm, D), BF16)],
        compiler_params=_cparams(48), name="ffn")(h, gain, w_gu, w_gu, w_down)


def _hgrn_in_proj(h, gain, w_in, lb):
    T, D = h.shape
    HK = w_in.shape[1] // 4
    tm = _tile(T, LINEAR_TM)

    def call(col, epilogue, dtypes, extras=()):
        out_shapes = tuple(jax.ShapeDtypeStruct((T, HK), dt) for dt in dtypes)
        out_specs = tuple(pl.BlockSpec((tm, HK), lambda i: (i, 0)) for _ in dtypes)
        return _fused_linear([h], [w_in], tm=tm, gain=gain, epilogue=epilogue,
                             wspecs=[pl.BlockSpec((D, HK), lambda i: (0, col))],
                             out_shapes=out_shapes, out_specs=out_specs, const_extras=extras,
                             name=f"hgrn_in_proj_{col}")

    def ep_silu(acc, x0, r_refs, c_refs, o_refs):
        o_refs[0][...] = _silu(acc).astype(BF16)

    def ep_ident(acc, x0, r_refs, c_refs, o_refs):
        o_refs[0][...] = acc.astype(BF16)

    def ep_gate(acc, x0, r_refs, c_refs, o_refs):
        k = (1.0 - c_refs[0][...]) / (1.0 + jnp.exp(acc))
        o_refs[0][...] = k.astype(BF16)
        o_refs[1][...] = jnp.log1p(-k)

    (qs,) = call(0, ep_silu, (BF16,))
    kk, logf = call(1, ep_gate, (BF16, F32), extras=(lb,))
    (vv,) = call(2, ep_ident, (BF16,))
    (gs,) = call(3, ep_silu, (BF16,))
    return qs, kk, logf, vv, gs


def _gla_band(qf, kf, vf, fg, n_diag, lane_sum):
    C = qf.shape[0]
    r8 = lax.broadcasted_iota(jnp.int32, qf.shape, 0) % SUBLANES
    gprod = None
    parts = []
    for d in range(n_diag):
        if d == 0:
            p = qf * kf
        else:
            fr = fg if d == 1 else pltpu.roll(fg, d - 1, 0)
            gprod = fr if gprod is None else gprod * fr
            p = jnp.where(r8 >= d, qf * gprod * pltpu.roll(kf, d, 0), 0.0)
        parts.append(p)
    sums = lane_sum(parts)
    out = None
    for d in range(n_diag):
        vr = vf if d == 0 else pltpu.roll(vf, d, 0)
        term = sums[d] * vr
        out = term if out is None else out + term
    return out


def _gla_prompt(qs, kk, logf, vv, gs, hg_norm, B, L, H):
    T, HK = qs.shape
    K = HK // H
    ct = _tile(L, GLA_CT, GLA_CHUNK)
    C = min(GLA_CHUNK, ct)
    nct = L // ct
    ncc = ct // C

    def kern(q_ref, k_ref, lf_ref, v_ref, g_ref, hn_ref, o_ref, s_ref, st_sc):
        ci = pl.program_id(2)

        @pl.when(ci == 0)
        def _():
            st_sc[...] = jnp.zeros((K, K), F32)

        rows = lax.broadcasted_iota(jnp.int32, (C, C), 0)
        cols = lax.broadcasted_iota(jnp.int32, (C, C), 1)
        tril = (rows >= cols).astype(F32)
        ones_bf = jnp.ones((K, K), BF16)

        def lane_sum(parts):
            r = _dot(jnp.concatenate([p.astype(BF16) for p in parts], axis=0), ones_bf)
            return [r[d * C:(d + 1) * C] for d in range(len(parts))]

        st = st_sc[...]
        for c in range(ncc):
            sl = slice(c * C, (c + 1) * C)
            qf, kf, vf = q_ref[sl, :].astype(F32), k_ref[sl, :].astype(F32), v_ref[sl, :].astype(F32)
            lf = lf_ref[sl, :]
            fg = jnp.exp(lf)
            b = jnp.dot(tril, lf, precision=lax.Precision.HIGHEST, preferred_element_type=F32)
            blast = b[C - 1:C, :]
            qb = qf * jnp.exp(b)
            kb = kf * jnp.exp(blast - b)
            o = lax.dot_general(qb.astype(BF16), st.astype(BF16), NT_DIMS, preferred_element_type=F32)
            amat = None
            blk = C // 2
            while blk >= SUBLANES:
                q_parts, k_parts = [], []
                for m in range(C // blk):
                    r = slice(m * blk, (m + 1) * blk)
                    if m % 2 == 1:
                        ref = b[m * blk - 1:m * blk, :]
                        q_parts.append(qf[r] * jnp.exp(b[r] - ref))
                        k_parts.append(jnp.zeros((blk, K), F32))
                    else:
                        ref = b[(m + 1) * blk - 1:(m + 1) * blk, :]
                        q_parts.append(jnp.zeros((blk, K), F32))
                        k_parts.append(kf[r] * jnp.exp(ref - b[r]))
                ql = jnp.concatenate(q_parts, axis=0).astype(BF16)
                kl = jnp.concatenate(k_parts, axis=0).astype(BF16)
                al = lax.dot_general(ql, kl, NT_DIMS, preferred_element_type=F32)
                al = jnp.where((rows // (2 * blk)) == (cols // (2 * blk)), al, 0.0)
                amat = al if amat is None else amat + al
                blk //= 2
            if amat is not None:
                o = o + _dot(amat.astype(BF16), vf.astype(BF16))
            o = o + _gla_band(qf, kf, vf, fg, min(SUBLANES, C), lane_sum)
            st = st * jnp.exp(blast) + lax.dot_general(
                vf.astype(BF16), kb.astype(BF16), TN_DIMS, preferred_element_type=F32)
            on = _rms(o, hn_ref[...]) * g_ref[sl, :].astype(F32)
            o_ref[sl, :] = on.astype(BF16)
        st_sc[...] = st

        @pl.when(ci == nct - 1)
        def _():
            s_ref[0, 0] = st.T

    tok = lambda b, h, ci: (b * nct + ci, h)
    return pl.pallas_call(
        kern, grid=(B, H, nct),
        in_specs=[pl.BlockSpec((ct, K), tok)] * 5 + [pl.BlockSpec(hg_norm.shape, lambda b, h, ci: (0, 0))],
        out_specs=(pl.BlockSpec((ct, K), tok),
                   pl.BlockSpec((1, 1, K, K), lambda b, h, ci: (b, h, 0, 0))),
        out_shape=(jax.ShapeDtypeStruct((T, HK), BF16), jax.ShapeDtypeStruct((B, H, K, K), F32)),
        scratch_shapes=[pltpu.VMEM((K, K), F32)],
        compiler_params=_cparams(32), name="gla_prompt")(qs, kk, logf, vv, gs, hg_norm)


def _gla_sample(qs, kk, logf, vv, gs, hg_norm, s0, DB, Ld, H):
    HK = qs.shape[2]
    K = HK // H
    C = SUBLANES
    assert Ld <= C

    def kern(q_ref, k_ref, lf_ref, v_ref, g_ref, hn_ref, s0_ref, o_ref, s_ref, pad_sc):
        rows = lax.broadcasted_iota(jnp.int32, (C, C), 0)
        cols = lax.broadcasted_iota(jnp.int32, (C, C), 1)
        tril = (rows >= cols).astype(F32)

        def padded(ref, slot):
            pad_sc[slot] = jnp.zeros((C, HK), F32)
            pad_sc[slot, 0:Ld, :] = ref[0].astype(F32)
            return pad_sc[slot]

        qf, kf, lf, vf = padded(q_ref, 0), padded(k_ref, 1), padded(lf_ref, 2), padded(v_ref, 3)
        fg = jnp.exp(lf)
        b = jnp.dot(tril, lf, precision=lax.Precision.HIGHEST, preferred_element_type=F32)
        blast = b[C - 1:C, :]
        qb = (qf * jnp.exp(b)).astype(BF16)
        kb = (kf * jnp.exp(blast - b)).astype(BF16)
        dec = jnp.exp(blast)
        vb = vf.astype(BF16)

        def lane_sum(parts):
            return parts

        for h in range(H):
            hl = slice(h * K, (h + 1) * K)

            def head_sum(parts):
                return [jnp.sum(p, axis=-1, keepdims=True) for p in parts]

            st = s0_ref[0, h].T
            o = lax.dot_general(qb[:, hl], st.astype(BF16), NT_DIMS, preferred_element_type=F32)
            o = o + _gla_band(qf[:, hl], kf[:, hl], vf[:, hl], fg[:, hl], Ld, head_sum)
            st = st * dec[:, hl] + lax.dot_general(vb[:, hl], kb[:, hl], TN_DIMS,
                                                   preferred_element_type=F32)
            s_ref[0, h] = st.T
            on = _rms(o[0:Ld], hn_ref[...]) * g_ref[0, :, hl].astype(F32)
            o_ref[0, :, hl] = on.astype(BF16)

    tok = pl.BlockSpec((1, Ld, HK), lambda b: (b, 0, 0))
    st_spec = pl.BlockSpec((1, H, K, K), lambda b: (b, 0, 0, 0))
    return pl.pallas_call(
        kern, grid=(DB,),
        in_specs=[tok] * 5 + [pl.BlockSpec(hg_norm.shape, lambda b: (0, 0)), st_spec],
        out_specs=(tok, st_spec),
        out_shape=(jax.ShapeDtypeStruct((DB, Ld, HK), BF16), jax.ShapeDtypeStruct(s0.shape, F32)),
        scratch_shapes=[pltpu.VMEM((4, C, HK), F32)],
        compiler_params=_cparams(32), name="gla_sample")(qs, kk, logf, vv, gs, hg_norm, s0)


def _router(h, gain, w_router, b_router):
    T, D = h.shape
    E = w_router.shape[1]
    tm = _tile(T, LINEAR_TM)

    def kern(x_ref, g_ref, w_ref, b_ref, o_ref):
        xn = _rms(x_ref[...], g_ref[...])
        logits = jnp.dot(xn, w_ref[...], precision=lax.Precision.HIGHEST,
                         preferred_element_type=F32) + b_ref[...]
        idx = lax.broadcasted_iota(jnp.int32, logits.shape, 1)
        m1 = jnp.max(logits, axis=-1, keepdims=True)
        i1 = jnp.min(jnp.where(logits == m1, idx, E), axis=-1, keepdims=True)
        rest = jnp.where(idx == i1, -jnp.inf, logits)
        m2 = jnp.max(rest, axis=-1, keepdims=True)
        i2 = jnp.min(jnp.where(rest == m2, idx, E), axis=-1, keepdims=True)
        e2 = jnp.exp(m2 - m1)
        g1 = 1.0 / (1.0 + e2)
        o_ref[...] = jnp.where(idx == i1, g1, 0.0) + jnp.where(idx == i2, e2 * g1, 0.0)

    return pl.pallas_call(
        kern, grid=(T // tm,),
        in_specs=[pl.BlockSpec((tm, D), lambda i: (i, 0)),
                  pl.BlockSpec(gain.shape, lambda i: (0, 0)),
                  pl.BlockSpec(w_router.shape, lambda i: (0, 0)),
                  pl.BlockSpec(b_router.shape, lambda i: (0, 0))],
        out_specs=pl.BlockSpec((tm, E), lambda i: (i, 0)),
        out_shape=jax.ShapeDtypeStruct((T, E), F32),
        compiler_params=_cparams(32), name="router")(h, gain, w_router, b_router)


def _moe_dense(h, gain, comb, w_gu, w_down, final_gain):
    T, D = h.shape
    E, F = w_down.shape[0], w_down.shape[1]
    tm = _tile(T, MOE_TM)
    tf = _tile(F, MOE_TF, LANES)
    nf = F // tf

    def kern(x_ref, g_ref, c_ref, wg_ref, wu_ref, wd_ref, fg_ref, o_ref, xn_sc):
        e, f = pl.program_id(1), pl.program_id(2)

        @pl.when((e == 0) & (f == 0))
        def _():
            x = x_ref[...]
            xn_sc[...] = _rms(x, g_ref[...]).astype(BF16)
            o_ref[...] = x

        xn = xn_sc[...]
        comb_v = c_ref[...]
        lane = lax.broadcasted_iota(jnp.int32, comb_v.shape, 1)
        ce = jnp.sum(jnp.where(lane == e, comb_v, 0.0), axis=-1, keepdims=True)
        act = (_silu(_dot(xn, wg_ref[0])) * _dot(xn, wu_ref[0]) * ce).astype(BF16)
        o_ref[...] += _dot(act, wd_ref[0])

        @pl.when((e == E - 1) & (f == nf - 1))
        def _():
            o_ref[...] = _rms(o_ref[...], fg_ref[...])

    return pl.pallas_call(
        kern, grid=(T // tm, E, nf),
        in_specs=[pl.BlockSpec((tm, D), lambda i, e, f: (i, 0)),
                  pl.BlockSpec(gain.shape, lambda i, e, f: (0, 0)),
                  pl.BlockSpec((tm, E), lambda i, e, f: (i, 0)),
                  pl.BlockSpec((1, D, tf), lambda i, e, f: (e, 0, f)),
                  pl.BlockSpec((1, D, tf), lambda i, e, f: (e, 0, nf + f)),
                  pl.BlockSpec((1, tf, D), lambda i, e, f: (e, f, 0)),
                  pl.BlockSpec(final_gain.shape, lambda i, e, f: (0, 0))],
        out_specs=pl.BlockSpec((tm, D), lambda i, e, f: (i, 0)),
        out_shape=jax.ShapeDtypeStruct((T, D), F32),
        scratch_shapes=[pltpu.VMEM((tm, D), BF16)],
        compiler_params=_cparams(48), name="moe_dense")(h, gain, comb, w_gu, w_gu, w_down, final_gain)


def _rope_rows(pos, rope_dim):
    inv = ROPE_THETA ** (-jnp.arange(0, rope_dim, 2, dtype=F32) / rope_dim)
    ang = pos.astype(F32)[:, None] * inv[None, :]
    z = jnp.zeros((pos.shape[0], LANES - rope_dim), F32)
    cos, sin = jnp.cos(ang), jnp.sin(ang)
    return jnp.concatenate([cos, cos, z], axis=-1), jnp.concatenate([sin, sin, z], axis=-1)


def _rot_cols(w):
    half = w.shape[-1] // 2
    return jnp.concatenate([-w[..., half:], w[..., :half]], axis=-1)


def _pad_lanes(w):
    pad = LANES - w.shape[-1]
    return jnp.pad(w, [(0, 0)] * (w.ndim - 1) + [(0, pad)])


def _prep_even(a, w_in_e, w_q_b, w_kv_b, pool_w, w_out_e, w_ffn_gu, w_ffn_down, dims, H):
    DP, QL, KL, ROPE = dims
    w_in = w_in_e[a]
    k_raw = w_in[:, DP + QL + KL:]
    w_ext = jnp.concatenate([w_in[:, :DP + QL + KL], _pad_lanes(k_raw), _pad_lanes(_rot_cols(k_raw))],
                            axis=-1).astype(BF16)
    wq = w_q_b[a].reshape(QL, H, NOPE_DIM + ROPE)
    wq_rope = wq[..., NOPE_DIM:]
    wq_ext = jnp.concatenate([wq[..., :NOPE_DIM].reshape(QL, H * NOPE_DIM),
                              _pad_lanes(wq_rope).reshape(QL, H * LANES),
                              _pad_lanes(_rot_cols(wq_rope)).reshape(QL, H * LANES)], axis=-1).astype(BF16)
    w_kv = w_kv_b[a].reshape(KL, H, NOPE_DIM + V_DIM)
    wuk_t = jnp.transpose(w_kv[..., :NOPE_DIM], (1, 2, 0)).astype(BF16)
    wuv = jnp.transpose(w_kv[..., NOPE_DIM:], (1, 0, 2)).astype(BF16)
    w_out = w_out_e[a].astype(BF16)
    return dict(w_ext=w_ext, wq_ext=wq_ext, wuk_t=wuk_t, wuv=wuv, pool_w=pool_w[a].astype(BF16),
                w_out_pool=w_out[:DP], w_out_attn=w_out[DP:],
                w_gu=w_ffn_gu[a].astype(BF16), w_down=w_ffn_down[a].astype(BF16))


def kernel(x_prompt, x_sample, cache_ckv, cache_krope, page_table, state_pool, state_hgrn,
           norm_mix_e, w_in_e, q_norm, w_q_b, kv_norm, w_kv_b, pool_w, pool_scale, w_out_e,
           norm_ffn_e, w_ffn_gu, w_ffn_down,
           norm_mix_o, w_in_o, hg_lower_bound, hg_norm, w_out_o, norm_ffn_o, w_router, b_router,
           w_exp_gu, w_exp_down, final_norm):
    B, L, D = x_prompt.shape
    DB, Ld, _ = x_sample.shape
    n_pages = page_table.shape[1]
    PG = cache_ckv.shape[2]
    past_len = n_pages * PG
    KL, ROPE = cache_ckv.shape[3], cache_krope.shape[3]
    DP, PS = state_pool.shape[3], state_pool.shape[2]
    QL = q_norm.shape[1]
    H = w_q_b.shape[2] // (NOPE_DIM + ROPE)
    HG = state_hgrn.shape[2]
    depth = hg_lower_bound.shape[0]
    dims = (DP, QL, KL, ROPE)
    scale = float((NOPE_DIM + ROPE) ** -0.5)
    QW = KL + LANES
    KN = 16
    assert Ld <= KN and ROPE <= LANES

    row = lambda v: v.reshape(1, -1).astype(F32)
    lb_p = jax.nn.softmax(hg_lower_bound.astype(F32), axis=0)
    lower_bounds = jnp.cumsum(lb_p, axis=0) - lb_p[0]

    hp = x_prompt.reshape(B * L, D)
    hs = x_sample.reshape(DB * Ld, D)
    cos_p, sin_p = _rope_rows(jnp.tile(jnp.arange(L), B), ROPE)
    cos_s, sin_s = _rope_rows(jnp.tile(past_len + jnp.arange(Ld), DB), ROPE)

    outs_p = dict(ckv=[], krope=[], pool=[], hgrn=[])
    outs_s = dict(ckv=[], krope=[], pool=[], hgrn=[])
    for l in range(depth):
        a = l // 2
        if l % 2 == 0:
            w = _prep_even(a, w_in_e, w_q_b, w_kv_b, pool_w, w_out_e, w_ffn_gu, w_ffn_down, dims, H)
            g_mix, g_q, g_kv = row(norm_mix_e[a]), row(q_norm[a]), row(kv_norm[a])
            g_ffn, p_scale = row(norm_ffn_e[a]), row(pool_scale[a])

            u, qn, ckv, kpe, kcat = _even_in_proj(hp, g_mix, w["w_ext"], g_q, g_kv, cos_p, sin_p, dims)
            pool_out, pool_new = _pool_prompt(u, jnp.zeros((B, PS, DP), F32), w["pool_w"], p_scale, B, L)
            qcat = _q_proj(qn, w["wq_ext"], w["wuk_t"], cos_p, sin_p, H, KL, scale, head_major=True)
            attn = _flash_prompt(qcat, kcat, w["wuv"], B, L, H, KL)
            hp = _linear_residual([pool_out, attn], [w["w_out_pool"], w["w_out_attn"]], hp, "out_proj_e")
            hp = _ffn(hp, g_ffn, w["w_gu"], w["w_down"])
            outs_p["ckv"].append(ckv.reshape(B, L, KL))
            outs_p["krope"].append(kpe.reshape(B, L, ROPE))
            outs_p["pool"].append(pool_new)

            u, qn, ckv, kpe, kcat = _even_in_proj(hs, g_mix, w["w_ext"], g_q, g_kv, cos_s, sin_s, dims)
            u3 = u.reshape(DB, Ld, DP)
            ext_tm = jnp.transpose(jnp.concatenate([state_pool[a], u3], axis=1), (1, 0, 2))
            pool_tm = _pool_sample(ext_tm, w["pool_w"], p_scale, Ld, past_len)
            pool_out = jnp.transpose(pool_tm, (1, 0, 2)).reshape(DB * Ld, DP)
            qrows = _q_proj(qn, w["wq_ext"], w["wuk_t"], cos_s, sin_s, H, KL, scale, head_major=False)
            q_s = qrows.reshape(DB, Ld * H, QW)
            knew = jnp.pad(kcat.reshape(DB, Ld, QW), ((0, 0), (0, KN - Ld), (0, 0)))
            ctx = _decode_attention(page_table, q_s, knew, cache_ckv, cache_krope, a, H, Ld, KL, ROPE)
            attn = _ctx_to_attn_out(ctx.reshape(DB * Ld, H * KL), w["wuv"], H, KL)
            hs = _linear_residual([pool_out, attn], [w["w_out_pool"], w["w_out_attn"]], hs, "out_proj_e")
            hs = _ffn(hs, g_ffn, w["w_gu"], w["w_down"])
            outs_s["ckv"].append(ckv.reshape(DB, Ld, KL))
            outs_s["krope"].append(kpe.reshape(DB, Ld, ROPE))
            outs_s["pool"].append(jnp.concatenate([state_pool[a], u3], axis=1)[:, -PS:])
        else:
            w_in = w_in_o[a].astype(BF16)
            w_out = w_out_o[a].astype(BF16)
            w_gu = w_exp_gu[a].astype(BF16)
            w_dn = w_exp_down[a].astype(BF16)
            g_mix, g_ffn, g_hn = row(norm_mix_o[a]), row(norm_ffn_o[a]), row(hg_norm[a])
            lb = row(lower_bounds[l])
            w_r, b_r = w_router[a].astype(F32), row(b_router[a])
            g_fin = row(final_norm) if l == depth - 1 else None
            assert g_fin is not None

            qs, kk, logf, vv, gs = _hgrn_in_proj(hp, g_mix, w_in, lb)
            on, s_new = _gla_prompt(qs, kk, logf, vv, gs, g_hn, B, L, HG)
            hp = _linear_residual([on], [w_out], hp, "out_proj_o")
            hp = _moe_dense(hp, g_ffn, _router(hp, g_ffn, w_r, b_r), w_gu, w_dn, g_fin)
            outs_p["hgrn"].append(s_new)

            qs, kk, logf, vv, gs = _hgrn_in_proj(hs, g_mix, w_in, lb)
            r3 = lambda t: t.reshape(DB, Ld, -1)
            on, s_new = _gla_sample(r3(qs), r3(kk), r3(logf), r3(vv), r3(gs), g_hn, state_hgrn[a], DB, Ld, HG)
            hs = _linear_residual([on.reshape(DB * Ld, -1)], [w_out], hs, "out_proj_o")
            hs = _moe_dense(hs, g_ffn, _router(hs, g_ffn, w_r, b_r), w_gu, w_dn, g_fin)
            outs_s["hgrn"].append(s_new)

    return (hp.reshape(B, L, D), hs.reshape(DB, Ld, D),
            jnp.stack(outs_p["ckv"]), jnp.stack(outs_p["krope"]), jnp.stack(outs_p["pool"]),
            jnp.stack(outs_p["hgrn"]),
            jnp.stack(outs_s["ckv"]), jnp.stack(outs_s["krope"]), jnp.stack(outs_s["pool"]),
            jnp.stack(outs_s["hgrn"]))
```

```python
import functools
import math

import jax
import jax.numpy as jnp
from jax import lax
from jax.experimental import pallas as pl
from jax.experimental.pallas import tpu as pltpu

F32 = jnp.float32
BF16 = jnp.bfloat16

EPS = 1e-6
POOL_WINDOWS = (2, 4, 8, 16)
NOPE_DIM = 128
V_DIM = 128
ROPE_THETA = 10000.0
TOP_K = 2
LANES = 128
SUBLANES = 8
NEG_BIG = -1e30

LINEAR_TM = 512
FFN_TM = 512
FFN_TF = 512
MOE_TM = 512
MOE_TF = 256
FLASH_TQ = 128
FLASH_TK = 512
POOL_TL = 512
GLA_CT = 256
GLA_CHUNK = 64
DECODE_PAGES_PER_STEP = 16

NT_DIMS = (((1,), (1,)), ((), ()))
TN_DIMS = (((0,), (0,)), ((), ()))


def _cparams(vmem_mb):
    return pltpu.CompilerParams(vmem_limit_bytes=vmem_mb * 2 ** 20)


def _tile(n, pref, mult=SUBLANES):
    if n <= pref:
        return n
    for t in range(pref, 0, -1):
        if n % t == 0 and t % mult == 0:
            return t
    return n


def _rms(x, g):
    return x * lax.rsqrt(jnp.mean(x * x, axis=-1, keepdims=True) + EPS) * g


def _silu(x):
    return x / (1.0 + jnp.exp(-x))


def _dot(a, b):
    return jnp.dot(a, b, preferred_element_type=F32)


def _fused_linear(xs, ws, *, tm, epilogue, out_shapes, out_specs, gain=None, wspecs=None,
                  row_extras=(), const_extras=(), vmem_mb=48, name="linear"):
    T = xs[0].shape[0]
    grid = (T // tm,)
    n_x, n_w, n_r, n_c = len(xs), len(ws), len(row_extras), len(const_extras)

    def kern(*refs):
        pos = 0
        x_refs = refs[pos:pos + n_x]; pos += n_x
        gain_ref = None
        if gain is not None:
            gain_ref = refs[pos]; pos += 1
        w_refs = refs[pos:pos + n_w]; pos += n_w
        r_refs = refs[pos:pos + n_r]; pos += n_r
        c_refs = refs[pos:pos + n_c]; pos += n_c
        o_refs = refs[pos:]
        acc = None
        x0 = None
        for xr, wr in zip(x_refs, w_refs):
            xv = xr[...]
            if gain_ref is not None:
                x0 = xv
                xv = _rms(xv, gain_ref[...]).astype(BF16)
            d = _dot(xv, wr[...])
            acc = d if acc is None else acc + d
        epilogue(acc, x0, r_refs, c_refs, o_refs)

    in_specs = [pl.BlockSpec((tm, x.shape[1]), lambda i: (i, 0)) for x in xs]
    args = list(xs)
    if gain is not None:
        in_specs.append(pl.BlockSpec(gain.shape, lambda i: (0, 0)))
        args.append(gain)
    if wspecs is None:
        wspecs = [pl.BlockSpec(w.shape, lambda i: (0,) * w.ndim) for w in ws]
    in_specs += list(wspecs)
    args += list(ws)
    for r in row_extras:
        in_specs.append(pl.BlockSpec((tm, r.shape[1]), lambda i: (i, 0)))
        args.append(r)
    for c in const_extras:
        in_specs.append(pl.BlockSpec(c.shape, lambda i, nd=c.ndim: (0,) * nd))
        args.append(c)
    return pl.pallas_call(
        kern, grid=grid, in_specs=in_specs, out_specs=out_specs, out_shape=out_shapes,
        compiler_params=_cparams(vmem_mb), name=name)(*args)


def _even_in_proj(h, gain, w_ext, qn_g, kvn_g, cos_rows, sin_rows, dims):
    DP, QL, KL, ROPE = dims
    T, D = h.shape
    tm = _tile(T, LINEAR_TM)
    o_kr = DP + QL + KL

    def epilogue(acc, x0, r_refs, c_refs, o_refs):
        u_ref, qn_ref, ckv_ref, kpe_ref, kcat_ref = o_refs
        cos, sin = r_refs[0][...], r_refs[1][...]
        u_ref[...] = acc[:, :DP]
        qn_ref[...] = _rms(acc[:, DP:DP + QL], c_refs[0][...]).astype(BF16)
        ckv = _rms(acc[:, DP + QL:o_kr], c_refs[1][...])
        ckv_ref[...] = ckv
        kpe = acc[:, o_kr:o_kr + LANES] * cos + acc[:, o_kr + LANES:o_kr + 2 * LANES] * sin
        kpe_ref[...] = kpe[:, :ROPE]
        kcat_ref[:, :KL] = ckv.astype(BF16)
        kcat_ref[:, KL:] = kpe.astype(BF16)

    out_shapes = (jax.ShapeDtypeStruct((T, DP), F32), jax.ShapeDtypeStruct((T, QL), BF16),
                  jax.ShapeDtypeStruct((T, KL), F32), jax.ShapeDtypeStruct((T, ROPE), F32),
                  jax.ShapeDtypeStruct((T, KL + LANES), BF16))
    out_specs = tuple(pl.BlockSpec((tm, s.shape[1]), lambda i: (i, 0)) for s in out_shapes)
    return _fused_linear([h], [w_ext], tm=tm, gain=gain, epilogue=epilogue,
                         out_shapes=out_shapes, out_specs=out_specs,
                         row_extras=(cos_rows, sin_rows), const_extras=(qn_g, kvn_g),
                         name="even_in_proj")


def _q_proj(qn, wq_ext, wuk_t, cos_rows, sin_rows, H, KL, scale, head_major):
    T = qn.shape[0]
    tm = _tile(T, LINEAR_TM)
    QW = KL + LANES
    r0, r1 = H * NOPE_DIM, 2 * H * NOPE_DIM

    def epilogue(acc, x0, r_refs, c_refs, o_refs):
        (o_ref,) = o_refs
        cos, sin = r_refs[0][...], r_refs[1][...]
        wuk_ref = c_refs[0]
        for h in range(H):
            qn_h = acc[:, h * NOPE_DIM:(h + 1) * NOPE_DIM].astype(BF16)
            q_abs = (_dot(qn_h, wuk_ref[h]) * scale).astype(BF16)
            q_pe = ((acc[:, r0 + h * LANES:r0 + (h + 1) * LANES] * cos
                     + acc[:, r1 + h * LANES:r1 + (h + 1) * LANES] * sin) * scale).astype(BF16)
            if head_major:
                o_ref[h, :, :KL] = q_abs
                o_ref[h, :, KL:] = q_pe
            else:
                o_ref[:, h * QW:h * QW + KL] = q_abs
                o_ref[:, h * QW + KL:(h + 1) * QW] = q_pe

    if head_major:
        out_shape = jax.ShapeDtypeStruct((H, T, QW), BF16)
        out_spec = pl.BlockSpec((H, tm, QW), lambda i: (0, i, 0))
    else:
        out_shape = jax.ShapeDtypeStruct((T, H * QW), BF16)
        out_spec = pl.BlockSpec((tm, H * QW), lambda i: (i, 0))
    (out,) = _fused_linear([qn], [wq_ext], tm=tm, epilogue=epilogue, out_shapes=(out_shape,),
                           out_specs=(out_spec,), row_extras=(cos_rows, sin_rows),
                           const_extras=(wuk_t,), name="q_proj")
    return out


def _pool_group_out(ext_ref, base, tl, pos0, g, w, PG, pw_ref, scale_ref):
    lo, hi = g * PG, (g + 1) * PG
    x = ext_ref[base:base + tl, lo:hi]
    win = x
    for j in range(1, w):
        win = win + ext_ref[base - j:base - j + tl, lo:hi]
    if pos0 is None:
        d = win * (1.0 / w) - x
    else:
        pos = pos0 + lax.broadcasted_iota(jnp.int32, (tl, 1), 0)
        cnt = jnp.minimum(pos + 1, w).astype(F32)
        d = win / cnt - x
    return _dot(d.astype(BF16), pw_ref[g]) * scale_ref[:, lo:hi]


def _pool_prompt(u, past, pool_w, pool_scale, B, L):
    DP = u.shape[1]
    PS = past.shape[1]
    HALO = 16
    G = len(POOL_WINDOWS)
    PG = DP // G
    tl = _tile(L, POOL_TL, HALO)
    nl = L // tl

    def kern(u_ref, halo_ref, past_ref, pw_ref, sc_ref, o_ref, new_ref, ext_ref):
        i = pl.program_id(1)

        @pl.when(i == 0)
        def _():
            ext_ref[0:1, :] = jnp.zeros((1, DP), F32)
            ext_ref[HALO - PS:HALO, :] = past_ref[0]

        @pl.when(i > 0)
        def _():
            ext_ref[0:HALO, :] = halo_ref[...]

        ext_ref[HALO:HALO + tl, :] = u_ref[...]
        for g, w in enumerate(POOL_WINDOWS):
            o_ref[:, g * PG:(g + 1) * PG] = _pool_group_out(
                ext_ref, HALO, tl, i * tl, g, w, PG, pw_ref, sc_ref).astype(BF16)

        @pl.when(i == nl - 1)
        def _():
            new_ref[0] = ext_ref[HALO + tl - PS:HALO + tl, :]

    r = tl // HALO
    return pl.pallas_call(
        kern, grid=(B, nl),
        in_specs=[
            pl.BlockSpec((tl, DP), lambda b, i: (b * nl + i, 0)),
            pl.BlockSpec((HALO, DP), lambda b, i: (jnp.maximum((b * nl + i) * r - 1, 0), 0)),
            pl.BlockSpec((1, PS, DP), lambda b, i: (b, 0, 0)),
            pl.BlockSpec(pool_w.shape, lambda b, i: (0, 0, 0)),
            pl.BlockSpec(pool_scale.shape, lambda b, i: (0, 0)),
        ],
        out_specs=(pl.BlockSpec((tl, DP), lambda b, i: (b * nl + i, 0)),
                   pl.BlockSpec((1, PS, DP), lambda b, i: (b, 0, 0))),
        out_shape=(jax.ShapeDtypeStruct((B * L, DP), BF16), jax.ShapeDtypeStruct((B, PS, DP), F32)),
        scratch_shapes=[pltpu.VMEM((HALO + tl, DP), F32)],
        compiler_params=_cparams(40), name="pool_prompt")(u, u, past, pool_w, pool_scale)


def _pool_sample(ext_tm, pool_w, pool_scale, Ld, start):
    R, DB, DP = ext_tm.shape
    PS = R - Ld
    G = len(POOL_WINDOWS)
    PG = DP // G

    def kern(e_ref, pw_ref, sc_ref, o_ref):
        for t in range(Ld):
            for g, w in enumerate(POOL_WINDOWS):
                lo, hi = g * PG, (g + 1) * PG
                x = e_ref[PS + t, :, lo:hi]
                win = x
                for j in range(1, w):
                    win = win + e_ref[PS + t - j, :, lo:hi]
                cnt = float(min(start + t + 1, w))
                d = win / cnt - x
                o_ref[t, :, lo:hi] = (_dot(d.astype(BF16), pw_ref[g]) * sc_ref[:, lo:hi]).astype(BF16)

    return pl.pallas_call(
        kern, grid=(1,),
        in_specs=[pl.BlockSpec(ext_tm.shape, lambda i: (0, 0, 0)),
                  pl.BlockSpec(pool_w.shape, lambda i: (0, 0, 0)),
                  pl.BlockSpec(pool_scale.shape, lambda i: (0, 0))],
        out_specs=pl.BlockSpec((Ld, DB, DP), lambda i: (0, 0, 0)),
        out_shape=jax.ShapeDtypeStruct((Ld, DB, DP), BF16),
        compiler_params=_cparams(48), name="pool_sample")(ext_tm, pool_w, pool_scale)


def _flash_prompt(qcat, kcat, wuv, B, L, H, KL):
    QW = qcat.shape[2]
    tq = _tile(L, FLASH_TQ, 16)
    tk = _tile(L, FLASH_TK, 16)
    assert tk % tq == 0
    nq, nk = L // tq, L // tk
    R = H * tq

    def last_needed(qi):
        return (qi * tq + tq - 1) // tk

    def kern(q_ref, k_ref, wuv_ref, o_ref, m_sc, l_sc, acc_sc):
        qi, ki = pl.program_id(1), pl.program_id(2)

        @pl.when(ki == 0)
        def _():
            m_sc[...] = jnp.full((R, 1), NEG_BIG, F32)
            l_sc[...] = jnp.zeros((R, 1), F32)
            acc_sc[...] = jnp.zeros((R, KL), F32)

        def update(masked):
            q = q_ref[...].reshape(R, QW)
            k = k_ref[...]
            s = lax.dot_general(q, k, NT_DIMS, preferred_element_type=F32)
            if masked:
                qpos = qi * tq + (lax.broadcasted_iota(jnp.int32, (R, tk), 0) % tq)
                kpos = ki * tk + lax.broadcasted_iota(jnp.int32, (R, tk), 1)
                s = jnp.where(kpos <= qpos, s, NEG_BIG)
            m_prev = m_sc[...]
            m_new = jnp.maximum(m_prev, jnp.max(s, axis=-1, keepdims=True))
            alpha = jnp.exp2(m_prev - m_new)
            p = jnp.exp2(s - m_new)
            l_sc[...] = alpha * l_sc[...] + jnp.sum(p, axis=-1, keepdims=True)
            acc_sc[...] = alpha * acc_sc[...] + _dot(p.astype(BF16), k[:, :KL])
            m_sc[...] = m_new

        pl.when(ki < last_needed(qi))(functools.partial(update, False))
        pl.when(ki == last_needed(qi))(functools.partial(update, True))

        @pl.when(ki == last_needed(qi))
        def _():
            ctx = (acc_sc[...] / l_sc[...]).astype(BF16)
            for h in range(H):
                o_ref[:, h * V_DIM:(h + 1) * V_DIM] = _dot(
                    ctx[h * tq:(h + 1) * tq], wuv_ref[h]).astype(BF16)

    return pl.pallas_call(
        kern, grid=(B, nq, nk),
        in_specs=[
            pl.BlockSpec((H, tq, QW), lambda b, qi, ki: (0, b * nq + qi, 0)),
            pl.BlockSpec((tk, QW), lambda b, qi, ki: (b * nk + jnp.minimum(ki, last_needed(qi)), 0)),
            pl.BlockSpec(wuv.shape, lambda b, qi, ki: (0, 0, 0)),
        ],
        out_specs=pl.BlockSpec((tq, H * V_DIM), lambda b, qi, ki: (b * nq + qi, 0)),
        out_shape=jax.ShapeDtypeStruct((B * L, H * V_DIM), BF16),
        scratch_shapes=[pltpu.VMEM((R, 1), F32), pltpu.VMEM((R, 1), F32), pltpu.VMEM((R, KL), F32)],
        compiler_params=_cparams(56), name="flash_prompt")(qcat, kcat, wuv)


def _decode_attention(page_table, q_s, knew, cache_ckv, cache_krope_t, a, H, Ld, KL, ROPE):
    DB, R, QW = q_s.shape
    KN = knew.shape[1]
    n_pages = page_table.shape[1]
    PG = cache_ckv.shape[2]
    PP = DECODE_PAGES_PER_STEP
    while n_pages % PP:
        PP //= 2
    nj = n_pages // PP

    def kern(pt_ref, q_ref, kn_ref, *rest):
        ck_refs, kr_refs = rest[:PP], rest[PP:2 * PP]
        o_ref, m_sc, l_sc, acc_sc = rest[2 * PP:]
        j = pl.program_id(1)

        @pl.when(j == 0)
        def _():
            m_sc[...] = jnp.full((R, 1), NEG_BIG, F32)
            l_sc[...] = jnp.zeros((R, 1), F32)
            acc_sc[...] = jnp.zeros((R, KL), F32)

        q = q_ref[0]
        qa, qp = q[:, :KL], q[:, KL:KL + ROPE]

        def update(s, v):
            m_prev = m_sc[...]
            m_new = jnp.maximum(m_prev, jnp.max(s, axis=-1, keepdims=True))
            alpha = jnp.exp2(m_prev - m_new)
            p = jnp.exp2(s - m_new)
            l_sc[...] = alpha * l_sc[...] + jnp.sum(p, axis=-1, keepdims=True)
            acc_sc[...] = alpha * acc_sc[...] + _dot(p.astype(BF16), v)
            m_sc[...] = m_new

        ck = jnp.concatenate([r[...].astype(BF16) for r in ck_refs], axis=0)
        kr_t = jnp.concatenate([r[...].astype(BF16) for r in kr_refs], axis=1)
        update(lax.dot_general(qa, ck, NT_DIMS, preferred_element_type=F32) + _dot(qp, kr_t), ck)

        @pl.when(j == nj - 1)
        def _():
            kn = kn_ref[0]
            s = lax.dot_general(q, kn, NT_DIMS, preferred_element_type=F32)
            t_idx = lax.broadcasted_iota(jnp.int32, (R, KN), 0) // H
            s_idx = lax.broadcasted_iota(jnp.int32, (R, KN), 1)
            s = jnp.where(s_idx <= t_idx, s, NEG_BIG)
            update(s, kn[:, :KL])
            o_ref[0] = (acc_sc[...] / l_sc[...]).astype(BF16)

    def page_spec(shape, p_i):
        return pl.BlockSpec((None, None) + shape,
                            lambda b, j, pt, p_i=p_i: (a, pt[b, j * PP + p_i], 0, 0))

    in_specs = [pl.BlockSpec((1, R, QW), lambda b, j, pt: (b, 0, 0)),
                pl.BlockSpec((1, KN, QW), lambda b, j, pt: (b, 0, 0))]
    in_specs += [page_spec((PG, KL), p_i) for p_i in range(PP)]
    in_specs += [page_spec((ROPE, PG), p_i) for p_i in range(PP)]
    grid_spec = pltpu.PrefetchScalarGridSpec(
        num_scalar_prefetch=1, grid=(DB, nj), in_specs=in_specs,
        out_specs=pl.BlockSpec((1, R, KL), lambda b, j, pt: (b, 0, 0)),
        scratch_shapes=[pltpu.VMEM((R, 1), F32), pltpu.VMEM((R, 1), F32), pltpu.VMEM((R, KL), F32)])
    return pl.pallas_call(
        kern, grid_spec=grid_spec, out_shape=jax.ShapeDtypeStruct((DB, R, KL), BF16),
        compiler_params=_cparams(40), name="decode_attention")(
            page_table, q_s, knew, *([cache_ckv] * PP), *([cache_krope_t] * PP))


def _ctx_to_attn_out(ctx, wuv, H, KL):
    T = ctx.shape[0]
    tm = _tile(T, LINEAR_TM)

    def kern(c_ref, w_ref, o_ref):
        for h in range(H):
            o_ref[:, h * V_DIM:(h + 1) * V_DIM] = _dot(
                c_ref[:, h * KL:(h + 1) * KL], w_ref[h]).astype(BF16)

    return pl.pallas_call(
        kern, grid=(T // tm,),
        in_specs=[pl.BlockSpec((tm, H * KL), lambda i: (i, 0)),
                  pl.BlockSpec(wuv.shape, lambda i: (0, 0, 0))],
        out_specs=pl.BlockSpec((tm, H * V_DIM), lambda i: (i, 0)),
        out_shape=jax.ShapeDtypeStruct((T, H * V_DIM), BF16),
        compiler_params=_cparams(40), name="ctx_to_attn_out")(ctx, wuv)


def _linear_residual(xs, ws, res, name):
    T, D = res.shape
    tm = _tile(T, LINEAR_TM)

    def epilogue(acc, x0, r_refs, c_refs, o_refs):
        o_refs[0][...] = r_refs[0][...] + acc

    (out,) = _fused_linear(xs, ws, tm=tm, epilogue=epilogue,
                           out_shapes=(jax.ShapeDtypeStruct((T, D), F32),),
                           out_specs=(pl.BlockSpec((tm, D), lambda i: (i, 0)),),
                           row_extras=(res,), name=name)
    return out


def _ffn(h, gain, w_gu, w_down):
    T, D = h.shape
    F = w_down.shape[0]
    tm = _tile(T, FFN_TM)
    tf = _tile(F, FFN_TF, LANES)
    nf = F // tf

    def kern(x_ref, g_ref, wg_ref, wu_ref, wd_ref, o_ref, xn_sc):
        f = pl.program_id(1)

        @pl.when(f == 0)
        def _():
            x = x_ref[...]
            xn_sc[...] = _rms(x, g_ref[...]).astype(BF16)
            o_ref[...] = x

        xn = xn_sc[...]
        act = (_silu(_dot(xn, wg_ref[...])) * _dot(xn, wu_ref[...])).astype(BF16)
        o_ref[...] += _dot(act, wd_ref[...])

    return pl.pallas_call(
        kern, grid=(T // tm, nf),
        in_specs=[pl.BlockSpec((tm, D), lambda i, f: (i, 0)),
                  pl.BlockSpec(gain.shape, lambda i, f: (0, 0)),
                  pl.BlockSpec((D, tf), lambda i, f: (0, f)),
                  pl.BlockSpec((D, tf), lambda i, f: (0, nf + f)),
                  pl.BlockSpec((tf, D), lambda i, f: (f, 0))],
        out_specs=pl.BlockSpec((tm, D), lambda i, f: (i, 0)),
        out_shape=jax.ShapeDtypeStruct((T, D), F32),
        scratch_shapes=[pltpu.VMEM((tm, D), BF16)],
        compiler_params=_cparams(48), name="ffn")(h, gain, w_gu, w_gu, w_down)


def _hgrn_in_proj(h, gain, w_in, lb):
    T, D = h.shape
    HK = w_in.shape[1] // 4
    tm = _tile(T, LINEAR_TM)

    def call(col, epilogue, dtypes, extras=()):
        out_shapes = tuple(jax.ShapeDtypeStruct((T, HK), dt) for dt in dtypes)
        out_specs = tuple(pl.BlockSpec((tm, HK), lambda i: (i, 0)) for _ in dtypes)
        return _fused_linear([h], [w_in], tm=tm, gain=gain, epilogue=epilogue,
                             wspecs=[pl.BlockSpec((D, HK), lambda i: (0, col))],
                             out_shapes=out_shapes, out_specs=out_specs, const_extras=extras,
                             name=f"hgrn_in_proj_{col}")

    def ep_silu(acc, x0, r_refs, c_refs, o_refs):
        o_refs[0][...] = _silu(acc).astype(BF16)

    def ep_ident(acc, x0, r_refs, c_refs, o_refs):
        o_refs[0][...] = acc.astype(BF16)

    def ep_gate(acc, x0, r_refs, c_refs, o_refs):
        k = (1.0 - c_refs[0][...]) / (1.0 + jnp.exp(acc))
        o_refs[0][...] = k.astype(BF16)
        o_refs[1][...] = jnp.log1p(-k)

    (qs,) = call(0, ep_silu, (BF16,))
    kk, logf = call(1, ep_gate, (BF16, F32), extras=(lb,))
    (vv,) = call(2, ep_ident, (BF16,))
    (gs,) = call(3, ep_silu, (BF16,))
    return qs, kk, logf, vv, gs


def _gla_band(qf, kf, vf, fg, n_diag, lane_sum):
    C = qf.shape[0]
    r8 = lax.broadcasted_iota(jnp.int32, qf.shape, 0) % SUBLANES
    gprod = None
    parts = []
    for d in range(n_diag):
        if d == 0:
            p = qf * kf
        else:
            fr = fg if d == 1 else pltpu.roll(fg, d - 1, 0)
            gprod = fr if gprod is None else gprod * fr
            p = jnp.where(r8 >= d, qf * gprod * pltpu.roll(kf, d, 0), 0.0)
        parts.append(p)
    sums = lane_sum(parts)
    out = None
    for d in range(n_diag):
        vr = vf if d == 0 else pltpu.roll(vf, d, 0)
        term = sums[d] * vr
        out = term if out is None else out + term
    return out


def _gla_prompt(qs, kk, logf, vv, gs, hg_norm, B, L, H):
    T, HK = qs.shape
    K = HK // H
    ct = _tile(L, GLA_CT, GLA_CHUNK)
    C = min(GLA_CHUNK, ct)
    nct = L // ct
    ncc = ct // C

    def kern(q_ref, k_ref, lf_ref, v_ref, g_ref, hn_ref, o_ref, s_ref, st_sc):
        ci = pl.program_id(2)

        @pl.when(ci == 0)
        def _():
            st_sc[...] = jnp.zeros((K, K), F32)

        rows = lax.broadcasted_iota(jnp.int32, (C, C), 0)
        cols = lax.broadcasted_iota(jnp.int32, (C, C), 1)
        tril = (rows >= cols).astype(F32)
        ones_bf = jnp.ones((K, K), BF16)

        def lane_sum(parts):
            r = _dot(jnp.concatenate([p.astype(BF16) for p in parts], axis=0), ones_bf)
            return [r[d * C:(d + 1) * C] for d in range(len(parts))]

        st = st_sc[...]
        for c in range(ncc):
            sl = slice(c * C, (c + 1) * C)
            qf, kf, vf = q_ref[sl, :].astype(F32), k_ref[sl, :].astype(F32), v_ref[sl, :].astype(F32)
            lf = lf_ref[sl, :]
            fg = jnp.exp(lf)
            b = jnp.dot(tril, lf, precision=lax.Precision.HIGHEST, preferred_element_type=F32)
            blast = b[C - 1:C, :]
            qb = qf * jnp.exp(b)
            kb = kf * jnp.exp(blast - b)
            o = lax.dot_general(qb.astype(BF16), st.astype(BF16), NT_DIMS, preferred_element_type=F32)
            amat = None
            blk = C // 2
            while blk >= SUBLANES:
                q_parts, k_parts = [], []
                for m in range(C // blk):
                    r = slice(m * blk, (m + 1) * blk)
                    if m % 2 == 1:
                        ref = b[m * blk - 1:m * blk, :]
                        q_parts.append(qf[r] * jnp.exp(b[r] - ref))
                        k_parts.append(jnp.zeros((blk, K), F32))
                    else:
                        ref = b[(m + 1) * blk - 1:(m + 1) * blk, :]
                        q_parts.append(jnp.zeros((blk, K), F32))
                        k_parts.append(kf[r] * jnp.exp(ref - b[r]))
                ql = jnp.concatenate(q_parts, axis=0).astype(BF16)
                kl = jnp.concatenate(k_parts, axis=0).astype(BF16)
                al = lax.dot_general(ql, kl, NT_DIMS, preferred_element_type=F32)
                al = jnp.where((rows // (2 * blk)) == (cols // (2 * blk)), al, 0.0)
                amat = al if amat is None else amat + al
                blk //= 2
            if amat is not None:
                o = o + _dot(amat.astype(BF16), vf.astype(BF16))
            o = o + _gla_band(qf, kf, vf, fg, min(SUBLANES, C), lane_sum)
            st = st * jnp.exp(blast) + lax.dot_general(
                vf.astype(BF16), kb.astype(BF16), TN_DIMS, preferred_element_type=F32)
            on = _rms(o, hn_ref[...]) * g_ref[sl, :].astype(F32)
            o_ref[sl, :] = on.astype(BF16)
        st_sc[...] = st

        @pl.when(ci == nct - 1)
        def _():
            s_ref[0, 0] = st.T

    tok = lambda b, h, ci: (b * nct + ci, h)
    return pl.pallas_call(
        kern, grid=(B, H, nct),
        in_specs=[pl.BlockSpec((ct, K), tok)] * 5 + [pl.BlockSpec(hg_norm.shape, lambda b, h, ci: (0, 0))],
        out_specs=(pl.BlockSpec((ct, K), tok),
                   pl.BlockSpec((1, 1, K, K), lambda b, h, ci: (b, h, 0, 0))),
        out_shape=(jax.ShapeDtypeStruct((T, HK), BF16), jax.ShapeDtypeStruct((B, H, K, K), F32)),
        scratch_shapes=[pltpu.VMEM((K, K), F32)],
        compiler_params=_cparams(32), name="gla_prompt")(qs, kk, logf, vv, gs, hg_norm)


def _gla_sample(qs, kk, logf, vv, gs, hg_norm, s0, DB, Ld, H):
    HK = qs.shape[2]
    K = HK // H
    C = SUBLANES
    assert Ld <= C

    def kern(q_ref, k_ref, lf_ref, v_ref, g_ref, hn_ref, s0_ref, o_ref, s_ref, pad_sc):
        rows = lax.broadcasted_iota(jnp.int32, (C, C), 0)
        cols = lax.broadcasted_iota(jnp.int32, (C, C), 1)
        tril = (rows >= cols).astype(F32)

        def padded(ref, slot):
            pad_sc[slot] = jnp.zeros((C, HK), F32)
            pad_sc[slot, 0:Ld, :] = ref[0].astype(F32)
            return pad_sc[slot]

        qf, kf, lf, vf = padded(q_ref, 0), padded(k_ref, 1), padded(lf_ref, 2), padded(v_ref, 3)
        fg = jnp.exp(lf)
        b = jnp.dot(tril, lf, precision=lax.Precision.HIGHEST, preferred_element_type=F32)
        blast = b[C - 1:C, :]
        qb = (qf * jnp.exp(b)).astype(BF16)
        kb = (kf * jnp.exp(blast - b)).astype(BF16)
        dec = jnp.exp(blast)
        vb = vf.astype(BF16)

        def lane_sum(parts):
            return parts

        for h in range(H):
            hl = slice(h * K, (h + 1) * K)

            def head_sum(parts):
                return [jnp.sum(p, axis=-1, keepdims=True) for p in parts]

            st = s0_ref[0, h].T
            o = lax.dot_general(qb[:, hl], st.astype(BF16), NT_DIMS, preferred_element_type=F32)
            o = o + _gla_band(qf[:, hl], kf[:, hl], vf[:, hl], fg[:, hl], Ld, head_sum)
            st = st * dec[:, hl] + lax.dot_general(vb[:, hl], kb[:, hl], TN_DIMS,
                                                   preferred_element_type=F32)
            s_ref[0, h] = st.T
            on = _rms(o[0:Ld], hn_ref[...]) * g_ref[0, :, hl].astype(F32)
            o_ref[0, :, hl] = on.astype(BF16)

    tok = pl.BlockSpec((1, Ld, HK), lambda b: (b, 0, 0))
    st_spec = pl.BlockSpec((1, H, K, K), lambda b: (b, 0, 0, 0))
    return pl.pallas_call(
        kern, grid=(DB,),
        in_specs=[tok] * 5 + [pl.BlockSpec(hg_norm.shape, lambda b: (0, 0)), st_spec],
        out_specs=(tok, st_spec),
        out_shape=(jax.ShapeDtypeStruct((DB, Ld, HK), BF16), jax.ShapeDtypeStruct(s0.shape, F32)),
        scratch_shapes=[pltpu.VMEM((4, C, HK), F32)],
        compiler_params=_cparams(32), name="gla_sample")(qs, kk, logf, vv, gs, hg_norm, s0)


def _router(h, gain, w_router, b_router):
    T, D = h.shape
    E = w_router.shape[1]
    tm = _tile(T, LINEAR_TM)

    def kern(x_ref, g_ref, w_ref, b_ref, o_ref):
        xn = _rms(x_ref[...], g_ref[...])
        logits = jnp.dot(xn, w_ref[...], precision=lax.Precision.HIGHEST,
                         preferred_element_type=F32) + b_ref[...]
        idx = lax.broadcasted_iota(jnp.int32, logits.shape, 1)
        m1 = jnp.max(logits, axis=-1, keepdims=True)
        i1 = jnp.min(jnp.where(logits == m1, idx, E), axis=-1, keepdims=True)
        rest = jnp.where(idx == i1, -jnp.inf, logits)
        m2 = jnp.max(rest, axis=-1, keepdims=True)
        i2 = jnp.min(jnp.where(rest == m2, idx, E), axis=-1, keepdims=True)
        e2 = jnp.exp(m2 - m1)
        g1 = 1.0 / (1.0 + e2)
        o_ref[...] = jnp.where(idx == i1, g1, 0.0) + jnp.where(idx == i2, e2 * g1, 0.0)

    return pl.pallas_call(
        kern, grid=(T // tm,),
        in_specs=[pl.BlockSpec((tm, D), lambda i: (i, 0)),
                  pl.BlockSpec(gain.shape, lambda i: (0, 0)),
                  pl.BlockSpec(w_router.shape, lambda i: (0, 0)),
                  pl.BlockSpec(b_router.shape, lambda i: (0, 0))],
        out_specs=pl.BlockSpec((tm, E), lambda i: (i, 0)),
        out_shape=jax.ShapeDtypeStruct((T, E), F32),
        compiler_params=_cparams(32), name="router")(h, gain, w_router, b_router)


def _moe_dense(h, gain, comb, w_gu, w_down, final_gain):
    T, D = h.shape
    E, F = w_down.shape[0], w_down.shape[1]
    tm = _tile(T, MOE_TM)
    tf = _tile(F, MOE_TF, LANES)
    nf = F // tf

    def kern(x_ref, g_ref, c_ref, wg_ref, wu_ref, wd_ref, fg_ref, o_ref, xn_sc):
        e, f = pl.program_id(1), pl.program_id(2)

        @pl.when((e == 0) & (f == 0))
        def _():
            x = x_ref[...]
            xn_sc[...] = _rms(x, g_ref[...]).astype(BF16)
            o_ref[...] = x

        xn = xn_sc[...]
        comb_v = c_ref[...]
        lane = lax.broadcasted_iota(jnp.int32, comb_v.shape, 1)
        ce = jnp.sum(jnp.where(lane == e, comb_v, 0.0), axis=-1, keepdims=True)
        act = (_silu(_dot(xn, wg_ref[0])) * _dot(xn, wu_ref[0]) * ce).astype(BF16)
        o_ref[...] += _dot(act, wd_ref[0])

        @pl.when((e == E - 1) & (f == nf - 1))
        def _():
            o_ref[...] = _rms(o_ref[...], fg_ref[...])

    return pl.pallas_call(
        kern, grid=(T // tm, E, nf),
        in_specs=[pl.BlockSpec((tm, D), lambda i, e, f: (i, 0)),
                  pl.BlockSpec(gain.shape, lambda i, e, f: (0, 0)),
                  pl.BlockSpec((tm, E), lambda i, e, f: (i, 0)),
                  pl.BlockSpec((1, D, tf), lambda i, e, f: (e, 0, f)),
                  pl.BlockSpec((1, D, tf), lambda i, e, f: (e, 0, nf + f)),
                  pl.BlockSpec((1, tf, D), lambda i, e, f: (e, f, 0)),
                  pl.BlockSpec(final_gain.shape, lambda i, e, f: (0, 0))],
        out_specs=pl.BlockSpec((tm, D), lambda i, e, f: (i, 0)),
        out_shape=jax.ShapeDtypeStruct((T, D), F32),
        scratch_shapes=[pltpu.VMEM((tm, D), BF16)],
        compiler_params=_cparams(48), name="moe_dense")(h, gain, comb, w_gu, w_gu, w_down, final_gain)


def _rope_rows(pos, rope_dim):
    inv = ROPE_THETA ** (-jnp.arange(0, rope_dim, 2, dtype=F32) / rope_dim)
    ang = pos.astype(F32)[:, None] * inv[None, :]
    z = jnp.zeros((pos.shape[0], LANES - rope_dim), F32)
    cos, sin = jnp.cos(ang), jnp.sin(ang)
    return jnp.concatenate([cos, cos, z], axis=-1), jnp.concatenate([sin, sin, z], axis=-1)


def _rot_cols(w):
    half = w.shape[-1] // 2
    return jnp.concatenate([-w[..., half:], w[..., :half]], axis=-1)


def _pad_lanes(w):
    pad = LANES - w.shape[-1]
    return jnp.pad(w, [(0, 0)] * (w.ndim - 1) + [(0, pad)])


def _prep_even(a, w_in_e, w_q_b, w_kv_b, pool_w, w_out_e, w_ffn_gu, w_ffn_down, dims, H):
    DP, QL, KL, ROPE = dims
    w_in = w_in_e[a]
    k_raw = w_in[:, DP + QL + KL:]
    w_ext = jnp.concatenate([w_in[:, :DP + QL + KL], _pad_lanes(k_raw), _pad_lanes(_rot_cols(k_raw))],
                            axis=-1).astype(BF16)
    wq = w_q_b[a].reshape(QL, H, NOPE_DIM + ROPE)
    wq_rope = wq[..., NOPE_DIM:]
    wq_ext = jnp.concatenate([wq[..., :NOPE_DIM].reshape(QL, H * NOPE_DIM),
                              _pad_lanes(wq_rope).reshape(QL, H * LANES),
                              _pad_lanes(_rot_cols(wq_rope)).reshape(QL, H * LANES)], axis=-1).astype(BF16)
    w_kv = w_kv_b[a].reshape(KL, H, NOPE_DIM + V_DIM)
    wuk_t = jnp.transpose(w_kv[..., :NOPE_DIM], (1, 2, 0)).astype(BF16)
    wuv = jnp.transpose(w_kv[..., NOPE_DIM:], (1, 0, 2)).astype(BF16)
    w_out = w_out_e[a].astype(BF16)
    return dict(w_ext=w_ext, wq_ext=wq_ext, wuk_t=wuk_t, wuv=wuv, pool_w=pool_w[a].astype(BF16),
                w_out_pool=w_out[:DP], w_out_attn=w_out[DP:],
                w_gu=w_ffn_gu[a].astype(BF16), w_down=w_ffn_down[a].astype(BF16))


def kernel(x_prompt, x_sample, cache_ckv, cache_krope, page_table, state_pool, state_hgrn,
           norm_mix_e, w_in_e, q_norm, w_q_b, kv_norm, w_kv_b, pool_w, pool_scale, w_out_e,
           norm_ffn_e, w_ffn_gu, w_ffn_down,
           norm_mix_o, w_in_o, hg_lower_bound, hg_norm, w_out_o, norm_ffn_o, w_router, b_router,
           w_exp_gu, w_exp_down, final_norm):
    B, L, D = x_prompt.shape
    DB, Ld, _ = x_sample.shape
    n_pages = page_table.shape[1]
    PG = cache_ckv.shape[2]
    past_len = n_pages * PG
    KL, ROPE = cache_ckv.shape[3], cache_krope.shape[3]
    DP, PS = state_pool.shape[3], state_pool.shape[2]
    QL = q_norm.shape[1]
    H = w_q_b.shape[2] // (NOPE_DIM + ROPE)
    HG = state_hgrn.shape[2]
    depth = hg_lower_bound.shape[0]
    dims = (DP, QL, KL, ROPE)
    scale = float((NOPE_DIM + ROPE) ** -0.5 * math.log2(math.e))
    cache_krope_t = jnp.swapaxes(cache_krope, 2, 3)
    QW = KL + LANES
    KN = 16
    assert Ld <= KN and ROPE <= LANES

    row = lambda v: v.reshape(1, -1).astype(F32)
    lb_p = jax.nn.softmax(hg_lower_bound.astype(F32), axis=0)
    lower_bounds = jnp.cumsum(lb_p, axis=0) - lb_p[0]

    hp = x_prompt.reshape(B * L, D)
    hs = x_sample.reshape(DB * Ld, D)
    cos_p, sin_p = _rope_rows(jnp.tile(jnp.arange(L), B), ROPE)
    cos_s, sin_s = _rope_rows(jnp.tile(past_len + jnp.arange(Ld), DB), ROPE)

    outs_p = dict(ckv=[], krope=[], pool=[], hgrn=[])
    outs_s = dict(ckv=[], krope=[], pool=[], hgrn=[])
    for l in range(depth):
        a = l // 2
        if l % 2 == 0:
            w = _prep_even(a, w_in_e, w_q_b, w_kv_b, pool_w, w_out_e, w_ffn_gu, w_ffn_down, dims, H)
            g_mix, g_q, g_kv = row(norm_mix_e[a]), row(q_norm[a]), row(kv_norm[a])
            g_ffn, p_scale = row(norm_ffn_e[a]), row(pool_scale[a])

            u, qn, ckv, kpe, kcat = _even_in_proj(hp, g_mix, w["w_ext"], g_q, g_kv, cos_p, sin_p, dims)
            pool_out, pool_new = _pool_prompt(u, jnp.zeros((B, PS, DP), F32), w["pool_w"], p_scale, B, L)
            qcat = _q_proj(qn, w["wq_ext"], w["wuk_t"], cos_p, sin_p, H, KL, scale, head_major=True)
            attn = _flash_prompt(qcat, kcat, w["wuv"], B, L, H, KL)
            hp = _linear_residual([pool_out, attn], [w["w_out_pool"], w["w_out_attn"]], hp, "out_proj_e")
            hp = _ffn(hp, g_ffn, w["w_gu"], w["w_down"])
            outs_p["ckv"].append(ckv.reshape(B, L, KL))
            outs_p["krope"].append(kpe.reshape(B, L, ROPE))
            outs_p["pool"].append(pool_new)

            u, qn, ckv, kpe, kcat = _even_in_proj(hs, g_mix, w["w_ext"], g_q, g_kv, cos_s, sin_s, dims)
            u3 = u.reshape(DB, Ld, DP)
            ext_tm = jnp.transpose(jnp.concatenate([state_pool[a], u3], axis=1), (1, 0, 2))
            pool_tm = _pool_sample(ext_tm, w["pool_w"], p_scale, Ld, past_len)
            pool_out = jnp.transpose(pool_tm, (1, 0, 2)).reshape(DB * Ld, DP)
            qrows = _q_proj(qn, w["wq_ext"], w["wuk_t"], cos_s, sin_s, H, KL, scale, head_major=False)
            q_s = qrows.reshape(DB, Ld * H, QW)
            knew = jnp.pad(kcat.reshape(DB, Ld, QW), ((0, 0), (0, KN - Ld), (0, 0)))
            ctx = _decode_attention(page_table, q_s, knew, cache_ckv, cache_krope_t, a, H, Ld, KL, ROPE)
            attn = _ctx_to_attn_out(ctx.reshape(DB * Ld, H * KL), w["wuv"], H, KL)
            hs = _linear_residual([pool_out, attn], [w["w_out_pool"], w["w_out_attn"]], hs, "out_proj_e")
            hs = _ffn(hs, g_ffn, w["w_gu"], w["w_down"])
            outs_s["ckv"].append(ckv.reshape(DB, Ld, KL))
            outs_s["krope"].append(kpe.reshape(DB, Ld, ROPE))
            outs_s["pool"].append(jnp.concatenate([state_pool[a], u3], axis=1)[:, -PS:])
        else:
            w_in = w_in_o[a].astype(BF16)
            w_out = w_out_o[a].astype(BF16)
            w_gu = w_exp_gu[a].astype(BF16)
            w_dn = w_exp_down[a].astype(BF16)
            g_mix, g_ffn, g_hn = row(norm_mix_o[a]), row(norm_ffn_o[a]), row(hg_norm[a])
            lb = row(lower_bounds[l])
            w_r, b_r = w_router[a].astype(F32), row(b_router[a])
            g_fin = row(final_norm) if l == depth - 1 else None
            assert g_fin is not None

            qs, kk, logf, vv, gs = _hgrn_in_proj(hp, g_mix, w_in, lb)
            on, s_new = _gla_prompt(qs, kk, logf, vv, gs, g_hn, B, L, HG)
            hp = _linear_residual([on], [w_out], hp, "out_proj_o")
            hp = _moe_dense(hp, g_ffn, _router(hp, g_ffn, w_r, b_r), w_gu, w_dn, g_fin)
            outs_p["hgrn"].append(s_new)

            qs, kk, logf, vv, gs = _hgrn_in_proj(hs, g_mix, w_in, lb)
            r3 = lambda t: t.reshape(DB, Ld, -1)
            on, s_new = _gla_sample(r3(qs), r3(kk), r3(logf), r3(vv), r3(gs), g_hn, state_hgrn[a], DB, Ld, HG)
            hs = _linear_residual([on.reshape(DB * Ld, -1)], [w_out], hs, "out_proj_o")
            hs = _moe_dense(hs, g_ffn, _router(hs, g_ffn, w_r, b_r), w_gu, w_dn, g_fin)
            outs_s["hgrn"].append(s_new)

    return (hp.reshape(B, L, D), hs.reshape(DB, Ld, D),
            jnp.stack(outs_p["ckv"]), jnp.stack(outs_p["krope"]), jnp.stack(outs_p["pool"]),
            jnp.stack(outs_p["hgrn"]),
            jnp.stack(outs_s["ckv"]), jnp.stack(outs_s["krope"]), jnp.stack(outs_s["pool"]),
            jnp.stack(outs_s["hgrn"]))
```

```python
import functools
import math

import jax
import jax.numpy as jnp
from jax import lax
from jax.experimental import pallas as pl
from jax.experimental.pallas import tpu as pltpu

F32 = jnp.float32
BF16 = jnp.bfloat16

EPS = 1e-6
POOL_WINDOWS = (2, 4, 8, 16)
NOPE_DIM = 128
V_DIM = 128
ROPE_THETA = 10000.0
TOP_K = 2
LANES = 128
SUBLANES = 8
NEG_BIG = -1e30

LINEAR_TM = 512
FFN_TM = 512
FFN_TF = 512
MOE_TM = 512
MOE_MIN_TILE = 16
MOE_TF = 256
MOE_COMBINE_TM = 256
DMA_LOOP_UNROLL = 8
FLASH_TQ = 128
FLASH_TK = 512
POOL_TL = 512
GLA_CT = 256
GLA_CHUNK = 256
GLA_HEADS_PER_STEP = 2
DECODE_PAGES_PER_STEP = 16

NT_DIMS = (((1,), (1,)), ((), ()))
TN_DIMS = (((0,), (0,)), ((), ()))


def _cparams(vmem_mb):
    return pltpu.CompilerParams(vmem_limit_bytes=vmem_mb * 2 ** 20)


def _tile(n, pref, mult=SUBLANES):
    if n <= pref:
        return n
    for t in range(pref, 0, -1):
        if n % t == 0 and t % mult == 0:
            return t
    return n


def _rms(x, g):
    return x * lax.rsqrt(jnp.mean(x * x, axis=-1, keepdims=True) + EPS) * g


def _silu(x):
    return x / (1.0 + jnp.exp(-x))


def _dot(a, b):
    return jnp.dot(a, b, preferred_element_type=F32)


def _fused_linear(xs, ws, *, tm, epilogue, out_shapes, out_specs, gain=None, wspecs=None,
                  row_extras=(), const_extras=(), vmem_mb=48, name="linear"):
    T = xs[0].shape[0]
    grid = (T // tm,)
    n_x, n_w, n_r, n_c = len(xs), len(ws), len(row_extras), len(const_extras)

    def kern(*refs):
        pos = 0
        x_refs = refs[pos:pos + n_x]; pos += n_x
        gain_ref = None
        if gain is not None:
            gain_ref = refs[pos]; pos += 1
        w_refs = refs[pos:pos + n_w]; pos += n_w
        r_refs = refs[pos:pos + n_r]; pos += n_r
        c_refs = refs[pos:pos + n_c]; pos += n_c
        o_refs = refs[pos:]
        acc = None
        x0 = None
        for xr, wr in zip(x_refs, w_refs):
            xv = xr[...]
            if gain_ref is not None:
                x0 = xv
                xv = _rms(xv, gain_ref[...]).astype(BF16)
            d = _dot(xv, wr[...])
            acc = d if acc is None else acc + d
        epilogue(acc, x0, r_refs, c_refs, o_refs)

    in_specs = [pl.BlockSpec((tm, x.shape[1]), lambda i: (i, 0)) for x in xs]
    args = list(xs)
    if gain is not None:
        in_specs.append(pl.BlockSpec(gain.shape, lambda i: (0, 0)))
        args.append(gain)
    if wspecs is None:
        wspecs = [pl.BlockSpec(w.shape, lambda i: (0,) * w.ndim) for w in ws]
    in_specs += list(wspecs)
    args += list(ws)
    for r in row_extras:
        in_specs.append(pl.BlockSpec((tm, r.shape[1]), lambda i: (i, 0)))
        args.append(r)
    for c in const_extras:
        in_specs.append(pl.BlockSpec(c.shape, lambda i, nd=c.ndim: (0,) * nd))
        args.append(c)
    return pl.pallas_call(
        kern, grid=grid, in_specs=in_specs, out_specs=out_specs, out_shape=out_shapes,
        compiler_params=_cparams(vmem_mb), name=name)(*args)


def _even_in_proj(h, gain, w_ext, qn_g, kvn_g, cos_rows, sin_rows, dims):
    DP, QL, KL, ROPE = dims
    T, D = h.shape
    tm = _tile(T, LINEAR_TM)
    o_kr = DP + QL + KL

    def epilogue(acc, x0, r_refs, c_refs, o_refs):
        u_ref, qn_ref, ckv_ref, kpe_ref, kcat_ref = o_refs
        cos, sin = r_refs[0][...], r_refs[1][...]
        u_ref[...] = acc[:, :DP]
        qn_ref[...] = _rms(acc[:, DP:DP + QL], c_refs[0][...]).astype(BF16)
        ckv = _rms(acc[:, DP + QL:o_kr], c_refs[1][...])
        ckv_ref[...] = ckv
        kpe = acc[:, o_kr:o_kr + LANES] * cos + acc[:, o_kr + LANES:o_kr + 2 * LANES] * sin
        kpe_ref[...] = kpe[:, :ROPE]
        kcat_ref[:, :KL] = ckv.astype(BF16)
        kcat_ref[:, KL:] = kpe.astype(BF16)

    out_shapes = (jax.ShapeDtypeStruct((T, DP), F32), jax.ShapeDtypeStruct((T, QL), BF16),
                  jax.ShapeDtypeStruct((T, KL), F32), jax.ShapeDtypeStruct((T, ROPE), F32),
                  jax.ShapeDtypeStruct((T, KL + LANES), BF16))
    out_specs = tuple(pl.BlockSpec((tm, s.shape[1]), lambda i: (i, 0)) for s in out_shapes)
    return _fused_linear([h], [w_ext], tm=tm, gain=gain, epilogue=epilogue,
                         out_shapes=out_shapes, out_specs=out_specs,
                         row_extras=(cos_rows, sin_rows), const_extras=(qn_g, kvn_g),
                         name="even_in_proj")


def _q_proj(qn, wq_ext, wuk_t, cos_rows, sin_rows, H, KL, scale, head_major):
    T = qn.shape[0]
    tm = _tile(T, LINEAR_TM)
    QW = KL + LANES
    r0, r1 = H * NOPE_DIM, 2 * H * NOPE_DIM

    def epilogue(acc, x0, r_refs, c_refs, o_refs):
        (o_ref,) = o_refs
        cos, sin = r_refs[0][...], r_refs[1][...]
        wuk_ref = c_refs[0]
        for h in range(H):
            qn_h = acc[:, h * NOPE_DIM:(h + 1) * NOPE_DIM].astype(BF16)
            q_abs = (_dot(qn_h, wuk_ref[h]) * scale).astype(BF16)
            q_pe = ((acc[:, r0 + h * LANES:r0 + (h + 1) * LANES] * cos
                     + acc[:, r1 + h * LANES:r1 + (h + 1) * LANES] * sin) * scale).astype(BF16)
            if head_major:
                o_ref[h, :, :KL] = q_abs
                o_ref[h, :, KL:] = q_pe
            else:
                o_ref[:, h * QW:h * QW + KL] = q_abs
                o_ref[:, h * QW + KL:(h + 1) * QW] = q_pe

    if head_major:
        out_shape = jax.ShapeDtypeStruct((H, T, QW), BF16)
        out_spec = pl.BlockSpec((H, tm, QW), lambda i: (0, i, 0))
    else:
        out_shape = jax.ShapeDtypeStruct((T, H * QW), BF16)
        out_spec = pl.BlockSpec((tm, H * QW), lambda i: (i, 0))
    (out,) = _fused_linear([qn], [wq_ext], tm=tm, epilogue=epilogue, out_shapes=(out_shape,),
                           out_specs=(out_spec,), row_extras=(cos_rows, sin_rows),
                           const_extras=(wuk_t,), name="q_proj")
    return out


def _pool_group_out(ext_ref, base, tl, pos0, g, w, PG, pw_ref, scale_ref):
    lo, hi = g * PG, (g + 1) * PG
    x = ext_ref[base:base + tl, lo:hi]
    win = x
    for j in range(1, w):
        win = win + ext_ref[base - j:base - j + tl, lo:hi]
    if pos0 is None:
        d = win * (1.0 / w) - x
    else:
        pos = pos0 + lax.broadcasted_iota(jnp.int32, (tl, 1), 0)
        cnt = jnp.minimum(pos + 1, w).astype(F32)
        d = win / cnt - x
    return _dot(d.astype(BF16), pw_ref[g]) * scale_ref[:, lo:hi]


def _pool_prompt(u, past, pool_w, pool_scale, B, L):
    DP = u.shape[1]
    PS = past.shape[1]
    HALO = 16
    G = len(POOL_WINDOWS)
    PG = DP // G
    tl = _tile(L, POOL_TL, HALO)
    nl = L // tl

    def kern(u_ref, halo_ref, past_ref, pw_ref, sc_ref, o_ref, new_ref, ext_ref):
        i = pl.program_id(1)

        @pl.when(i == 0)
        def _():
            ext_ref[0:1, :] = jnp.zeros((1, DP), F32)
            ext_ref[HALO - PS:HALO, :] = past_ref[0]

        @pl.when(i > 0)
        def _():
            ext_ref[0:HALO, :] = halo_ref[...]

        ext_ref[HALO:HALO + tl, :] = u_ref[...]
        for g, w in enumerate(POOL_WINDOWS):
            o_ref[:, g * PG:(g + 1) * PG] = _pool_group_out(
                ext_ref, HALO, tl, i * tl, g, w, PG, pw_ref, sc_ref).astype(BF16)

        @pl.when(i == nl - 1)
        def _():
            new_ref[0] = ext_ref[HALO + tl - PS:HALO + tl, :]

    r = tl // HALO
    return pl.pallas_call(
        kern, grid=(B, nl),
        in_specs=[
            pl.BlockSpec((tl, DP), lambda b, i: (b * nl + i, 0)),
            pl.BlockSpec((HALO, DP), lambda b, i: (jnp.maximum((b * nl + i) * r - 1, 0), 0)),
            pl.BlockSpec((1, PS, DP), lambda b, i: (b, 0, 0)),
            pl.BlockSpec(pool_w.shape, lambda b, i: (0, 0, 0)),
            pl.BlockSpec(pool_scale.shape, lambda b, i: (0, 0)),
        ],
        out_specs=(pl.BlockSpec((tl, DP), lambda b, i: (b * nl + i, 0)),
                   pl.BlockSpec((1, PS, DP), lambda b, i: (b, 0, 0))),
        out_shape=(jax.ShapeDtypeStruct((B * L, DP), BF16), jax.ShapeDtypeStruct((B, PS, DP), F32)),
        scratch_shapes=[pltpu.VMEM((HALO + tl, DP), F32)],
        compiler_params=_cparams(40), name="pool_prompt")(u, u, past, pool_w, pool_scale)


def _pool_sample(ext_tm, pool_w, pool_scale, Ld, start):
    R, DB, DP = ext_tm.shape
    PS = R - Ld
    G = len(POOL_WINDOWS)
    PG = DP // G

    def kern(e_ref, pw_ref, sc_ref, o_ref):
        for t in range(Ld):
            for g, w in enumerate(POOL_WINDOWS):
                lo, hi = g * PG, (g + 1) * PG
                x = e_ref[PS + t, :, lo:hi]
                win = x
                for j in range(1, w):
                    win = win + e_ref[PS + t - j, :, lo:hi]
                cnt = float(min(start + t + 1, w))
                d = win / cnt - x
                o_ref[t, :, lo:hi] = (_dot(d.astype(BF16), pw_ref[g]) * sc_ref[:, lo:hi]).astype(BF16)

    return pl.pallas_call(
        kern, grid=(1,),
        in_specs=[pl.BlockSpec(ext_tm.shape, lambda i: (0, 0, 0)),
                  pl.BlockSpec(pool_w.shape, lambda i: (0, 0, 0)),
                  pl.BlockSpec(pool_scale.shape, lambda i: (0, 0))],
        out_specs=pl.BlockSpec((Ld, DB, DP), lambda i: (0, 0, 0)),
        out_shape=jax.ShapeDtypeStruct((Ld, DB, DP), BF16),
        compiler_params=_cparams(48), name="pool_sample")(ext_tm, pool_w, pool_scale)


def _flash_prompt(qcat, kcat, wuv, B, L, H, KL):
    QW = qcat.shape[2]
    tq = _tile(L, FLASH_TQ, 16)
    tk = _tile(L, FLASH_TK, 16)
    assert tk % tq == 0
    nq, nk = L // tq, L // tk
    R = H * tq

    def last_needed(qi):
        return (qi * tq + tq - 1) // tk

    def kern(q_ref, k_ref, wuv_ref, o_ref, m_sc, l_sc, acc_sc):
        qi, ki = pl.program_id(1), pl.program_id(2)

        @pl.when(ki == 0)
        def _():
            m_sc[...] = jnp.full((R, 1), NEG_BIG, F32)
            l_sc[...] = jnp.zeros((R, 1), F32)
            acc_sc[...] = jnp.zeros((R, KL), F32)

        def update(masked):
            q = q_ref[...].reshape(R, QW)
            k = k_ref[...]
            s = lax.dot_general(q, k, NT_DIMS, preferred_element_type=F32)
            if masked:
                qpos = qi * tq + (lax.broadcasted_iota(jnp.int32, (R, tk), 0) % tq)
                kpos = ki * tk + lax.broadcasted_iota(jnp.int32, (R, tk), 1)
                s = jnp.where(kpos <= qpos, s, NEG_BIG)
            m_prev = m_sc[...]
            m_new = jnp.maximum(m_prev, jnp.max(s, axis=-1, keepdims=True))
            alpha = jnp.exp2(m_prev - m_new)
            p = jnp.exp2(s - m_new)
            l_sc[...] = alpha * l_sc[...] + jnp.sum(p, axis=-1, keepdims=True)
            acc_sc[...] = alpha * acc_sc[...] + _dot(p.astype(BF16), k[:, :KL])
            m_sc[...] = m_new

        pl.when(ki < last_needed(qi))(functools.partial(update, False))
        pl.when(ki == last_needed(qi))(functools.partial(update, True))

        @pl.when(ki == last_needed(qi))
        def _():
            ctx = (acc_sc[...] / l_sc[...]).astype(BF16)
            for h in range(H):
                o_ref[:, h * V_DIM:(h + 1) * V_DIM] = _dot(
                    ctx[h * tq:(h + 1) * tq], wuv_ref[h]).astype(BF16)

    return pl.pallas_call(
        kern, grid=(B, nq, nk),
        in_specs=[
            pl.BlockSpec((H, tq, QW), lambda b, qi, ki: (0, b * nq + qi, 0)),
            pl.BlockSpec((tk, QW), lambda b, qi, ki: (b * nk + jnp.minimum(ki, last_needed(qi)), 0)),
            pl.BlockSpec(wuv.shape, lambda b, qi, ki: (0, 0, 0)),
        ],
        out_specs=pl.BlockSpec((tq, H * V_DIM), lambda b, qi, ki: (b * nq + qi, 0)),
        out_shape=jax.ShapeDtypeStruct((B * L, H * V_DIM), BF16),
        scratch_shapes=[pltpu.VMEM((R, 1), F32), pltpu.VMEM((R, 1), F32), pltpu.VMEM((R, KL), F32)],
        compiler_params=_cparams(56), name="flash_prompt")(qcat, kcat, wuv)


def _decode_attention(page_table, q_s, knew, cache_ckv, cache_krope_t, a, H, Ld, KL, ROPE):
    DB, R, QW = q_s.shape
    KN = knew.shape[1]
    n_pages = page_table.shape[1]
    PG = cache_ckv.shape[2]
    PP = DECODE_PAGES_PER_STEP
    while n_pages % PP:
        PP //= 2
    nj = n_pages // PP

    def kern(pt_ref, q_ref, kn_ref, *rest):
        ck_refs, kr_refs = rest[:PP], rest[PP:2 * PP]
        o_ref, m_sc, l_sc, acc_sc = rest[2 * PP:]
        j = pl.program_id(1)

        @pl.when(j == 0)
        def _():
            m_sc[...] = jnp.full((R, 1), NEG_BIG, F32)
            l_sc[...] = jnp.zeros((R, 1), F32)
            acc_sc[...] = jnp.zeros((R, KL), F32)

        q = q_ref[0]
        qa, qp = q[:, :KL], q[:, KL:KL + ROPE]

        def update(s, v):
            m_prev = m_sc[...]
            m_new = jnp.maximum(m_prev, jnp.max(s, axis=-1, keepdims=True))
            alpha = jnp.exp2(m_prev - m_new)
            p = jnp.exp2(s - m_new)
            l_sc[...] = alpha * l_sc[...] + jnp.sum(p, axis=-1, keepdims=True)
            acc_sc[...] = alpha * acc_sc[...] + _dot(p.astype(BF16), v)
            m_sc[...] = m_new

        ck = jnp.concatenate([r[...].astype(BF16) for r in ck_refs], axis=0)
        kr_t = jnp.concatenate([r[...].astype(BF16) for r in kr_refs], axis=1)
        update(lax.dot_general(qa, ck, NT_DIMS, preferred_element_type=F32) + _dot(qp, kr_t), ck)

        @pl.when(j == nj - 1)
        def _():
            kn = kn_ref[0]
            s = lax.dot_general(q, kn, NT_DIMS, preferred_element_type=F32)
            t_idx = lax.broadcasted_iota(jnp.int32, (R, KN), 0) // H
            s_idx = lax.broadcasted_iota(jnp.int32, (R, KN), 1)
            s = jnp.where(s_idx <= t_idx, s, NEG_BIG)
            update(s, kn[:, :KL])
            o_ref[0] = (acc_sc[...] / l_sc[...]).astype(BF16)

    def page_spec(shape, p_i):
        return pl.BlockSpec((None, None) + shape,
                            lambda b, j, pt, p_i=p_i: (a, pt[b, j * PP + p_i], 0, 0))

    in_specs = [pl.BlockSpec((1, R, QW), lambda b, j, pt: (b, 0, 0)),
                pl.BlockSpec((1, KN, QW), lambda b, j, pt: (b, 0, 0))]
    in_specs += [page_spec((PG, KL), p_i) for p_i in range(PP)]
    in_specs += [page_spec((ROPE, PG), p_i) for p_i in range(PP)]
    grid_spec = pltpu.PrefetchScalarGridSpec(
        num_scalar_prefetch=1, grid=(DB, nj), in_specs=in_specs,
        out_specs=pl.BlockSpec((1, R, KL), lambda b, j, pt: (b, 0, 0)),
        scratch_shapes=[pltpu.VMEM((R, 1), F32), pltpu.VMEM((R, 1), F32), pltpu.VMEM((R, KL), F32)])
    return pl.pallas_call(
        kern, grid_spec=grid_spec, out_shape=jax.ShapeDtypeStruct((DB, R, KL), BF16),
        compiler_params=_cparams(40), name="decode_attention")(
            page_table, q_s, knew, *([cache_ckv] * PP), *([cache_krope_t] * PP))


def _ctx_to_attn_out(ctx, wuv, H, KL):
    T = ctx.shape[0]
    tm = _tile(T, LINEAR_TM)

    def kern(c_ref, w_ref, o_ref):
        for h in range(H):
            o_ref[:, h * V_DIM:(h + 1) * V_DIM] = _dot(
                c_ref[:, h * KL:(h + 1) * KL], w_ref[h]).astype(BF16)

    return pl.pallas_call(
        kern, grid=(T // tm,),
        in_specs=[pl.BlockSpec((tm, H * KL), lambda i: (i, 0)),
                  pl.BlockSpec(wuv.shape, lambda i: (0, 0, 0))],
        out_specs=pl.BlockSpec((tm, H * V_DIM), lambda i: (i, 0)),
        out_shape=jax.ShapeDtypeStruct((T, H * V_DIM), BF16),
        compiler_params=_cparams(40), name="ctx_to_attn_out")(ctx, wuv)


def _linear_residual(xs, ws, res, name):
    T, D = res.shape
    tm = _tile(T, LINEAR_TM)

    def epilogue(acc, x0, r_refs, c_refs, o_refs):
        o_refs[0][...] = r_refs[0][...] + acc

    (out,) = _fused_linear(xs, ws, tm=tm, epilogue=epilogue,
                           out_shapes=(jax.ShapeDtypeStruct((T, D), F32),),
                           out_specs=(pl.BlockSpec((tm, D), lambda i: (i, 0)),),
                           row_extras=(res,), name=name)
    return out


def _ffn(h, gain, w_gu, w_down):
    T, D = h.shape
    F = w_down.shape[0]
    tm = _tile(T, FFN_TM)
    tf = _tile(F, FFN_TF, LANES)
    nf = F // tf

    def kern(x_ref, g_ref, wg_ref, wu_ref, wd_ref, o_ref, xn_sc):
        f = pl.program_id(1)

        @pl.when(f == 0)
        def _():
            x = x_ref[...]
            xn_sc[...] = _rms(x, g_ref[...]).astype(BF16)
            o_ref[...] = x

        xn = xn_sc[...]
        act = (_silu(_dot(xn, wg_ref[...])) * _dot(xn, wu_ref[...])).astype(BF16)
        o_ref[...] += _dot(act, wd_ref[...])

    return pl.pallas_call(
        kern, grid=(T // tm, nf),
        in_specs=[pl.BlockSpec((tm, D), lambda i, f: (i, 0)),
                  pl.BlockSpec(gain.shape, lambda i, f: (0, 0)),
                  pl.BlockSpec((D, tf), lambda i, f: (0, f)),
                  pl.BlockSpec((D, tf), lambda i, f: (0, nf + f)),
                  pl.BlockSpec((tf, D), lambda i, f: (f, 0))],
        out_specs=pl.BlockSpec((tm, D), lambda i, f: (i, 0)),
        out_shape=jax.ShapeDtypeStruct((T, D), F32),
        scratch_shapes=[pltpu.VMEM((tm, D), BF16)],
        compiler_params=_cparams(48), name="ffn")(h, gain, w_gu, w_gu, w_down)


def _hgrn_in_proj(h, gain, w_in, lb):
    T, D = h.shape
    HK = w_in.shape[1] // 4
    tm = _tile(T, LINEAR_TM)

    def call(col, epilogue, dtypes, extras=()):
        out_shapes = tuple(jax.ShapeDtypeStruct((T, HK), dt) for dt in dtypes)
        out_specs = tuple(pl.BlockSpec((tm, HK), lambda i: (i, 0)) for _ in dtypes)
        return _fused_linear([h], [w_in], tm=tm, gain=gain, epilogue=epilogue,
                             wspecs=[pl.BlockSpec((D, HK), lambda i: (0, col))],
                             out_shapes=out_shapes, out_specs=out_specs, const_extras=extras,
                             name=f"hgrn_in_proj_{col}")

    def ep_silu(acc, x0, r_refs, c_refs, o_refs):
        o_refs[0][...] = _silu(acc).astype(BF16)

    def ep_ident(acc, x0, r_refs, c_refs, o_refs):
        o_refs[0][...] = acc.astype(BF16)

    def ep_gate(acc, x0, r_refs, c_refs, o_refs):
        k = (1.0 - c_refs[0][...]) / (1.0 + jnp.exp(acc))
        o_refs[0][...] = k.astype(BF16)
        o_refs[1][...] = jnp.log1p(-k)

    (qs,) = call(0, ep_silu, (BF16,))
    kk, logf = call(1, ep_gate, (BF16, F32), extras=(lb,))
    (vv,) = call(2, ep_ident, (BF16,))
    (gs,) = call(3, ep_silu, (BF16,))
    return qs, kk, logf, vv, gs


def _gla_band(qf, kf, vf, fg, n_diag, lane_sum):
    C = qf.shape[0]
    r8 = lax.broadcasted_iota(jnp.int32, qf.shape, 0) % SUBLANES
    gprod = None
    parts = []
    for d in range(n_diag):
        if d == 0:
            p = qf * kf
        else:
            fr = fg if d == 1 else pltpu.roll(fg, d - 1, 0)
            gprod = fr if gprod is None else gprod * fr
            p = jnp.where(r8 >= d, qf * gprod * pltpu.roll(kf, d, 0), 0.0)
        parts.append(p)
    sums = lane_sum(parts)
    out = None
    for d in range(n_diag):
        vr = vf if d == 0 else pltpu.roll(vf, d, 0)
        term = sums[d] * vr
        out = term if out is None else out + term
    return out


def _gla_prompt(qs, kk, logf, vv, gs, hg_norm, B, L, H):
    T, HK = qs.shape
    K = HK // H
    ct = _tile(L, GLA_CT, GLA_CHUNK)
    C = min(GLA_CHUNK, ct)
    nct = L // ct
    ncc = ct // C
    HPS = GLA_HEADS_PER_STEP if H % GLA_HEADS_PER_STEP == 0 else 1

    def kern(q_ref, k_ref, lf_ref, v_ref, g_ref, hn_ref, o_ref, s_ref, st_sc):
        ci = pl.program_id(2)

        @pl.when(ci == 0)
        def _():
            st_sc[...] = jnp.zeros((HPS, K, K), F32)

        rows = lax.broadcasted_iota(jnp.int32, (C, C), 0)
        cols = lax.broadcasted_iota(jnp.int32, (C, C), 1)
        tril = (rows >= cols).astype(F32)
        ones_bf = jnp.ones((K, K), BF16)

        def lane_sum(parts):
            r = _dot(jnp.concatenate([p.astype(BF16) for p in parts], axis=0), ones_bf)
            return [r[d * C:(d + 1) * C] for d in range(len(parts))]

        def chunk(c, hh, st):
            sl, hl = slice(c * C, (c + 1) * C), slice(hh * K, (hh + 1) * K)
            qf, kf, vf = q_ref[sl, hl].astype(F32), k_ref[sl, hl].astype(F32), v_ref[sl, hl].astype(F32)
            lf = lf_ref[sl, hl]
            fg = jnp.exp(lf)
            b = jnp.dot(tril, lf, precision=lax.Precision.HIGHEST, preferred_element_type=F32)
            blast = b[C - 1:C, :]
            qb = qf * jnp.exp(b)
            kb = kf * jnp.exp(blast - b)
            o = lax.dot_general(qb.astype(BF16), st.astype(BF16), NT_DIMS, preferred_element_type=F32)
            amat = None
            blk = C // 2
            while blk >= SUBLANES:
                q_parts, k_parts = [], []
                for m in range(C // blk):
                    r = slice(m * blk, (m + 1) * blk)
                    if m % 2 == 1:
                        ref = b[m * blk - 1:m * blk, :]
                        q_parts.append(qf[r] * jnp.exp(b[r] - ref))
                        k_parts.append(jnp.zeros((blk, K), F32))
                    else:
                        ref = b[(m + 1) * blk - 1:(m + 1) * blk, :]
                        q_parts.append(jnp.zeros((blk, K), F32))
                        k_parts.append(kf[r] * jnp.exp(ref - b[r]))
                ql = jnp.concatenate(q_parts, axis=0).astype(BF16)
                kl = jnp.concatenate(k_parts, axis=0).astype(BF16)
                al = lax.dot_general(ql, kl, NT_DIMS, preferred_element_type=F32)
                al = jnp.where((rows // (2 * blk)) == (cols // (2 * blk)), al, 0.0)
                amat = al if amat is None else amat + al
                blk //= 2
            if amat is not None:
                o = o + _dot(amat.astype(BF16), vf.astype(BF16))
            o = o + _gla_band(qf, kf, vf, fg, min(SUBLANES, C), lane_sum)
            st = st * jnp.exp(blast) + lax.dot_general(
                vf.astype(BF16), kb.astype(BF16), TN_DIMS, preferred_element_type=F32)
            on = _rms(o, hn_ref[...]) * g_ref[sl, hl].astype(F32)
            o_ref[sl, hl] = on.astype(BF16)
            return st

        sts = [st_sc[hh] for hh in range(HPS)]
        for c in range(ncc):
            sts = [chunk(c, hh, sts[hh]) for hh in range(HPS)]
        for hh in range(HPS):
            st_sc[hh] = sts[hh]

        @pl.when(ci == nct - 1)
        def _():
            for hh in range(HPS):
                s_ref[0, hh] = sts[hh].T

    tok = lambda b, h, ci: (b * nct + ci, h)
    return pl.pallas_call(
        kern, grid=(B, H // HPS, nct),
        in_specs=[pl.BlockSpec((ct, HPS * K), tok)] * 5 + [pl.BlockSpec(hg_norm.shape, lambda b, h, ci: (0, 0))],
        out_specs=(pl.BlockSpec((ct, HPS * K), tok),
                   pl.BlockSpec((1, HPS, K, K), lambda b, h, ci: (b, h, 0, 0))),
        out_shape=(jax.ShapeDtypeStruct((T, HK), BF16), jax.ShapeDtypeStruct((B, H, K, K), F32)),
        scratch_shapes=[pltpu.VMEM((HPS, K, K), F32)],
        compiler_params=_cparams(32), name="gla_prompt")(qs, kk, logf, vv, gs, hg_norm)


def _gla_sample(qs, kk, logf, vv, gs, hg_norm, s0, DB, Ld, H):
    HK = qs.shape[2]
    K = HK // H
    C = SUBLANES
    assert Ld <= C

    def kern(q_ref, k_ref, lf_ref, v_ref, g_ref, hn_ref, s0_ref, o_ref, s_ref, pad_sc):
        rows = lax.broadcasted_iota(jnp.int32, (C, C), 0)
        cols = lax.broadcasted_iota(jnp.int32, (C, C), 1)
        tril = (rows >= cols).astype(F32)

        def padded(ref, slot):
            pad_sc[slot] = jnp.zeros((C, HK), F32)
            pad_sc[slot, 0:Ld, :] = ref[0].astype(F32)
            return pad_sc[slot]

        qf, kf, lf, vf = padded(q_ref, 0), padded(k_ref, 1), padded(lf_ref, 2), padded(v_ref, 3)
        fg = jnp.exp(lf)
        b = jnp.dot(tril, lf, precision=lax.Precision.HIGHEST, preferred_element_type=F32)
        blast = b[C - 1:C, :]
        qb = (qf * jnp.exp(b)).astype(BF16)
        kb = (kf * jnp.exp(blast - b)).astype(BF16)
        dec = jnp.exp(blast)
        vb = vf.astype(BF16)

        def lane_sum(parts):
            return parts

        for h in range(H):
            hl = slice(h * K, (h + 1) * K)

            def head_sum(parts):
                return [jnp.sum(p, axis=-1, keepdims=True) for p in parts]

            st = s0_ref[0, h].T
            o = lax.dot_general(qb[:, hl], st.astype(BF16), NT_DIMS, preferred_element_type=F32)
            o = o + _gla_band(qf[:, hl], kf[:, hl], vf[:, hl], fg[:, hl], Ld, head_sum)
            st = st * dec[:, hl] + lax.dot_general(vb[:, hl], kb[:, hl], TN_DIMS,
                                                   preferred_element_type=F32)
            s_ref[0, h] = st.T
            on = _rms(o[0:Ld], hn_ref[...]) * g_ref[0, :, hl].astype(F32)
            o_ref[0, :, hl] = on.astype(BF16)

    tok = pl.BlockSpec((1, Ld, HK), lambda b: (b, 0, 0))
    st_spec = pl.BlockSpec((1, H, K, K), lambda b: (b, 0, 0, 0))
    return pl.pallas_call(
        kern, grid=(DB,),
        in_specs=[tok] * 5 + [pl.BlockSpec(hg_norm.shape, lambda b: (0, 0)), st_spec],
        out_specs=(tok, st_spec),
        out_shape=(jax.ShapeDtypeStruct((DB, Ld, HK), BF16), jax.ShapeDtypeStruct(s0.shape, F32)),
        scratch_shapes=[pltpu.VMEM((4, C, HK), F32)],
        compiler_params=_cparams(32), name="gla_sample")(qs, kk, logf, vv, gs, hg_norm, s0)


def _router(h, gain, w_router, b_router):
    T, D = h.shape
    E = w_router.shape[1]
    tm = _tile(T, LINEAR_TM)
    assert TOP_K == 2

    def kern(x_ref, g_ref, w_ref, b_ref, i_ref, o_ref):
        xn = _rms(x_ref[...], g_ref[...])
        logits = jnp.dot(xn, w_ref[...], precision=lax.Precision.HIGHEST,
                         preferred_element_type=F32) + b_ref[...]
        idx = lax.broadcasted_iota(jnp.int32, logits.shape, 1)
        m1 = jnp.max(logits, axis=-1, keepdims=True)
        i1 = jnp.min(jnp.where(logits == m1, idx, E), axis=-1, keepdims=True)
        rest = jnp.where(idx == i1, -jnp.inf, logits)
        m2 = jnp.max(rest, axis=-1, keepdims=True)
        i2 = jnp.min(jnp.where(rest == m2, idx, E), axis=-1, keepdims=True)
        e2 = jnp.exp(m2 - m1)
        g1 = 1.0 / (1.0 + e2)
        slot = lax.broadcasted_iota(jnp.int32, (tm, TOP_K), 1)
        i_ref[...] = jnp.where(slot == 0, i1, i2)
        o_ref[...] = jnp.where(slot == 0, g1, e2 * g1)

    return pl.pallas_call(
        kern, grid=(T // tm,),
        in_specs=[pl.BlockSpec((tm, D), lambda i: (i, 0)),
                  pl.BlockSpec(gain.shape, lambda i: (0, 0)),
                  pl.BlockSpec(w_router.shape, lambda i: (0, 0)),
                  pl.BlockSpec(b_router.shape, lambda i: (0, 0))],
        out_specs=(pl.BlockSpec((tm, TOP_K), lambda i: (i, 0)), pl.BlockSpec((tm, TOP_K), lambda i: (i, 0))),
        out_shape=(jax.ShapeDtypeStruct((T, TOP_K), jnp.int32), jax.ShapeDtypeStruct((T, TOP_K), F32)),
        compiler_params=_cparams(32), name="router")(h, gain, w_router, b_router)


def _route_plan(idx, E, tmx):
    T = idx.shape[0]
    NP = TOP_K * T + E * tmx
    assert NP % tmx == 0
    NT = NP // tmx
    sel = jnp.any(idx[:, :, None] == jnp.arange(E, dtype=jnp.int32), axis=1).astype(jnp.int32)
    incl = jnp.cumsum(sel, axis=0)
    counts = incl[-1]
    padded = ((counts + tmx - 1) // tmx) * tmx
    ends = jnp.cumsum(padded)
    offs = ends - padded
    rank = jnp.take_along_axis(incl - sel, idx, axis=1)
    pos = (offs[idx] + rank).reshape(-1).astype(jnp.int32)
    tok = jnp.repeat(jnp.arange(T, dtype=jnp.int32), TOP_K)
    src = jnp.zeros((NP,), jnp.int32).at[pos].set(tok)
    n_used = (ends[-1] // tmx).astype(jnp.int32)
    tile_start = jnp.minimum(jnp.arange(NT, dtype=jnp.int32), n_used - 1) * tmx
    tile_expert = jnp.minimum(jnp.searchsorted(ends, tile_start, side="right"), E - 1).astype(jnp.int32)
    return pos, src, tile_expert, n_used.reshape(1)


def _moe_experts(h, gain, src, tile_expert, n_used, w_gu, w_down, tmx):
    T, D = h.shape
    E, F = w_down.shape[0], w_down.shape[1]
    NP = src.shape[0]
    NT = NP // tmx
    tf = _tile(F, MOE_TF, LANES)
    nf = F // tf

    def kern(src_ref, te_ref, nu_ref, h_ref, g_ref, wg_ref, wu_ref, wd_ref, o_ref, xbuf, xn_sc, sem):
        i, f = pl.program_id(0), pl.program_id(1)
        n_used_v = nu_ref[0]

        def row_copy(tile, r):
            return pltpu.make_async_copy(h_ref.at[pl.ds(src_ref[tile * tmx + r], 1)],
                                         xbuf.at[pl.ds(r, 1)], sem)

        def start_gather(tile):
            lax.fori_loop(0, tmx, lambda r, c: (row_copy(tile, r).start(), c)[1], 0, unroll=DMA_LOOP_UNROLL)

        @pl.when((f == 0) & (i == 0))
        def _():
            start_gather(0)

        @pl.when((f == 0) & (i < n_used_v))
        def _():
            lax.fori_loop(0, tmx, lambda r, c: (row_copy(i, r).wait(), c)[1], 0, unroll=DMA_LOOP_UNROLL)
            xn_sc[...] = _rms(xbuf[...], g_ref[...]).astype(BF16)

            @pl.when(i + 1 < n_used_v)
            def _():
                start_gather(i + 1)

        @pl.when(f == 0)
        def _():
            o_ref[...] = jnp.zeros((tmx, D), F32)

        @pl.when(i < n_used_v)
        def _():
            xn = xn_sc[...]
            act = (_silu(_dot(xn, wg_ref[0])) * _dot(xn, wu_ref[0])).astype(BF16)
            o_ref[...] += _dot(act, wd_ref[0])

    def wspec(shape, fn):
        def index_map(i, f, src_r, te_r, nu_r):
            return fn(te_r[i], jnp.where(i < nu_r[0], f, nf - 1))
        return pl.BlockSpec(shape, index_map)

    grid_spec = pltpu.PrefetchScalarGridSpec(
        num_scalar_prefetch=3, grid=(NT, nf),
        in_specs=[pl.BlockSpec(memory_space=pl.ANY),
                  pl.BlockSpec(gain.shape, lambda i, f, *_: (0, 0)),
                  wspec((1, D, tf), lambda e, f: (e, 0, f)),
                  wspec((1, D, tf), lambda e, f: (e, 0, nf + f)),
                  wspec((1, tf, D), lambda e, f: (e, f, 0))],
        out_specs=pl.BlockSpec((tmx, D), lambda i, f, *_: (i, 0)),
        scratch_shapes=[pltpu.VMEM((tmx, D), F32), pltpu.VMEM((tmx, D), BF16),
                        pltpu.SemaphoreType.DMA(())])
    return pl.pallas_call(
        kern, grid_spec=grid_spec, out_shape=jax.ShapeDtypeStruct((NP, D), F32),
        compiler_params=_cparams(48), name="moe_experts")(
            src, tile_expert, n_used, h, gain, w_gu, w_gu, w_down)


def _moe_combine(h, y_sorted, pos, gate, final_gain):
    T, D = h.shape
    tm = _tile(T, MOE_COMBINE_TM)
    nt = T // tm

    def kern(pos_ref, h_ref, y_ref, gt_ref, fg_ref, o_ref, ybuf, sems):
        i = pl.program_id(0)

        def row_copy(step, r, s):
            slot = step % 2
            return pltpu.make_async_copy(y_ref.at[pl.ds(pos_ref[(step * tm + r) * TOP_K + s], 1)],
                                         ybuf.at[slot, s, pl.ds(r, 1)], sems.at[slot])

        def for_rows(step, fn):
            def body(r, c):
                for s in range(TOP_K):
                    fn(row_copy(step, r, s))
                return c
            lax.fori_loop(0, tm, body, 0, unroll=DMA_LOOP_UNROLL)

        @pl.when(i == 0)
        def _():
            for_rows(0, lambda cp: cp.start())

        @pl.when(i + 1 < nt)
        def _():
            for_rows(i + 1, lambda cp: cp.start())

        for_rows(i, lambda cp: cp.wait())
        slot = i % 2
        gt = gt_ref[...]
        y = h_ref[...]
        for s in range(TOP_K):
            y = y + gt[:, s:s + 1] * ybuf[slot, s]
        o_ref[...] = _rms(y, fg_ref[...])

    grid_spec = pltpu.PrefetchScalarGridSpec(
        num_scalar_prefetch=1, grid=(nt,),
        in_specs=[pl.BlockSpec((tm, D), lambda i, *_: (i, 0)),
                  pl.BlockSpec(memory_space=pl.ANY),
                  pl.BlockSpec((tm, TOP_K), lambda i, *_: (i, 0)),
                  pl.BlockSpec(final_gain.shape, lambda i, *_: (0, 0))],
        out_specs=pl.BlockSpec((tm, D), lambda i, *_: (i, 0)),
        scratch_shapes=[pltpu.VMEM((2, TOP_K, tm, D), F32), pltpu.SemaphoreType.DMA((2,))])
    return pl.pallas_call(
        kern, grid_spec=grid_spec, out_shape=jax.ShapeDtypeStruct((T, D), F32),
        compiler_params=_cparams(40), name="moe_combine")(pos, h, y_sorted, gate, final_gain)


def _moe(h, gain, w_router, b_router, w_gu, w_down, final_gain):
    T = h.shape[0]
    E = w_down.shape[0]
    tmx = _tile(max(MOE_MIN_TILE, TOP_K * T // E), MOE_TM)
    idx, gate = _router(h, gain, w_router, b_router)
    pos, src, tile_expert, n_used = _route_plan(idx, E, tmx)
    y_sorted = _moe_experts(h, gain, src, tile_expert, n_used, w_gu, w_down, tmx)
    return _moe_combine(h, y_sorted, pos, gate, final_gain)


def _rope_rows(pos, rope_dim):
    inv = ROPE_THETA ** (-jnp.arange(0, rope_dim, 2, dtype=F32) / rope_dim)
    ang = pos.astype(F32)[:, None] * inv[None, :]
    z = jnp.zeros((pos.shape[0], LANES - rope_dim), F32)
    cos, sin = jnp.cos(ang), jnp.sin(ang)
    return jnp.concatenate([cos, cos, z], axis=-1), jnp.concatenate([sin, sin, z], axis=-1)


def _rot_cols(w):
    half = w.shape[-1] // 2
    return jnp.concatenate([-w[..., half:], w[..., :half]], axis=-1)


def _pad_lanes(w):
    pad = LANES - w.shape[-1]
    return jnp.pad(w, [(0, 0)] * (w.ndim - 1) + [(0, pad)])


def _prep_even(a, w_in_e, w_q_b, w_kv_b, pool_w, w_out_e, w_ffn_gu, w_ffn_down, dims, H):
    DP, QL, KL, ROPE = dims
    w_in = w_in_e[a]
    k_raw = w_in[:, DP + QL + KL:]
    w_ext = jnp.concatenate([w_in[:, :DP + QL + KL], _pad_lanes(k_raw), _pad_lanes(_rot_cols(k_raw))],
                            axis=-1).astype(BF16)
    wq = w_q_b[a].reshape(QL, H, NOPE_DIM + ROPE)
    wq_rope = wq[..., NOPE_DIM:]
    wq_ext = jnp.concatenate([wq[..., :NOPE_DIM].reshape(QL, H * NOPE_DIM),
                              _pad_lanes(wq_rope).reshape(QL, H * LANES),
                              _pad_lanes(_rot_cols(wq_rope)).reshape(QL, H * LANES)], axis=-1).astype(BF16)
    w_kv = w_kv_b[a].reshape(KL, H, NOPE_DIM + V_DIM)
    wuk_t = jnp.transpose(w_kv[..., :NOPE_DIM], (1, 2, 0)).astype(BF16)
    wuv = jnp.transpose(w_kv[..., NOPE_DIM:], (1, 0, 2)).astype(BF16)
    w_out = w_out_e[a].astype(BF16)
    return dict(w_ext=w_ext, wq_ext=wq_ext, wuk_t=wuk_t, wuv=wuv, pool_w=pool_w[a].astype(BF16),
                w_out_pool=w_out[:DP], w_out_attn=w_out[DP:],
                w_gu=w_ffn_gu[a].astype(BF16), w_down=w_ffn_down[a].astype(BF16))


def kernel(x_prompt, x_sample, cache_ckv, cache_krope, page_table, state_pool, state_hgrn,
           norm_mix_e, w_in_e, q_norm, w_q_b, kv_norm, w_kv_b, pool_w, pool_scale, w_out_e,
           norm_ffn_e, w_ffn_gu, w_ffn_down,
           norm_mix_o, w_in_o, hg_lower_bound, hg_norm, w_out_o, norm_ffn_o, w_router, b_router,
           w_exp_gu, w_exp_down, final_norm):
    B, L, D = x_prompt.shape
    DB, Ld, _ = x_sample.shape
    n_pages = page_table.shape[1]
    PG = cache_ckv.shape[2]
    past_len = n_pages * PG
    KL, ROPE = cache_ckv.shape[3], cache_krope.shape[3]
    DP, PS = state_pool.shape[3], state_pool.shape[2]
    QL = q_norm.shape[1]
    H = w_q_b.shape[2] // (NOPE_DIM + ROPE)
    HG = state_hgrn.shape[2]
    depth = hg_lower_bound.shape[0]
    dims = (DP, QL, KL, ROPE)
    scale = float((NOPE_DIM + ROPE) ** -0.5 * math.log2(math.e))
    cache_krope_t = jnp.swapaxes(cache_krope, 2, 3)
    QW = KL + LANES
    KN = 16
    assert Ld <= KN and ROPE <= LANES

    row = lambda v: v.reshape(1, -1).astype(F32)
    lb_p = jax.nn.softmax(hg_lower_bound.astype(F32), axis=0)
    lower_bounds = jnp.cumsum(lb_p, axis=0) - lb_p[0]

    hp = x_prompt.reshape(B * L, D)
    hs = x_sample.reshape(DB * Ld, D)
    cos_p, sin_p = _rope_rows(jnp.tile(jnp.arange(L), B), ROPE)
    cos_s, sin_s = _rope_rows(jnp.tile(past_len + jnp.arange(Ld), DB), ROPE)

    outs_p = dict(ckv=[], krope=[], pool=[], hgrn=[])
    outs_s = dict(ckv=[], krope=[], pool=[], hgrn=[])
    for l in range(depth):
        a = l // 2
        if l % 2 == 0:
            w = _prep_even(a, w_in_e, w_q_b, w_kv_b, pool_w, w_out_e, w_ffn_gu, w_ffn_down, dims, H)
            g_mix, g_q, g_kv = row(norm_mix_e[a]), row(q_norm[a]), row(kv_norm[a])
            g_ffn, p_scale = row(norm_ffn_e[a]), row(pool_scale[a])

            u, qn, ckv, kpe, kcat = _even_in_proj(hp, g_mix, w["w_ext"], g_q, g_kv, cos_p, sin_p, dims)
            pool_out, pool_new = _pool_prompt(u, jnp.zeros((B, PS, DP), F32), w["pool_w"], p_scale, B, L)
            qcat = _q_proj(qn, w["wq_ext"], w["wuk_t"], cos_p, sin_p, H, KL, scale, head_major=True)
            attn = _flash_prompt(qcat, kcat, w["wuv"], B, L, H, KL)
            hp = _linear_residual([pool_out, attn], [w["w_out_pool"], w["w_out_attn"]], hp, "out_proj_e")
            hp = _ffn(hp, g_ffn, w["w_gu"], w["w_down"])
            outs_p["ckv"].append(ckv.reshape(B, L, KL))
            outs_p["krope"].append(kpe.reshape(B, L, ROPE))
            outs_p["pool"].append(pool_new)

            u, qn, ckv, kpe, kcat = _even_in_proj(hs, g_mix, w["w_ext"], g_q, g_kv, cos_s, sin_s, dims)
            u3 = u.reshape(DB, Ld, DP)
            ext_tm = jnp.transpose(jnp.concatenate([state_pool[a], u3], axis=1), (1, 0, 2))
            pool_tm = _pool_sample(ext_tm, w["pool_w"], p_scale, Ld, past_len)
            pool_out = jnp.transpose(pool_tm, (1, 0, 2)).reshape(DB * Ld, DP)
            qrows = _q_proj(qn, w["wq_ext"], w["wuk_t"], cos_s, sin_s, H, KL, scale, head_major=False)
            q_s = qrows.reshape(DB, Ld * H, QW)
            knew = jnp.pad(kcat.reshape(DB, Ld, QW), ((0, 0), (0, KN - Ld), (0, 0)))
            ctx = _decode_attention(page_table, q_s, knew, cache_ckv, cache_krope_t, a, H, Ld, KL, ROPE)
            attn = _ctx_to_attn_out(ctx.reshape(DB * Ld, H * KL), w["wuv"], H, KL)
            hs = _linear_residual([pool_out, attn], [w["w_out_pool"], w["w_out_attn"]], hs, "out_proj_e")
            hs = _ffn(hs, g_ffn, w["w_gu"], w["w_down"])
            outs_s["ckv"].append(ckv.reshape(DB, Ld, KL))
            outs_s["krope"].append(kpe.reshape(DB, Ld, ROPE))
            outs_s["pool"].append(jnp.concatenate([state_pool[a], u3], axis=1)[:, -PS:])
        else:
            w_in = w_in_o[a].astype(BF16)
            w_out = w_out_o[a].astype(BF16)
            w_gu = w_exp_gu[a].astype(BF16)
            w_dn = w_exp_down[a].astype(BF16)
            g_mix, g_ffn, g_hn = row(norm_mix_o[a]), row(norm_ffn_o[a]), row(hg_norm[a])
            lb = row(lower_bounds[l])
            w_r, b_r = w_router[a].astype(F32), row(b_router[a])
            g_fin = row(final_norm) if l == depth - 1 else None
            assert g_fin is not None

            qs, kk, logf, vv, gs = _hgrn_in_proj(hp, g_mix, w_in, lb)
            on, s_new = _gla_prompt(qs, kk, logf, vv, gs, g_hn, B, L, HG)
            hp = _linear_residual([on], [w_out], hp, "out_proj_o")
            hp = _moe(hp, g_ffn, w_r, b_r, w_gu, w_dn, g_fin)
            outs_p["hgrn"].append(s_new)

            qs, kk, logf, vv, gs = _hgrn_in_proj(hs, g_mix, w_in, lb)
            r3 = lambda t: t.reshape(DB, Ld, -1)
            on, s_new = _gla_sample(r3(qs), r3(kk), r3(logf), r3(vv), r3(gs), g_hn, state_hgrn[a], DB, Ld, HG)
            hs = _linear_residual([on.reshape(DB * Ld, -1)], [w_out], hs, "out_proj_o")
            hs = _moe(hs, g_ffn, w_r, b_r, w_gu, w_dn, g_fin)
            outs_s["hgrn"].append(s_new)

    return (hp.reshape(B, L, D), hs.reshape(DB, Ld, D),
            jnp.stack(outs_p["ckv"]), jnp.stack(outs_p["krope"]), jnp.stack(outs_p["pool"]),
            jnp.stack(outs_p["hgrn"]),
            jnp.stack(outs_s["ckv"]), jnp.stack(outs_s["krope"]), jnp.stack(outs_s["pool"]),
            jnp.stack(outs_s["hgrn"]))
```

```python
import functools
import math

import jax
import jax.numpy as jnp
from jax import lax
from jax.experimental import pallas as pl
from jax.experimental.pallas import tpu as pltpu

F32 = jnp.float32
BF16 = jnp.bfloat16

EPS = 1e-6
POOL_WINDOWS = (2, 4, 8, 16)
NOPE_DIM = 128
V_DIM = 128
ROPE_THETA = 10000.0
TOP_K = 2
LANES = 128
SUBLANES = 8
NEG_BIG = -1e30

LINEAR_TM = 512
FFN_TM = 512
FFN_TF = 512
MOE_TM = 512
MOE_MIN_TILE = 16
MOE_TF = 256
MOE_COMBINE_TM = 256
DMA_LOOP_UNROLL = 8
FLASH_TQ = 128
FLASH_TK = 512
POOL_TL = 512
GLA_CT = 256
GLA_CHUNK = 256
GLA_HEADS_PER_STEP = 2
DECODE_PAGES_PER_STEP = 16

NT_DIMS = (((1,), (1,)), ((), ()))
TN_DIMS = (((0,), (0,)), ((), ()))


def _cparams(vmem_mb):
    return pltpu.CompilerParams(vmem_limit_bytes=vmem_mb * 2 ** 20)


def _tile(n, pref, mult=SUBLANES):
    if n <= pref:
        return n
    for t in range(pref, 0, -1):
        if n % t == 0 and t % mult == 0:
            return t
    return n


def _rms(x, g):
    return x * lax.rsqrt(jnp.mean(x * x, axis=-1, keepdims=True) + EPS) * g


def _silu(x):
    return x / (1.0 + jnp.exp(-x))


def _dot(a, b):
    return jnp.dot(a, b, preferred_element_type=F32)


def _fused_linear(xs, ws, *, tm, epilogue, out_shapes, out_specs, gain=None, wspecs=None,
                  row_extras=(), const_extras=(), vmem_mb=48, name="linear"):
    T = xs[0].shape[0]
    grid = (T // tm,)
    n_x, n_w, n_r, n_c = len(xs), len(ws), len(row_extras), len(const_extras)

    def kern(*refs):
        pos = 0
        x_refs = refs[pos:pos + n_x]; pos += n_x
        gain_ref = None
        if gain is not None:
            gain_ref = refs[pos]; pos += 1
        w_refs = refs[pos:pos + n_w]; pos += n_w
        r_refs = refs[pos:pos + n_r]; pos += n_r
        c_refs = refs[pos:pos + n_c]; pos += n_c
        o_refs = refs[pos:]
        acc = None
        x0 = None
        for xr, wr in zip(x_refs, w_refs):
            xv = xr[...]
            if gain_ref is not None:
                x0 = xv
                xv = _rms(xv, gain_ref[...]).astype(BF16)
            d = _dot(xv, wr[...])
            acc = d if acc is None else acc + d
        epilogue(acc, x0, r_refs, c_refs, o_refs)

    in_specs = [pl.BlockSpec((tm, x.shape[1]), lambda i: (i, 0)) for x in xs]
    args = list(xs)
    if gain is not None:
        in_specs.append(pl.BlockSpec(gain.shape, lambda i: (0, 0)))
        args.append(gain)
    if wspecs is None:
        wspecs = [pl.BlockSpec(w.shape, lambda i: (0,) * w.ndim) for w in ws]
    in_specs += list(wspecs)
    args += list(ws)
    for r in row_extras:
        in_specs.append(pl.BlockSpec((tm, r.shape[1]), lambda i: (i, 0)))
        args.append(r)
    for c in const_extras:
        in_specs.append(pl.BlockSpec(c.shape, lambda i, nd=c.ndim: (0,) * nd))
        args.append(c)
    return pl.pallas_call(
        kern, grid=grid, in_specs=in_specs, out_specs=out_specs, out_shape=out_shapes,
        compiler_params=_cparams(vmem_mb), name=name)(*args)


def _even_in_proj(h, gain, w_ext, qn_g, kvn_g, cos_rows, sin_rows, dims, with_ckv_t):
    DP, QL, KL, ROPE = dims
    T, D = h.shape
    tm = _tile(T, LINEAR_TM)
    o_kr = DP + QL + KL

    def epilogue(acc, x0, r_refs, c_refs, o_refs):
        u_ref, qn_ref, ckv_ref, kpe_ref, kcat_ref = o_refs[:5]
        cos, sin = r_refs[0][...], r_refs[1][...]
        u_ref[...] = acc[:, :DP]
        qn_ref[...] = _rms(acc[:, DP:DP + QL], c_refs[0][...]).astype(BF16)
        ckv = _rms(acc[:, DP + QL:o_kr], c_refs[1][...])
        ckv_ref[...] = ckv
        kpe = acc[:, o_kr:o_kr + LANES] * cos + acc[:, o_kr + LANES:o_kr + 2 * LANES] * sin
        kpe_ref[...] = kpe[:, :ROPE]
        kcat_ref[:, :KL] = ckv.astype(BF16)
        kcat_ref[:, KL:] = kpe.astype(BF16)
        if with_ckv_t:
            o_refs[5][...] = ckv.T.astype(BF16)

    out_shapes = (jax.ShapeDtypeStruct((T, DP), F32), jax.ShapeDtypeStruct((T, QL), BF16),
                  jax.ShapeDtypeStruct((T, KL), F32), jax.ShapeDtypeStruct((T, ROPE), F32),
                  jax.ShapeDtypeStruct((T, KL + LANES), BF16))
    out_specs = tuple(pl.BlockSpec((tm, s.shape[1]), lambda i: (i, 0)) for s in out_shapes)
    if with_ckv_t:
        out_shapes += (jax.ShapeDtypeStruct((KL, T), BF16),)
        out_specs += (pl.BlockSpec((KL, tm), lambda i: (0, i)),)
    return _fused_linear([h], [w_ext], tm=tm, gain=gain, epilogue=epilogue,
                         out_shapes=out_shapes, out_specs=out_specs,
                         row_extras=(cos_rows, sin_rows), const_extras=(qn_g, kvn_g),
                         name="even_in_proj")


def _q_proj(qn, wq_ext, wuk_t, cos_rows, sin_rows, H, KL, scale, head_major):
    T = qn.shape[0]
    tm = _tile(T, LINEAR_TM)
    QW = KL + LANES
    r0, r1 = H * NOPE_DIM, 2 * H * NOPE_DIM

    def epilogue(acc, x0, r_refs, c_refs, o_refs):
        (o_ref,) = o_refs
        cos, sin = r_refs[0][...], r_refs[1][...]
        wuk_ref = c_refs[0]
        for h in range(H):
            qn_h = acc[:, h * NOPE_DIM:(h + 1) * NOPE_DIM].astype(BF16)
            q_abs = (_dot(qn_h, wuk_ref[h]) * scale).astype(BF16)
            q_pe = ((acc[:, r0 + h * LANES:r0 + (h + 1) * LANES] * cos
                     + acc[:, r1 + h * LANES:r1 + (h + 1) * LANES] * sin) * scale).astype(BF16)
            if head_major:
                o_ref[h, :, :KL] = q_abs
                o_ref[h, :, KL:] = q_pe
            else:
                o_ref[:, h * QW:h * QW + KL] = q_abs
                o_ref[:, h * QW + KL:(h + 1) * QW] = q_pe

    if head_major:
        out_shape = jax.ShapeDtypeStruct((H, T, QW), BF16)
        out_spec = pl.BlockSpec((H, tm, QW), lambda i: (0, i, 0))
    else:
        out_shape = jax.ShapeDtypeStruct((T, H * QW), BF16)
        out_spec = pl.BlockSpec((tm, H * QW), lambda i: (i, 0))
    (out,) = _fused_linear([qn], [wq_ext], tm=tm, epilogue=epilogue, out_shapes=(out_shape,),
                           out_specs=(out_spec,), row_extras=(cos_rows, sin_rows),
                           const_extras=(wuk_t,), name="q_proj")
    return out


def _pool_group_out(ext_ref, base, tl, pos0, g, w, PG, pw_ref, scale_ref):
    lo, hi = g * PG, (g + 1) * PG
    x = ext_ref[base:base + tl, lo:hi]
    win = x
    for j in range(1, w):
        win = win + ext_ref[base - j:base - j + tl, lo:hi]
    if pos0 is None:
        d = win * (1.0 / w) - x
    else:
        pos = pos0 + lax.broadcasted_iota(jnp.int32, (tl, 1), 0)
        cnt = jnp.minimum(pos + 1, w).astype(F32)
        d = win / cnt - x
    return _dot(d.astype(BF16), pw_ref[g]) * scale_ref[:, lo:hi]


def _pool_prompt(u, past, pool_w, pool_scale, B, L):
    DP = u.shape[1]
    PS = past.shape[1]
    HALO = 16
    G = len(POOL_WINDOWS)
    PG = DP // G
    tl = _tile(L, POOL_TL, HALO)
    nl = L // tl

    def kern(u_ref, halo_ref, past_ref, pw_ref, sc_ref, o_ref, new_ref, ext_ref):
        i = pl.program_id(1)

        @pl.when(i == 0)
        def _():
            ext_ref[0:1, :] = jnp.zeros((1, DP), F32)
            ext_ref[HALO - PS:HALO, :] = past_ref[0]

        @pl.when(i > 0)
        def _():
            ext_ref[0:HALO, :] = halo_ref[...]

        ext_ref[HALO:HALO + tl, :] = u_ref[...]
        for g, w in enumerate(POOL_WINDOWS):
            o_ref[:, g * PG:(g + 1) * PG] = _pool_group_out(
                ext_ref, HALO, tl, i * tl, g, w, PG, pw_ref, sc_ref).astype(BF16)

        @pl.when(i == nl - 1)
        def _():
            new_ref[0] = ext_ref[HALO + tl - PS:HALO + tl, :]

    r = tl // HALO
    return pl.pallas_call(
        kern, grid=(B, nl),
        in_specs=[
            pl.BlockSpec((tl, DP), lambda b, i: (b * nl + i, 0)),
            pl.BlockSpec((HALO, DP), lambda b, i: (jnp.maximum((b * nl + i) * r - 1, 0), 0)),
            pl.BlockSpec((1, PS, DP), lambda b, i: (b, 0, 0)),
            pl.BlockSpec(pool_w.shape, lambda b, i: (0, 0, 0)),
            pl.BlockSpec(pool_scale.shape, lambda b, i: (0, 0)),
        ],
        out_specs=(pl.BlockSpec((tl, DP), lambda b, i: (b * nl + i, 0)),
                   pl.BlockSpec((1, PS, DP), lambda b, i: (b, 0, 0))),
        out_shape=(jax.ShapeDtypeStruct((B * L, DP), BF16), jax.ShapeDtypeStruct((B, PS, DP), F32)),
        scratch_shapes=[pltpu.VMEM((HALO + tl, DP), F32)],
        compiler_params=_cparams(40), name="pool_prompt")(u, u, past, pool_w, pool_scale)


def _pool_sample(ext_tm, pool_w, pool_scale, Ld, start):
    R, DB, DP = ext_tm.shape
    PS = R - Ld
    G = len(POOL_WINDOWS)
    PG = DP // G

    def kern(e_ref, pw_ref, sc_ref, o_ref):
        for t in range(Ld):
            for g, w in enumerate(POOL_WINDOWS):
                lo, hi = g * PG, (g + 1) * PG
                x = e_ref[PS + t, :, lo:hi]
                win = x
                for j in range(1, w):
                    win = win + e_ref[PS + t - j, :, lo:hi]
                cnt = float(min(start + t + 1, w))
                d = win / cnt - x
                o_ref[t, :, lo:hi] = (_dot(d.astype(BF16), pw_ref[g]) * sc_ref[:, lo:hi]).astype(BF16)

    return pl.pallas_call(
        kern, grid=(1,),
        in_specs=[pl.BlockSpec(ext_tm.shape, lambda i: (0, 0, 0)),
                  pl.BlockSpec(pool_w.shape, lambda i: (0, 0, 0)),
                  pl.BlockSpec(pool_scale.shape, lambda i: (0, 0))],
        out_specs=pl.BlockSpec((Ld, DB, DP), lambda i: (0, 0, 0)),
        out_shape=jax.ShapeDtypeStruct((Ld, DB, DP), BF16),
        compiler_params=_cparams(48), name="pool_sample")(ext_tm, pool_w, pool_scale)


def _flash_prompt(qcat, kcat, ckv_t, wuv, B, L, H, KL):
    QW = qcat.shape[2]
    tq = _tile(L, FLASH_TQ, LANES)
    tk = _tile(L, FLASH_TK, LANES)
    assert tk % tq == 0 and tq % LANES == 0
    nq, nk = L // tq, L // tk
    R = H * tq

    def last_needed(qi):
        return (qi * tq + tq - 1) // tk

    def kern(q_ref, k_ref, kt_ref, wuv_ref, o_ref, m_sc, l_sc, acc_sc):
        qi, ki = pl.program_id(1), pl.program_id(2)

        @pl.when(ki == 0)
        def _():
            m_sc[...] = jnp.full((1, R), NEG_BIG, F32)
            l_sc[...] = jnp.zeros((1, R), F32)
            acc_sc[...] = jnp.zeros((KL, R), F32)

        def update(masked):
            q = q_ref[...].reshape(R, QW)
            s = lax.dot_general(k_ref[...], q, NT_DIMS, preferred_element_type=F32)
            if masked:
                kpos = ki * tk + lax.broadcasted_iota(jnp.int32, (tk, R), 0)
                qpos = qi * tq + (lax.broadcasted_iota(jnp.int32, (tk, R), 1) % tq)
                s = jnp.where(kpos <= qpos, s, NEG_BIG)
            m_prev = m_sc[...]
            m_new = jnp.maximum(m_prev, jnp.max(s, axis=0, keepdims=True))
            alpha = jnp.exp2(m_prev - m_new)
            p = jnp.exp2(s - m_new)
            l_sc[...] = alpha * l_sc[...] + jnp.sum(p, axis=0, keepdims=True)
            acc_sc[...] = alpha * acc_sc[...] + _dot(kt_ref[...], p.astype(BF16))
            m_sc[...] = m_new

        pl.when(ki < last_needed(qi))(functools.partial(update, False))
        pl.when(ki == last_needed(qi))(functools.partial(update, True))

        @pl.when(ki == last_needed(qi))
        def _():
            ctx_t = (acc_sc[...] / l_sc[...]).astype(BF16)
            for h in range(H):
                o_ref[:, h * V_DIM:(h + 1) * V_DIM] = lax.dot_general(
                    ctx_t[:, h * tq:(h + 1) * tq], wuv_ref[h], TN_DIMS,
                    preferred_element_type=F32).astype(BF16)

    def kblk(b, qi, ki):
        return b * nk + jnp.minimum(ki, last_needed(qi))

    return pl.pallas_call(
        kern, grid=(B, nq, nk),
        in_specs=[
            pl.BlockSpec((H, tq, QW), lambda b, qi, ki: (0, b * nq + qi, 0)),
            pl.BlockSpec((tk, QW), lambda b, qi, ki: (kblk(b, qi, ki), 0)),
            pl.BlockSpec((KL, tk), lambda b, qi, ki: (0, kblk(b, qi, ki))),
            pl.BlockSpec(wuv.shape, lambda b, qi, ki: (0, 0, 0)),
        ],
        out_specs=pl.BlockSpec((tq, H * V_DIM), lambda b, qi, ki: (b * nq + qi, 0)),
        out_shape=jax.ShapeDtypeStruct((B * L, H * V_DIM), BF16),
        scratch_shapes=[pltpu.VMEM((1, R), F32), pltpu.VMEM((1, R), F32), pltpu.VMEM((KL, R), F32)],
        compiler_params=_cparams(56), name="flash_prompt")(qcat, kcat, ckv_t, wuv)


def _decode_attention(page_table, q_s, knew, cache_ckv, cache_krope_t, a, H, Ld, KL, ROPE):
    DB, R, QW = q_s.shape
    KN = knew.shape[1]
    n_pages = page_table.shape[1]
    PG = cache_ckv.shape[2]
    PP = DECODE_PAGES_PER_STEP
    while n_pages % PP:
        PP //= 2
    nj = n_pages // PP

    def kern(pt_ref, q_ref, kn_ref, *rest):
        ck_refs, kr_refs = rest[:PP], rest[PP:2 * PP]
        o_ref, m_sc, l_sc, acc_sc = rest[2 * PP:]
        j = pl.program_id(1)

        @pl.when(j == 0)
        def _():
            m_sc[...] = jnp.full((R, 1), NEG_BIG, F32)
            l_sc[...] = jnp.zeros((R, 1), F32)
            acc_sc[...] = jnp.zeros((R, KL), F32)

        q = q_ref[0]
        qa, qp = q[:, :KL], q[:, KL:KL + ROPE]

        def update(s, v):
            m_prev = m_sc[...]
            m_new = jnp.maximum(m_prev, jnp.max(s, axis=-1, keepdims=True))
            alpha = jnp.exp2(m_prev - m_new)
            p = jnp.exp2(s - m_new)
            l_sc[...] = alpha * l_sc[...] + jnp.sum(p, axis=-1, keepdims=True)
            acc_sc[...] = alpha * acc_sc[...] + _dot(p.astype(BF16), v)
            m_sc[...] = m_new

        ck = jnp.concatenate([r[...].astype(BF16) for r in ck_refs], axis=0)
        kr_t = jnp.concatenate([r[...].astype(BF16) for r in kr_refs], axis=1)
        update(lax.dot_general(qa, ck, NT_DIMS, preferred_element_type=F32) + _dot(qp, kr_t), ck)

        @pl.when(j == nj - 1)
        def _():
            kn = kn_ref[0]
            s = lax.dot_general(q, kn, NT_DIMS, preferred_element_type=F32)
            t_idx = lax.broadcasted_iota(jnp.int32, (R, KN), 0) // H
            s_idx = lax.broadcasted_iota(jnp.int32, (R, KN), 1)
            s = jnp.where(s_idx <= t_idx, s, NEG_BIG)
            update(s, kn[:, :KL])
            o_ref[0] = (acc_sc[...] / l_sc[...]).astype(BF16)

    def page_spec(shape, p_i):
        return pl.BlockSpec((None, None) + shape,
                            lambda b, j, pt, p_i=p_i: (a, pt[b, j * PP + p_i], 0, 0))

    in_specs = [pl.BlockSpec((1, R, QW), lambda b, j, pt: (b, 0, 0)),
                pl.BlockSpec((1, KN, QW), lambda b, j, pt: (b, 0, 0))]
    in_specs += [page_spec((PG, KL), p_i) for p_i in range(PP)]
    in_specs += [page_spec((ROPE, PG), p_i) for p_i in range(PP)]
    grid_spec = pltpu.PrefetchScalarGridSpec(
        num_scalar_prefetch=1, grid=(DB, nj), in_specs=in_specs,
        out_specs=pl.BlockSpec((1, R, KL), lambda b, j, pt: (b, 0, 0)),
        scratch_shapes=[pltpu.VMEM((R, 1), F32), pltpu.VMEM((R, 1), F32), pltpu.VMEM((R, KL), F32)])
    return pl.pallas_call(
        kern, grid_spec=grid_spec, out_shape=jax.ShapeDtypeStruct((DB, R, KL), BF16),
        compiler_params=_cparams(40), name="decode_attention")(
            page_table, q_s, knew, *([cache_ckv] * PP), *([cache_krope_t] * PP))


def _ctx_to_attn_out(ctx, wuv, H, KL):
    T = ctx.shape[0]
    tm = _tile(T, LINEAR_TM)

    def kern(c_ref, w_ref, o_ref):
        for h in range(H):
            o_ref[:, h * V_DIM:(h + 1) * V_DIM] = _dot(
                c_ref[:, h * KL:(h + 1) * KL], w_ref[h]).astype(BF16)

    return pl.pallas_call(
        kern, grid=(T // tm,),
        in_specs=[pl.BlockSpec((tm, H * KL), lambda i: (i, 0)),
                  pl.BlockSpec(wuv.shape, lambda i: (0, 0, 0))],
        out_specs=pl.BlockSpec((tm, H * V_DIM), lambda i: (i, 0)),
        out_shape=jax.ShapeDtypeStruct((T, H * V_DIM), BF16),
        compiler_params=_cparams(40), name="ctx_to_attn_out")(ctx, wuv)


def _linear_residual(xs, ws, res, name):
    T, D = res.shape
    tm = _tile(T, LINEAR_TM)

    def epilogue(acc, x0, r_refs, c_refs, o_refs):
        o_refs[0][...] = r_refs[0][...] + acc

    (out,) = _fused_linear(xs, ws, tm=tm, epilogue=epilogue,
                           out_shapes=(jax.ShapeDtypeStruct((T, D), F32),),
                           out_specs=(pl.BlockSpec((tm, D), lambda i: (i, 0)),),
                           row_extras=(res,), name=name)
    return out


def _ffn(h, gain, w_gu, w_down):
    T, D = h.shape
    F = w_down.shape[0]
    tm = _tile(T, FFN_TM)
    tf = _tile(F, FFN_TF, LANES)
    nf = F // tf

    def kern(x_ref, g_ref, wg_ref, wu_ref, wd_ref, o_ref, xn_sc):
        f = pl.program_id(1)

        @pl.when(f == 0)
        def _():
            x = x_ref[...]
            xn_sc[...] = _rms(x, g_ref[...]).astype(BF16)
            o_ref[...] = x

        xn = xn_sc[...]
        act = (_silu(_dot(xn, wg_ref[...])) * _dot(xn, wu_ref[...])).astype(BF16)
        o_ref[...] += _dot(act, wd_ref[...])

    return pl.pallas_call(
        kern, grid=(T // tm, nf),
        in_specs=[pl.BlockSpec((tm, D), lambda i, f: (i, 0)),
                  pl.BlockSpec(gain.shape, lambda i, f: (0, 0)),
                  pl.BlockSpec((D, tf), lambda i, f: (0, f)),
                  pl.BlockSpec((D, tf), lambda i, f: (0, nf + f)),
                  pl.BlockSpec((tf, D), lambda i, f: (f, 0))],
        out_specs=pl.BlockSpec((tm, D), lambda i, f: (i, 0)),
        out_shape=jax.ShapeDtypeStruct((T, D), F32),
        scratch_shapes=[pltpu.VMEM((tm, D), BF16)],
        compiler_params=_cparams(48), name="ffn")(h, gain, w_gu, w_gu, w_down)


def _hgrn_in_proj(h, gain, w_in, lb):
    T, D = h.shape
    HK = w_in.shape[1] // 4
    tm = _tile(T, LINEAR_TM)

    def call(col, epilogue, dtypes, extras=()):
        out_shapes = tuple(jax.ShapeDtypeStruct((T, HK), dt) for dt in dtypes)
        out_specs = tuple(pl.BlockSpec((tm, HK), lambda i: (i, 0)) for _ in dtypes)
        return _fused_linear([h], [w_in], tm=tm, gain=gain, epilogue=epilogue,
                             wspecs=[pl.BlockSpec((D, HK), lambda i: (0, col))],
                             out_shapes=out_shapes, out_specs=out_specs, const_extras=extras,
                             name=f"hgrn_in_proj_{col}")

    def ep_silu(acc, x0, r_refs, c_refs, o_refs):
        o_refs[0][...] = _silu(acc).astype(BF16)

    def ep_ident(acc, x0, r_refs, c_refs, o_refs):
        o_refs[0][...] = acc.astype(BF16)

    def ep_gate(acc, x0, r_refs, c_refs, o_refs):
        k = (1.0 - c_refs[0][...]) / (1.0 + jnp.exp(acc))
        o_refs[0][...] = k.astype(BF16)
        o_refs[1][...] = jnp.log1p(-k)

    (qs,) = call(0, ep_silu, (BF16,))
    kk, logf = call(1, ep_gate, (BF16, F32), extras=(lb,))
    (vv,) = call(2, ep_ident, (BF16,))
    (gs,) = call(3, ep_silu, (BF16,))
    return qs, kk, logf, vv, gs


def _gla_band(qf, kf, vf, fg, n_diag, lane_sum):
    C = qf.shape[0]
    r8 = lax.broadcasted_iota(jnp.int32, qf.shape, 0) % SUBLANES
    gprod = None
    parts = []
    for d in range(n_diag):
        if d == 0:
            p = qf * kf
        else:
            fr = fg if d == 1 else pltpu.roll(fg, d - 1, 0)
            gprod = fr if gprod is None else gprod * fr
            p = jnp.where(r8 >= d, qf * gprod * pltpu.roll(kf, d, 0), 0.0)
        parts.append(p)
    sums = lane_sum(parts)
    out = None
    for d in range(n_diag):
        vr = vf if d == 0 else pltpu.roll(vf, d, 0)
        term = sums[d] * vr
        out = term if out is None else out + term
    return out


def _gla_prompt(qs, kk, logf, vv, gs, hg_norm, B, L, H):
    T, HK = qs.shape
    K = HK // H
    ct = _tile(L, GLA_CT, GLA_CHUNK)
    C = min(GLA_CHUNK, ct)
    nct = L // ct
    ncc = ct // C
    HPS = GLA_HEADS_PER_STEP if H % GLA_HEADS_PER_STEP == 0 else 1

    def kern(q_ref, k_ref, lf_ref, v_ref, g_ref, hn_ref, o_ref, s_ref, st_sc):
        ci = pl.program_id(2)

        @pl.when(ci == 0)
        def _():
            st_sc[...] = jnp.zeros((HPS, K, K), F32)

        rows = lax.broadcasted_iota(jnp.int32, (C, C), 0)
        cols = lax.broadcasted_iota(jnp.int32, (C, C), 1)
        tril = (rows >= cols).astype(F32)
        ones_bf = jnp.ones((K, K), BF16)

        def lane_sum(parts):
            r = _dot(jnp.concatenate([p.astype(BF16) for p in parts], axis=0), ones_bf)
            return [r[d * C:(d + 1) * C] for d in range(len(parts))]

        def chunk(c, hh, st):
            sl, hl = slice(c * C, (c + 1) * C), slice(hh * K, (hh + 1) * K)
            qf, kf, vf = q_ref[sl, hl].astype(F32), k_ref[sl, hl].astype(F32), v_ref[sl, hl].astype(F32)
            lf = lf_ref[sl, hl]
            fg = jnp.exp(lf)
            b = jnp.dot(tril, lf, precision=lax.Precision.HIGHEST, preferred_element_type=F32)
            blast = b[C - 1:C, :]
            qb = qf * jnp.exp(b)
            kb = kf * jnp.exp(blast - b)
            o = lax.dot_general(qb.astype(BF16), st.astype(BF16), NT_DIMS, preferred_element_type=F32)
            amat = None
            blk = C // 2
            while blk >= SUBLANES:
                q_parts, k_parts = [], []
                for m in range(C // blk):
                    r = slice(m * blk, (m + 1) * blk)
                    if m % 2 == 1:
                        ref = b[m * blk - 1:m * blk, :]
                        q_parts.append(qf[r] * jnp.exp(b[r] - ref))
                        k_parts.append(jnp.zeros((blk, K), F32))
                    else:
                        ref = b[(m + 1) * blk - 1:(m + 1) * blk, :]
                        q_parts.append(jnp.zeros((blk, K), F32))
                        k_parts.append(kf[r] * jnp.exp(ref - b[r]))
                ql = jnp.concatenate(q_parts, axis=0).astype(BF16)
                kl = jnp.concatenate(k_parts, axis=0).astype(BF16)
                al = lax.dot_general(ql, kl, NT_DIMS, preferred_element_type=F32)
                al = jnp.where((rows // (2 * blk)) == (cols // (2 * blk)), al, 0.0)
                amat = al if amat is None else amat + al
                blk //= 2
            if amat is not None:
                o = o + _dot(amat.astype(BF16), vf.astype(BF16))
            o = o + _gla_band(qf, kf, vf, fg, min(SUBLANES, C), lane_sum)
            st = st * jnp.exp(blast) + lax.dot_general(
                vf.astype(BF16), kb.astype(BF16), TN_DIMS, preferred_element_type=F32)
            on = _rms(o, hn_ref[...]) * g_ref[sl, hl].astype(F32)
            o_ref[sl, hl] = on.astype(BF16)
            return st

        sts = [st_sc[hh] for hh in range(HPS)]
        for c in range(ncc):
            sts = [chunk(c, hh, sts[hh]) for hh in range(HPS)]
        for hh in range(HPS):
            st_sc[hh] = sts[hh]

        @pl.when(ci == nct - 1)
        def _():
            for hh in range(HPS):
                s_ref[0, hh] = sts[hh].T

    tok = lambda b, h, ci: (b * nct + ci, h)
    return pl.pallas_call(
        kern, grid=(B, H // HPS, nct),
        in_specs=[pl.BlockSpec((ct, HPS * K), tok)] * 5 + [pl.BlockSpec(hg_norm.shape, lambda b, h, ci: (0, 0))],
        out_specs=(pl.BlockSpec((ct, HPS * K), tok),
                   pl.BlockSpec((1, HPS, K, K), lambda b, h, ci: (b, h, 0, 0))),
        out_shape=(jax.ShapeDtypeStruct((T, HK), BF16), jax.ShapeDtypeStruct((B, H, K, K), F32)),
        scratch_shapes=[pltpu.VMEM((HPS, K, K), F32)],
        compiler_params=_cparams(32), name="gla_prompt")(qs, kk, logf, vv, gs, hg_norm)


def _gla_sample(qs, kk, logf, vv, gs, hg_norm, s0, DB, Ld, H):
    HK = qs.shape[2]
    K = HK // H
    C = SUBLANES
    assert Ld <= C

    def kern(q_ref, k_ref, lf_ref, v_ref, g_ref, hn_ref, s0_ref, o_ref, s_ref, pad_sc):
        rows = lax.broadcasted_iota(jnp.int32, (C, C), 0)
        cols = lax.broadcasted_iota(jnp.int32, (C, C), 1)
        tril = (rows >= cols).astype(F32)

        def padded(ref, slot):
            pad_sc[slot] = jnp.zeros((C, HK), F32)
            pad_sc[slot, 0:Ld, :] = ref[0].astype(F32)
            return pad_sc[slot]

        qf, kf, lf, vf = padded(q_ref, 0), padded(k_ref, 1), padded(lf_ref, 2), padded(v_ref, 3)
        fg = jnp.exp(lf)
        b = jnp.dot(tril, lf, precision=lax.Precision.HIGHEST, preferred_element_type=F32)
        blast = b[C - 1:C, :]
        qb = (qf * jnp.exp(b)).astype(BF16)
        kb = (kf * jnp.exp(blast - b)).astype(BF16)
        dec = jnp.exp(blast)
        vb = vf.astype(BF16)

        def lane_sum(parts):
            return parts

        for h in range(H):
            hl = slice(h * K, (h + 1) * K)

            def head_sum(parts):
                return [jnp.sum(p, axis=-1, keepdims=True) for p in parts]

            st = s0_ref[0, h].T
            o = lax.dot_general(qb[:, hl], st.astype(BF16), NT_DIMS, preferred_element_type=F32)
            o = o + _gla_band(qf[:, hl], kf[:, hl], vf[:, hl], fg[:, hl], Ld, head_sum)
            st = st * dec[:, hl] + lax.dot_general(vb[:, hl], kb[:, hl], TN_DIMS,
                                                   preferred_element_type=F32)
            s_ref[0, h] = st.T
            on = _rms(o[0:Ld], hn_ref[...]) * g_ref[0, :, hl].astype(F32)
            o_ref[0, :, hl] = on.astype(BF16)

    tok = pl.BlockSpec((1, Ld, HK), lambda b: (b, 0, 0))
    st_spec = pl.BlockSpec((1, H, K, K), lambda b: (b, 0, 0, 0))
    return pl.pallas_call(
        kern, grid=(DB,),
        in_specs=[tok] * 5 + [pl.BlockSpec(hg_norm.shape, lambda b: (0, 0)), st_spec],
        out_specs=(tok, st_spec),
        out_shape=(jax.ShapeDtypeStruct((DB, Ld, HK), BF16), jax.ShapeDtypeStruct(s0.shape, F32)),
        scratch_shapes=[pltpu.VMEM((4, C, HK), F32)],
        compiler_params=_cparams(32), name="gla_sample")(qs, kk, logf, vv, gs, hg_norm, s0)


def _router(h, gain, w_router, b_router):
    T, D = h.shape
    E = w_router.shape[1]
    tm = _tile(T, LINEAR_TM)
    assert TOP_K == 2

    def kern(x_ref, g_ref, w_ref, b_ref, i_ref, o_ref):
        xn = _rms(x_ref[...], g_ref[...])
        logits = jnp.dot(xn, w_ref[...], precision=lax.Precision.HIGHEST,
                         preferred_element_type=F32) + b_ref[...]
        idx = lax.broadcasted_iota(jnp.int32, logits.shape, 1)
        m1 = jnp.max(logits, axis=-1, keepdims=True)
        i1 = jnp.min(jnp.where(logits == m1, idx, E), axis=-1, keepdims=True)
        rest = jnp.where(idx == i1, -jnp.inf, logits)
        m2 = jnp.max(rest, axis=-1, keepdims=True)
        i2 = jnp.min(jnp.where(rest == m2, idx, E), axis=-1, keepdims=True)
        e2 = jnp.exp(m2 - m1)
        g1 = 1.0 / (1.0 + e2)
        slot = lax.broadcasted_iota(jnp.int32, (tm, TOP_K), 1)
        i_ref[...] = jnp.where(slot == 0, i1, i2)
        o_ref[...] = jnp.where(slot == 0, g1, e2 * g1)

    return pl.pallas_call(
        kern, grid=(T // tm,),
        in_specs=[pl.BlockSpec((tm, D), lambda i: (i, 0)),
                  pl.BlockSpec(gain.shape, lambda i: (0, 0)),
                  pl.BlockSpec(w_router.shape, lambda i: (0, 0)),
                  pl.BlockSpec(b_router.shape, lambda i: (0, 0))],
        out_specs=(pl.BlockSpec((tm, TOP_K), lambda i: (i, 0)), pl.BlockSpec((tm, TOP_K), lambda i: (i, 0))),
        out_shape=(jax.ShapeDtypeStruct((T, TOP_K), jnp.int32), jax.ShapeDtypeStruct((T, TOP_K), F32)),
        compiler_params=_cparams(32), name="router")(h, gain, w_router, b_router)


def _route_plan(idx, E, tmx):
    T = idx.shape[0]
    NP = TOP_K * T + E * tmx
    assert NP % tmx == 0
    NT = NP // tmx
    sel = jnp.any(idx[:, :, None] == jnp.arange(E, dtype=jnp.int32), axis=1).astype(jnp.int32)
    incl = jnp.cumsum(sel, axis=0)
    counts = incl[-1]
    padded = ((counts + tmx - 1) // tmx) * tmx
    ends = jnp.cumsum(padded)
    offs = ends - padded
    rank = jnp.take_along_axis(incl - sel, idx, axis=1)
    pos = (offs[idx] + rank).reshape(-1).astype(jnp.int32)
    tok = jnp.repeat(jnp.arange(T, dtype=jnp.int32), TOP_K)
    src = jnp.zeros((NP,), jnp.int32).at[pos].set(tok)
    n_used = (ends[-1] // tmx).astype(jnp.int32)
    tile_start = jnp.minimum(jnp.arange(NT, dtype=jnp.int32), n_used - 1) * tmx
    tile_expert = jnp.minimum(jnp.sum(tile_start[:, None] >= ends[None, :], axis=1), E - 1).astype(jnp.int32)
    return pos, src, tile_expert, n_used.reshape(1)


def _moe_experts(h, gain, src, tile_expert, n_used, w_gu, w_down, tmx):
    T, D = h.shape
    E, F = w_down.shape[0], w_down.shape[1]
    NP = src.shape[0]
    NT = NP // tmx
    tf = _tile(F, MOE_TF, LANES)
    nf = F // tf

    def kern(src_ref, te_ref, nu_ref, h_ref, g_ref, wg_ref, wu_ref, wd_ref, o_ref, xbuf, xn_sc, sem):
        i, f = pl.program_id(0), pl.program_id(1)
        n_used_v = nu_ref[0]

        def row_copy(tile, r):
            return pltpu.make_async_copy(h_ref.at[pl.ds(src_ref[tile * tmx + r], 1)],
                                         xbuf.at[pl.ds(r, 1)], sem)

        def start_gather(tile):
            lax.fori_loop(0, tmx, lambda r, c: (row_copy(tile, r).start(), c)[1], 0, unroll=DMA_LOOP_UNROLL)

        @pl.when((f == 0) & (i == 0))
        def _():
            start_gather(0)

        @pl.when((f == 0) & (i < n_used_v))
        def _():
            lax.fori_loop(0, tmx, lambda r, c: (row_copy(i, r).wait(), c)[1], 0, unroll=DMA_LOOP_UNROLL)
            xn_sc[...] = _rms(xbuf[...], g_ref[...]).astype(BF16)

            @pl.when(i + 1 < n_used_v)
            def _():
                start_gather(i + 1)

        @pl.when(f == 0)
        def _():
            o_ref[...] = jnp.zeros((tmx, D), F32)

        @pl.when(i < n_used_v)
        def _():
            xn = xn_sc[...]
            act = (_silu(_dot(xn, wg_ref[0])) * _dot(xn, wu_ref[0])).astype(BF16)
            o_ref[...] += _dot(act, wd_ref[0])

    def wspec(shape, fn):
        def index_map(i, f, src_r, te_r, nu_r):
            return fn(te_r[i], jnp.where(i < nu_r[0], f, nf - 1))
        return pl.BlockSpec(shape, index_map)

    grid_spec = pltpu.PrefetchScalarGridSpec(
        num_scalar_prefetch=3, grid=(NT, nf),
        in_specs=[pl.BlockSpec(memory_space=pl.ANY),
                  pl.BlockSpec(gain.shape, lambda i, f, *_: (0, 0)),
                  wspec((1, D, tf), lambda e, f: (e, 0, f)),
                  wspec((1, D, tf), lambda e, f: (e, 0, nf + f)),
                  wspec((1, tf, D), lambda e, f: (e, f, 0))],
        out_specs=pl.BlockSpec((tmx, D), lambda i, f, *_: (i, 0)),
        scratch_shapes=[pltpu.VMEM((tmx, D), F32), pltpu.VMEM((tmx, D), BF16),
                        pltpu.SemaphoreType.DMA(())])
    return pl.pallas_call(
        kern, grid_spec=grid_spec, out_shape=jax.ShapeDtypeStruct((NP, D), F32),
        compiler_params=_cparams(48), name="moe_experts")(
            src, tile_expert, n_used, h, gain, w_gu, w_gu, w_down)


def _moe_combine(h, y_sorted, pos, gate, final_gain):
    T, D = h.shape
    tm = _tile(T, MOE_COMBINE_TM)
    nt = T // tm

    def kern(pos_ref, h_ref, y_ref, gt_ref, fg_ref, o_ref, ybuf, sems):
        i = pl.program_id(0)

        def row_copy(step, r, s):
            slot = step % 2
            return pltpu.make_async_copy(y_ref.at[pl.ds(pos_ref[(step * tm + r) * TOP_K + s], 1)],
                                         ybuf.at[slot, s, pl.ds(r, 1)], sems.at[slot])

        def for_rows(step, fn):
            def body(r, c):
                for s in range(TOP_K):
                    fn(row_copy(step, r, s))
                return c
            lax.fori_loop(0, tm, body, 0, unroll=DMA_LOOP_UNROLL)

        @pl.when(i == 0)
        def _():
            for_rows(0, lambda cp: cp.start())

        @pl.when(i + 1 < nt)
        def _():
            for_rows(i + 1, lambda cp: cp.start())

        for_rows(i, lambda cp: cp.wait())
        slot = i % 2
        gt = gt_ref[...]
        y = h_ref[...]
        for s in range(TOP_K):
            y = y + gt[:, s:s + 1] * ybuf[slot, s]
        o_ref[...] = _rms(y, fg_ref[...])

    grid_spec = pltpu.PrefetchScalarGridSpec(
        num_scalar_prefetch=1, grid=(nt,),
        in_specs=[pl.BlockSpec((tm, D), lambda i, *_: (i, 0)),
                  pl.BlockSpec(memory_space=pl.ANY),
                  pl.BlockSpec((tm, TOP_K), lambda i, *_: (i, 0)),
                  pl.BlockSpec(final_gain.shape, lambda i, *_: (0, 0))],
        out_specs=pl.BlockSpec((tm, D), lambda i, *_: (i, 0)),
        scratch_shapes=[pltpu.VMEM((2, TOP_K, tm, D), F32), pltpu.SemaphoreType.DMA((2,))])
    return pl.pallas_call(
        kern, grid_spec=grid_spec, out_shape=jax.ShapeDtypeStruct((T, D), F32),
        compiler_params=_cparams(40), name="moe_combine")(pos, h, y_sorted, gate, final_gain)


def _moe(h, gain, w_router, b_router, w_gu, w_down, final_gain):
    T = h.shape[0]
    E = w_down.shape[0]
    tmx = _tile(max(MOE_MIN_TILE, 2 * TOP_K * T // E), MOE_TM)
    idx, gate = _router(h, gain, w_router, b_router)
    pos, src, tile_expert, n_used = _route_plan(idx, E, tmx)
    y_sorted = _moe_experts(h, gain, src, tile_expert, n_used, w_gu, w_down, tmx)
    return _moe_combine(h, y_sorted, pos, gate, final_gain)


def _rope_rows(pos, rope_dim):
    inv = ROPE_THETA ** (-jnp.arange(0, rope_dim, 2, dtype=F32) / rope_dim)
    ang = pos.astype(F32)[:, None] * inv[None, :]
    z = jnp.zeros((pos.shape[0], LANES - rope_dim), F32)
    cos, sin = jnp.cos(ang), jnp.sin(ang)
    return jnp.concatenate([cos, cos, z], axis=-1), jnp.concatenate([sin, sin, z], axis=-1)


def _rot_cols(w):
    half = w.shape[-1] // 2
    return jnp.concatenate([-w[..., half:], w[..., :half]], axis=-1)


def _pad_lanes(w):
    pad = LANES - w.shape[-1]
    return jnp.pad(w, [(0, 0)] * (w.ndim - 1) + [(0, pad)])


def _prep_even(a, w_in_e, w_q_b, w_kv_b, pool_w, w_out_e, w_ffn_gu, w_ffn_down, dims, H):
    DP, QL, KL, ROPE = dims
    w_in = w_in_e[a]
    k_raw = w_in[:, DP + QL + KL:]
    w_ext = jnp.concatenate([w_in[:, :DP + QL + KL], _pad_lanes(k_raw), _pad_lanes(_rot_cols(k_raw))],
                            axis=-1).astype(BF16)
    wq = w_q_b[a].reshape(QL, H, NOPE_DIM + ROPE)
    wq_rope = wq[..., NOPE_DIM:]
    wq_ext = jnp.concatenate([wq[..., :NOPE_DIM].reshape(QL, H * NOPE_DIM),
                              _pad_lanes(wq_rope).reshape(QL, H * LANES),
                              _pad_lanes(_rot_cols(wq_rope)).reshape(QL, H * LANES)], axis=-1).astype(BF16)
    w_kv = w_kv_b[a].reshape(KL, H, NOPE_DIM + V_DIM)
    wuk_t = jnp.transpose(w_kv[..., :NOPE_DIM], (1, 2, 0)).astype(BF16)
    wuv = jnp.transpose(w_kv[..., NOPE_DIM:], (1, 0, 2)).astype(BF16)
    w_out = w_out_e[a].astype(BF16)
    return dict(w_ext=w_ext, wq_ext=wq_ext, wuk_t=wuk_t, wuv=wuv, pool_w=pool_w[a].astype(BF16),
                w_out_pool=w_out[:DP], w_out_attn=w_out[DP:],
                w_gu=w_ffn_gu[a].astype(BF16), w_down=w_ffn_down[a].astype(BF16))


def kernel(x_prompt, x_sample, cache_ckv, cache_krope, page_table, state_pool, state_hgrn,
           norm_mix_e, w_in_e, q_norm, w_q_b, kv_norm, w_kv_b, pool_w, pool_scale, w_out_e,
           norm_ffn_e, w_ffn_gu, w_ffn_down,
           norm_mix_o, w_in_o, hg_lower_bound, hg_norm, w_out_o, norm_ffn_o, w_router, b_router,
           w_exp_gu, w_exp_down, final_norm):
    B, L, D = x_prompt.shape
    DB, Ld, _ = x_sample.shape
    n_pages = page_table.shape[1]
    PG = cache_ckv.shape[2]
    past_len = n_pages * PG
    KL, ROPE = cache_ckv.shape[3], cache_krope.shape[3]
    DP, PS = state_pool.shape[3], state_pool.shape[2]
    QL = q_norm.shape[1]
    H = w_q_b.shape[2] // (NOPE_DIM + ROPE)
    HG = state_hgrn.shape[2]
    depth = hg_lower_bound.shape[0]
    dims = (DP, QL, KL, ROPE)
    scale = float((NOPE_DIM + ROPE) ** -0.5 * math.log2(math.e))
    cache_krope_t = jnp.swapaxes(cache_krope, 2, 3)
    QW = KL + LANES
    KN = 16
    assert Ld <= KN and ROPE <= LANES

    row = lambda v: v.reshape(1, -1).astype(F32)
    lb_p = jax.nn.softmax(hg_lower_bound.astype(F32), axis=0)
    lower_bounds = jnp.cumsum(lb_p, axis=0) - lb_p[0]

    hp = x_prompt.reshape(B * L, D)
    hs = x_sample.reshape(DB * Ld, D)
    cos_p, sin_p = _rope_rows(jnp.tile(jnp.arange(L), B), ROPE)
    cos_s, sin_s = _rope_rows(jnp.tile(past_len + jnp.arange(Ld), DB), ROPE)

    outs_p = dict(ckv=[], krope=[], pool=[], hgrn=[])
    outs_s = dict(ckv=[], krope=[], pool=[], hgrn=[])
    for l in range(depth):
        a = l // 2
        if l % 2 == 0:
            w = _prep_even(a, w_in_e, w_q_b, w_kv_b, pool_w, w_out_e, w_ffn_gu, w_ffn_down, dims, H)
            g_mix, g_q, g_kv = row(norm_mix_e[a]), row(q_norm[a]), row(kv_norm[a])
            g_ffn, p_scale = row(norm_ffn_e[a]), row(pool_scale[a])

            u, qn, ckv, kpe, kcat, ckv_t = _even_in_proj(hp, g_mix, w["w_ext"], g_q, g_kv, cos_p, sin_p,
                                                         dims, True)
            pool_out, pool_new = _pool_prompt(u, jnp.zeros((B, PS, DP), F32), w["pool_w"], p_scale, B, L)
            qcat = _q_proj(qn, w["wq_ext"], w["wuk_t"], cos_p, sin_p, H, KL, scale, head_major=True)
            attn = _flash_prompt(qcat, kcat, ckv_t, w["wuv"], B, L, H, KL)
            hp = _linear_residual([pool_out, attn], [w["w_out_pool"], w["w_out_attn"]], hp, "out_proj_e")
            hp = _ffn(hp, g_ffn, w["w_gu"], w["w_down"])
            outs_p["ckv"].append(ckv.reshape(B, L, KL))
            outs_p["krope"].append(kpe.reshape(B, L, ROPE))
            outs_p["pool"].append(pool_new)

            u, qn, ckv, kpe, kcat = _even_in_proj(hs, g_mix, w["w_ext"], g_q, g_kv, cos_s, sin_s, dims, False)
            u3 = u.reshape(DB, Ld, DP)
            ext_tm = jnp.transpose(jnp.concatenate([state_pool[a], u3], axis=1), (1, 0, 2))
            pool_tm = _pool_sample(ext_tm, w["pool_w"], p_scale, Ld, past_len)
            pool_out = jnp.transpose(pool_tm, (1, 0, 2)).reshape(DB * Ld, DP)
            qrows = _q_proj(qn, w["wq_ext"], w["wuk_t"], cos_s, sin_s, H, KL, scale, head_major=False)
            q_s = qrows.reshape(DB, Ld * H, QW)
            knew = jnp.pad(kcat.reshape(DB, Ld, QW), ((0, 0), (0, KN - Ld), (0, 0)))
            ctx = _decode_attention(page_table, q_s, knew, cache_ckv, cache_krope_t, a, H, Ld, KL, ROPE)
            attn = _ctx_to_attn_out(ctx.reshape(DB * Ld, H * KL), w["wuv"], H, KL)
            hs = _linear_residual([pool_out, attn], [w["w_out_pool"], w["w_out_attn"]], hs, "out_proj_e")
            hs = _ffn(hs, g_ffn, w["w_gu"], w["w_down"])
            outs_s["ckv"].append(ckv.reshape(DB, Ld, KL))
            outs_s["krope"].append(kpe.reshape(DB, Ld, ROPE))
            outs_s["pool"].append(jnp.concatenate([state_pool[a], u3], axis=1)[:, -PS:])
        else:
            w_in = w_in_o[a].astype(BF16)
            w_out = w_out_o[a].astype(BF16)
            w_gu = w_exp_gu[a].astype(BF16)
            w_dn = w_exp_down[a].astype(BF16)
            g_mix, g_ffn, g_hn = row(norm_mix_o[a]), row(norm_ffn_o[a]), row(hg_norm[a])
            lb = row(lower_bounds[l])
            w_r, b_r = w_router[a].astype(F32), row(b_router[a])
            g_fin = row(final_norm) if l == depth - 1 else None
            assert g_fin is not None

            qs, kk, logf, vv, gs = _hgrn_in_proj(hp, g_mix, w_in, lb)
            on, s_new = _gla_prompt(qs, kk, logf, vv, gs, g_hn, B, L, HG)
            hp = _linear_residual([on], [w_out], hp, "out_proj_o")
            hp = _moe(hp, g_ffn, w_r, b_r, w_gu, w_dn, g_fin)
            outs_p["hgrn"].append(s_new)

            qs, kk, logf, vv, gs = _hgrn_in_proj(hs, g_mix, w_in, lb)
            r3 = lambda t: t.reshape(DB, Ld, -1)
            on, s_new = _gla_sample(r3(qs), r3(kk), r3(logf), r3(vv), r3(gs), g_hn, state_hgrn[a], DB, Ld, HG)
            hs = _linear_residual([on.reshape(DB * Ld, -1)], [w_out], hs, "out_proj_o")
            hs = _moe(hs, g_ffn, w_r, b_r, w_gu, w_dn, g_fin)
            outs_s["hgrn"].append(s_new)

    return (hp.reshape(B, L, D), hs.reshape(DB, Ld, D),
            jnp.stack(outs_p["ckv"]), jnp.stack(outs_p["krope"]), jnp.stack(outs_p["pool"]),
            jnp.stack(outs_p["hgrn"]),
            jnp.stack(outs_s["ckv"]), jnp.stack(outs_s["krope"]), jnp.stack(outs_s["pool"]),
            jnp.stack(outs_s["hgrn"]))
```

```python
import functools
import math

import jax
import jax.numpy as jnp
from jax import lax
from jax.experimental import pallas as pl
from jax.experimental.pallas import tpu as pltpu

F32 = jnp.float32
BF16 = jnp.bfloat16

EPS = 1e-6
POOL_WINDOWS = (2, 4, 8, 16)
NOPE_DIM = 128
V_DIM = 128
ROPE_THETA = 10000.0
TOP_K = 2
LANES = 128
SUBLANES = 8
NEG_BIG = -1e30

LINEAR_TM = 512
FFN_TM = 512
FFN_TF = 512
MOE_TM = 512
MOE_MIN_TILE = 16
MOE_TF = 256
MOE_COMBINE_TM = 256
DMA_LOOP_UNROLL = 8
FLASH_TQ = 128
FLASH_TK = 512
POOL_TL = 512
GLA_CT = 256
GLA_CHUNK = 256
GLA_HEADS_PER_STEP = 2
DECODE_PAGES_PER_STEP = 16
DECODE_SEQS_PER_STEP = 2

NT_DIMS = (((1,), (1,)), ((), ()))
TN_DIMS = (((0,), (0,)), ((), ()))


def _cparams(vmem_mb):
    return pltpu.CompilerParams(vmem_limit_bytes=vmem_mb * 2 ** 20)


def _tile(n, pref, mult=SUBLANES):
    if n <= pref:
        return n
    for t in range(pref, 0, -1):
        if n % t == 0 and t % mult == 0:
            return t
    return n


def _rms(x, g):
    return x * lax.rsqrt(jnp.mean(x * x, axis=-1, keepdims=True) + EPS) * g


def _silu(x):
    return x / (1.0 + jnp.exp(-x))


def _dot(a, b):
    return jnp.dot(a, b, preferred_element_type=F32)


def _fused_linear(xs, ws, *, tm, epilogue, out_shapes, out_specs, gain=None, wspecs=None,
                  row_extras=(), const_extras=(), vmem_mb=48, name="linear"):
    T = xs[0].shape[0]
    grid = (T // tm,)
    n_x, n_w, n_r, n_c = len(xs), len(ws), len(row_extras), len(const_extras)

    def kern(*refs):
        pos = 0
        x_refs = refs[pos:pos + n_x]; pos += n_x
        gain_ref = None
        if gain is not None:
            gain_ref = refs[pos]; pos += 1
        w_refs = refs[pos:pos + n_w]; pos += n_w
        r_refs = refs[pos:pos + n_r]; pos += n_r
        c_refs = refs[pos:pos + n_c]; pos += n_c
        o_refs = refs[pos:]
        acc = None
        x0 = None
        for xr, wr in zip(x_refs, w_refs):
            xv = xr[...]
            if gain_ref is not None:
                x0 = xv
                xv = _rms(xv, gain_ref[...]).astype(BF16)
            d = _dot(xv, wr[...])
            acc = d if acc is None else acc + d
        epilogue(acc, x0, r_refs, c_refs, o_refs)

    in_specs = [pl.BlockSpec((tm, x.shape[1]), lambda i: (i, 0)) for x in xs]
    args = list(xs)
    if gain is not None:
        in_specs.append(pl.BlockSpec(gain.shape, lambda i: (0, 0)))
        args.append(gain)
    if wspecs is None:
        wspecs = [pl.BlockSpec(w.shape, lambda i: (0,) * w.ndim) for w in ws]
    in_specs += list(wspecs)
    args += list(ws)
    for r in row_extras:
        in_specs.append(pl.BlockSpec((tm, r.shape[1]), lambda i: (i, 0)))
        args.append(r)
    for c in const_extras:
        in_specs.append(pl.BlockSpec(c.shape, lambda i, nd=c.ndim: (0,) * nd))
        args.append(c)
    return pl.pallas_call(
        kern, grid=grid, in_specs=in_specs, out_specs=out_specs, out_shape=out_shapes,
        compiler_params=_cparams(vmem_mb), name=name)(*args)


def _even_in_proj(h, gain, w_ext, qn_g, kvn_g, cos_rows, sin_rows, dims, with_ckv_t):
    DP, QL, KL, ROPE = dims
    T, D = h.shape
    tm = _tile(T, LINEAR_TM)
    o_kr = DP + QL + KL

    def epilogue(acc, x0, r_refs, c_refs, o_refs):
        u_ref, qn_ref, ckv_ref, kpe_ref, kcat_ref = o_refs[:5]
        cos, sin = r_refs[0][...], r_refs[1][...]
        u_ref[...] = acc[:, :DP]
        qn_ref[...] = _rms(acc[:, DP:DP + QL], c_refs[0][...]).astype(BF16)
        ckv = _rms(acc[:, DP + QL:o_kr], c_refs[1][...])
        ckv_ref[...] = ckv
        kpe = acc[:, o_kr:o_kr + LANES] * cos + acc[:, o_kr + LANES:o_kr + 2 * LANES] * sin
        kpe_ref[...] = kpe[:, :ROPE]
        kcat_ref[:, :KL] = ckv.astype(BF16)
        kcat_ref[:, KL:] = kpe.astype(BF16)
        if with_ckv_t:
            o_refs[5][...] = ckv.T.astype(BF16)

    out_shapes = (jax.ShapeDtypeStruct((T, DP), F32), jax.ShapeDtypeStruct((T, QL), BF16),
                  jax.ShapeDtypeStruct((T, KL), F32), jax.ShapeDtypeStruct((T, ROPE), F32),
                  jax.ShapeDtypeStruct((T, KL + LANES), BF16))
    out_specs = tuple(pl.BlockSpec((tm, s.shape[1]), lambda i: (i, 0)) for s in out_shapes)
    if with_ckv_t:
        out_shapes += (jax.ShapeDtypeStruct((KL, T), BF16),)
        out_specs += (pl.BlockSpec((KL, tm), lambda i: (0, i)),)
    return _fused_linear([h], [w_ext], tm=tm, gain=gain, epilogue=epilogue,
                         out_shapes=out_shapes, out_specs=out_specs,
                         row_extras=(cos_rows, sin_rows), const_extras=(qn_g, kvn_g),
                         name="even_in_proj")


def _q_proj(qn, wq_ext, wuk_t, cos_rows, sin_rows, H, KL, scale, head_major):
    T = qn.shape[0]
    tm = _tile(T, LINEAR_TM)
    QW = KL + LANES
    r0, r1 = H * NOPE_DIM, 2 * H * NOPE_DIM

    def epilogue(acc, x0, r_refs, c_refs, o_refs):
        (o_ref,) = o_refs
        cos, sin = r_refs[0][...], r_refs[1][...]
        wuk_ref = c_refs[0]
        for h in range(H):
            qn_h = acc[:, h * NOPE_DIM:(h + 1) * NOPE_DIM].astype(BF16)
            q_abs = (_dot(qn_h, wuk_ref[h]) * scale).astype(BF16)
            q_pe = ((acc[:, r0 + h * LANES:r0 + (h + 1) * LANES] * cos
                     + acc[:, r1 + h * LANES:r1 + (h + 1) * LANES] * sin) * scale).astype(BF16)
            if head_major:
                o_ref[h, :, :KL] = q_abs
                o_ref[h, :, KL:] = q_pe
            else:
                o_ref[:, h * QW:h * QW + KL] = q_abs
                o_ref[:, h * QW + KL:(h + 1) * QW] = q_pe

    if head_major:
        out_shape = jax.ShapeDtypeStruct((H, T, QW), BF16)
        out_spec = pl.BlockSpec((H, tm, QW), lambda i: (0, i, 0))
    else:
        out_shape = jax.ShapeDtypeStruct((T, H * QW), BF16)
        out_spec = pl.BlockSpec((tm, H * QW), lambda i: (i, 0))
    (out,) = _fused_linear([qn], [wq_ext], tm=tm, epilogue=epilogue, out_shapes=(out_shape,),
                           out_specs=(out_spec,), row_extras=(cos_rows, sin_rows),
                           const_extras=(wuk_t,), name="q_proj")
    return out


def _pool_group_out(ext_ref, base, tl, pos0, g, w, PG, pw_ref, scale_ref):
    lo, hi = g * PG, (g + 1) * PG
    x = ext_ref[base:base + tl, lo:hi]
    win = x
    for j in range(1, w):
        win = win + ext_ref[base - j:base - j + tl, lo:hi]
    if pos0 is None:
        d = win * (1.0 / w) - x
    else:
        pos = pos0 + lax.broadcasted_iota(jnp.int32, (tl, 1), 0)
        cnt = jnp.minimum(pos + 1, w).astype(F32)
        d = win / cnt - x
    return _dot(d.astype(BF16), pw_ref[g]) * scale_ref[:, lo:hi]


def _pool_prompt(u, past, pool_w, pool_scale, B, L):
    DP = u.shape[1]
    PS = past.shape[1]
    HALO = 16
    G = len(POOL_WINDOWS)
    PG = DP // G
    tl = _tile(L, POOL_TL, HALO)
    nl = L // tl

    def kern(u_ref, halo_ref, past_ref, pw_ref, sc_ref, o_ref, new_ref, ext_ref):
        i = pl.program_id(1)

        @pl.when(i == 0)
        def _():
            ext_ref[0:1, :] = jnp.zeros((1, DP), F32)
            ext_ref[HALO - PS:HALO, :] = past_ref[0]

        @pl.when(i > 0)
        def _():
            ext_ref[0:HALO, :] = halo_ref[...]

        ext_ref[HALO:HALO + tl, :] = u_ref[...]
        for g, w in enumerate(POOL_WINDOWS):
            o_ref[:, g * PG:(g + 1) * PG] = _pool_group_out(
                ext_ref, HALO, tl, i * tl, g, w, PG, pw_ref, sc_ref).astype(BF16)

        @pl.when(i == nl - 1)
        def _():
            new_ref[0] = ext_ref[HALO + tl - PS:HALO + tl, :]

    r = tl // HALO
    return pl.pallas_call(
        kern, grid=(B, nl),
        in_specs=[
            pl.BlockSpec((tl, DP), lambda b, i: (b * nl + i, 0)),
            pl.BlockSpec((HALO, DP), lambda b, i: (jnp.maximum((b * nl + i) * r - 1, 0), 0)),
            pl.BlockSpec((1, PS, DP), lambda b, i: (b, 0, 0)),
            pl.BlockSpec(pool_w.shape, lambda b, i: (0, 0, 0)),
            pl.BlockSpec(pool_scale.shape, lambda b, i: (0, 0)),
        ],
        out_specs=(pl.BlockSpec((tl, DP), lambda b, i: (b * nl + i, 0)),
                   pl.BlockSpec((1, PS, DP), lambda b, i: (b, 0, 0))),
        out_shape=(jax.ShapeDtypeStruct((B * L, DP), BF16), jax.ShapeDtypeStruct((B, PS, DP), F32)),
        scratch_shapes=[pltpu.VMEM((HALO + tl, DP), F32)],
        compiler_params=_cparams(40), name="pool_prompt")(u, u, past, pool_w, pool_scale)


def _pool_sample(ext_tm, pool_w, pool_scale, Ld, start):
    R, DB, DP = ext_tm.shape
    PS = R - Ld
    G = len(POOL_WINDOWS)
    PG = DP // G

    def kern(e_ref, pw_ref, sc_ref, o_ref):
        for t in range(Ld):
            for g, w in enumerate(POOL_WINDOWS):
                lo, hi = g * PG, (g + 1) * PG
                x = e_ref[PS + t, :, lo:hi]
                win = x
                for j in range(1, w):
                    win = win + e_ref[PS + t - j, :, lo:hi]
                cnt = float(min(start + t + 1, w))
                d = win / cnt - x
                o_ref[t, :, lo:hi] = (_dot(d.astype(BF16), pw_ref[g]) * sc_ref[:, lo:hi]).astype(BF16)

    return pl.pallas_call(
        kern, grid=(1,),
        in_specs=[pl.BlockSpec(ext_tm.shape, lambda i: (0, 0, 0)),
                  pl.BlockSpec(pool_w.shape, lambda i: (0, 0, 0)),
                  pl.BlockSpec(pool_scale.shape, lambda i: (0, 0))],
        out_specs=pl.BlockSpec((Ld, DB, DP), lambda i: (0, 0, 0)),
        out_shape=jax.ShapeDtypeStruct((Ld, DB, DP), BF16),
        compiler_params=_cparams(48), name="pool_sample")(ext_tm, pool_w, pool_scale)


def _flash_prompt(qcat, kcat, ckv_t, wuv, B, L, H, KL):
    QW = qcat.shape[2]
    tq = _tile(L, FLASH_TQ, LANES)
    tk = _tile(L, FLASH_TK, LANES)
    assert tk % tq == 0 and tq % LANES == 0
    nq, nk = L // tq, L // tk
    R = H * tq

    def last_needed(qi):
        return (qi * tq + tq - 1) // tk

    pairs = [(qi, ki) for qi in range(nq) for ki in range(last_needed(qi) + 1)]
    qi_tab = jnp.asarray([p[0] for p in pairs], jnp.int32)
    ki_tab = jnp.asarray([p[1] for p in pairs], jnp.int32)

    def kern(qi_ref, ki_ref, q_ref, k_ref, kt_ref, wuv_ref, o_ref, m_sc, l_sc, acc_sc):
        qi, ki = qi_ref[pl.program_id(1)], ki_ref[pl.program_id(1)]

        @pl.when(ki == 0)
        def _():
            m_sc[...] = jnp.full((1, R), NEG_BIG, F32)
            l_sc[...] = jnp.zeros((1, R), F32)
            acc_sc[...] = jnp.zeros((KL, R), F32)

        def update(masked):
            q = q_ref[...].reshape(R, QW)
            s = lax.dot_general(k_ref[...], q, NT_DIMS, preferred_element_type=F32)
            if masked:
                kpos = ki * tk + lax.broadcasted_iota(jnp.int32, (tk, R), 0)
                qpos = qi * tq + (lax.broadcasted_iota(jnp.int32, (tk, R), 1) % tq)
                s = jnp.where(kpos <= qpos, s, NEG_BIG)
            m_prev = m_sc[...]
            m_new = jnp.maximum(m_prev, jnp.max(s, axis=0, keepdims=True))
            alpha = jnp.exp2(m_prev - m_new)
            p = jnp.exp2(s - m_new)
            l_sc[...] = alpha * l_sc[...] + jnp.sum(p, axis=0, keepdims=True)
            acc_sc[...] = alpha * acc_sc[...] + _dot(kt_ref[...], p.astype(BF16))
            m_sc[...] = m_new

        pl.when(ki < last_needed(qi))(functools.partial(update, False))
        pl.when(ki == last_needed(qi))(functools.partial(update, True))

        @pl.when(ki == last_needed(qi))
        def _():
            ctx_t = (acc_sc[...] / l_sc[...]).astype(BF16)
            for h in range(H):
                o_ref[:, h * V_DIM:(h + 1) * V_DIM] = lax.dot_general(
                    ctx_t[:, h * tq:(h + 1) * tq], wuv_ref[h], TN_DIMS,
                    preferred_element_type=F32).astype(BF16)

    grid_spec = pltpu.PrefetchScalarGridSpec(
        num_scalar_prefetch=2, grid=(B, len(pairs)),
        in_specs=[
            pl.BlockSpec((H, tq, QW), lambda b, s, qt, kt: (0, b * nq + qt[s], 0)),
            pl.BlockSpec((tk, QW), lambda b, s, qt, kt: (b * nk + kt[s], 0)),
            pl.BlockSpec((KL, tk), lambda b, s, qt, kt: (0, b * nk + kt[s])),
            pl.BlockSpec(wuv.shape, lambda b, s, qt, kt: (0, 0, 0)),
        ],
        out_specs=pl.BlockSpec((tq, H * V_DIM), lambda b, s, qt, kt: (b * nq + qt[s], 0)),
        scratch_shapes=[pltpu.VMEM((1, R), F32), pltpu.VMEM((1, R), F32), pltpu.VMEM((KL, R), F32)])
    return pl.pallas_call(
        kern, grid_spec=grid_spec, out_shape=jax.ShapeDtypeStruct((B * L, H * V_DIM), BF16),
        compiler_params=_cparams(56), name="flash_prompt")(qi_tab, ki_tab, qcat, kcat, ckv_t, wuv)


def _decode_attention(page_table, q_s, knew, cache_ckv, cache_krope_t, a, H, Ld, KL, ROPE):
    DB, R, QW = q_s.shape
    KN = knew.shape[1]
    n_pages = page_table.shape[1]
    PG = cache_ckv.shape[2]
    PP = DECODE_PAGES_PER_STEP
    while n_pages % PP:
        PP //= 2
    nj = n_pages // PP
    SB = DECODE_SEQS_PER_STEP if DB % DECODE_SEQS_PER_STEP == 0 else 1
    NP = SB * PP
    n_steps = (DB // SB) * nj

    def kern(pt_ref, q_ref, kn_ref, ck_hbm, kr_hbm, o_ref, ckbuf, krbuf, sems, m_sc, l_sc, acc_sc):
        bi, j = pl.program_id(0), pl.program_id(1)
        n = bi * nj + j

        def page_copies(step, i):
            sb, p = i // PP, i % PP
            seq = (step // nj) * SB + sb
            page = pt_ref[seq * n_pages + (step % nj) * PP + p]
            slot = step % 2
            return (pltpu.make_async_copy(ck_hbm.at[a, page], ckbuf.at[slot, i], sems.at[slot]),
                    pltpu.make_async_copy(kr_hbm.at[a, page], krbuf.at[slot, i], sems.at[slot]))

        def for_pages(step, fn):
            def body(i, c):
                for cp in page_copies(step, i):
                    fn(cp)
                return c
            lax.fori_loop(0, NP, body, 0, unroll=DMA_LOOP_UNROLL)

        @pl.when(n == 0)
        def _():
            for_pages(0, lambda cp: cp.start())

        @pl.when(n + 1 < n_steps)
        def _():
            for_pages(n + 1, lambda cp: cp.start())

        @pl.when(j == 0)
        def _():
            m_sc[...] = jnp.full((SB, R, 1), NEG_BIG, F32)
            l_sc[...] = jnp.zeros((SB, R, 1), F32)
            acc_sc[...] = jnp.zeros((SB, R, KL), F32)

        for_pages(n, lambda cp: cp.wait())
        slot = n % 2

        def update(sb, s, v):
            m_prev = m_sc[sb]
            m_new = jnp.maximum(m_prev, jnp.max(s, axis=-1, keepdims=True))
            alpha = jnp.exp2(m_prev - m_new)
            p = jnp.exp2(s - m_new)
            l_sc[sb] = alpha * l_sc[sb] + jnp.sum(p, axis=-1, keepdims=True)
            acc_sc[sb] = alpha * acc_sc[sb] + _dot(p.astype(BF16), v)
            m_sc[sb] = m_new

        for sb in range(SB):
            q = q_ref[sb]
            ck = ckbuf[slot, sb * PP:(sb + 1) * PP].reshape(PP * PG, KL).astype(BF16)
            kr_t = jnp.concatenate([krbuf[slot, sb * PP + p].astype(BF16) for p in range(PP)], axis=1)
            s = (lax.dot_general(q[:, :KL], ck, NT_DIMS, preferred_element_type=F32)
                 + _dot(q[:, KL:KL + ROPE], kr_t))
            update(sb, s, ck)

        @pl.when(j == nj - 1)
        def _():
            for sb in range(SB):
                q, kn = q_ref[sb], kn_ref[sb]
                s = lax.dot_general(q, kn, NT_DIMS, preferred_element_type=F32)
                t_idx = lax.broadcasted_iota(jnp.int32, (R, KN), 0) // H
                s_idx = lax.broadcasted_iota(jnp.int32, (R, KN), 1)
                update(sb, jnp.where(s_idx <= t_idx, s, NEG_BIG), kn[:, :KL])
                o_ref[sb] = (acc_sc[sb] / l_sc[sb]).astype(BF16)

    grid_spec = pltpu.PrefetchScalarGridSpec(
        num_scalar_prefetch=1, grid=(DB // SB, nj),
        in_specs=[pl.BlockSpec((SB, R, QW), lambda b, j, pt: (b, 0, 0)),
                  pl.BlockSpec((SB, KN, QW), lambda b, j, pt: (b, 0, 0)),
                  pl.BlockSpec(memory_space=pl.ANY),
                  pl.BlockSpec(memory_space=pl.ANY)],
        out_specs=pl.BlockSpec((SB, R, KL), lambda b, j, pt: (b, 0, 0)),
        scratch_shapes=[pltpu.VMEM((2, NP, PG, KL), F32), pltpu.VMEM((2, NP, ROPE, PG), F32),
                        pltpu.SemaphoreType.DMA((2,)),
                        pltpu.VMEM((SB, R, 1), F32), pltpu.VMEM((SB, R, 1), F32),
                        pltpu.VMEM((SB, R, KL), F32)])
    return pl.pallas_call(
        kern, grid_spec=grid_spec, out_shape=jax.ShapeDtypeStruct((DB, R, KL), BF16),
        compiler_params=_cparams(48), name="decode_attention")(
            page_table.reshape(-1), q_s, knew, cache_ckv, cache_krope_t)


def _ctx_to_attn_out(ctx, wuv, H, KL):
    T = ctx.shape[0]
    tm = _tile(T, LINEAR_TM)

    def kern(c_ref, w_ref, o_ref):
        for h in range(H):
            o_ref[:, h * V_DIM:(h + 1) * V_DIM] = _dot(
                c_ref[:, h * KL:(h + 1) * KL], w_ref[h]).astype(BF16)

    return pl.pallas_call(
        kern, grid=(T // tm,),
        in_specs=[pl.BlockSpec((tm, H * KL), lambda i: (i, 0)),
                  pl.BlockSpec(wuv.shape, lambda i: (0, 0, 0))],
        out_specs=pl.BlockSpec((tm, H * V_DIM), lambda i: (i, 0)),
        out_shape=jax.ShapeDtypeStruct((T, H * V_DIM), BF16),
        compiler_params=_cparams(40), name="ctx_to_attn_out")(ctx, wuv)


def _linear_residual(xs, ws, res, name):
    T, D = res.shape
    tm = _tile(T, LINEAR_TM)

    def epilogue(acc, x0, r_refs, c_refs, o_refs):
        o_refs[0][...] = r_refs[0][...] + acc

    (out,) = _fused_linear(xs, ws, tm=tm, epilogue=epilogue,
                           out_shapes=(jax.ShapeDtypeStruct((T, D), F32),),
                           out_specs=(pl.BlockSpec((tm, D), lambda i: (i, 0)),),
                           row_extras=(res,), name=name)
    return out


def _ffn(h, gain, w_gu, w_down):
    T, D = h.shape
    F = w_down.shape[0]
    tm = _tile(T, FFN_TM)
    tf = _tile(F, FFN_TF, LANES)
    nf = F // tf

    def kern(x_ref, g_ref, wg_ref, wu_ref, wd_ref, o_ref, xn_sc):
        f = pl.program_id(1)

        @pl.when(f == 0)
        def _():
            x = x_ref[...]
            xn_sc[...] = _rms(x, g_ref[...]).astype(BF16)
            o_ref[...] = x

        xn = xn_sc[...]
        act = (_silu(_dot(xn, wg_ref[...])) * _dot(xn, wu_ref[...])).astype(BF16)
        o_ref[...] += _dot(act, wd_ref[...])

    return pl.pallas_call(
        kern, grid=(T // tm, nf),
        in_specs=[pl.BlockSpec((tm, D), lambda i, f: (i, 0)),
                  pl.BlockSpec(gain.shape, lambda i, f: (0, 0)),
                  pl.BlockSpec((D, tf), lambda i, f: (0, f)),
                  pl.BlockSpec((D, tf), lambda i, f: (0, nf + f)),
                  pl.BlockSpec((tf, D), lambda i, f: (f, 0))],
        out_specs=pl.BlockSpec((tm, D), lambda i, f: (i, 0)),
        out_shape=jax.ShapeDtypeStruct((T, D), F32),
        scratch_shapes=[pltpu.VMEM((tm, D), BF16)],
        compiler_params=_cparams(48), name="ffn")(h, gain, w_gu, w_gu, w_down)


def _hgrn_in_proj(h, gain, w_in, lb):
    T, D = h.shape
    HK = w_in.shape[1] // 4
    tm = _tile(T, LINEAR_TM)

    def call(col, epilogue, dtypes, extras=()):
        out_shapes = tuple(jax.ShapeDtypeStruct((T, HK), dt) for dt in dtypes)
        out_specs = tuple(pl.BlockSpec((tm, HK), lambda i: (i, 0)) for _ in dtypes)
        return _fused_linear([h], [w_in], tm=tm, gain=gain, epilogue=epilogue,
                             wspecs=[pl.BlockSpec((D, HK), lambda i: (0, col))],
                             out_shapes=out_shapes, out_specs=out_specs, const_extras=extras,
                             name=f"hgrn_in_proj_{col}")

    def ep_silu(acc, x0, r_refs, c_refs, o_refs):
        o_refs[0][...] = _silu(acc).astype(BF16)

    def ep_ident(acc, x0, r_refs, c_refs, o_refs):
        o_refs[0][...] = acc.astype(BF16)

    def ep_gate(acc, x0, r_refs, c_refs, o_refs):
        k = (1.0 - c_refs[0][...]) / (1.0 + jnp.exp(acc))
        o_refs[0][...] = k.astype(BF16)
        o_refs[1][...] = jnp.log1p(-k)

    (qs,) = call(0, ep_silu, (BF16,))
    kk, logf = call(1, ep_gate, (BF16, F32), extras=(lb,))
    (vv,) = call(2, ep_ident, (BF16,))
    (gs,) = call(3, ep_silu, (BF16,))
    return qs, kk, logf, vv, gs


def _gla_band(qf, kf, vf, fg, n_diag, lane_sum):
    C = qf.shape[0]
    r8 = lax.broadcasted_iota(jnp.int32, qf.shape, 0) % SUBLANES
    gprod = None
    parts = []
    for d in range(n_diag):
        if d == 0:
            p = qf * kf
        else:
            fr = fg if d == 1 else pltpu.roll(fg, d - 1, 0)
            gprod = fr if gprod is None else gprod * fr
            p = jnp.where(r8 >= d, qf * gprod * pltpu.roll(kf, d, 0), 0.0)
        parts.append(p)
    sums = lane_sum(parts)
    out = None
    for d in range(n_diag):
        vr = vf if d == 0 else pltpu.roll(vf, d, 0)
        term = sums[d] * vr
        out = term if out is None else out + term
    return out


def _gla_prompt(qs, kk, logf, vv, gs, hg_norm, B, L, H):
    T, HK = qs.shape
    K = HK // H
    ct = _tile(L, GLA_CT, GLA_CHUNK)
    C = min(GLA_CHUNK, ct)
    nct = L // ct
    ncc = ct // C
    HPS = GLA_HEADS_PER_STEP if H % GLA_HEADS_PER_STEP == 0 else 1

    def kern(q_ref, k_ref, lf_ref, v_ref, g_ref, hn_ref, o_ref, s_ref, st_sc):
        ci = pl.program_id(2)

        @pl.when(ci == 0)
        def _():
            st_sc[...] = jnp.zeros((HPS, K, K), F32)

        rows = lax.broadcasted_iota(jnp.int32, (C, C), 0)
        cols = lax.broadcasted_iota(jnp.int32, (C, C), 1)
        tril = (rows >= cols).astype(F32)
        ones_bf = jnp.ones((K, K), BF16)

        def lane_sum(parts):
            r = _dot(jnp.concatenate([p.astype(BF16) for p in parts], axis=0), ones_bf)
            return [r[d * C:(d + 1) * C] for d in range(len(parts))]

        def chunk(c, hh, st):
            sl, hl = slice(c * C, (c + 1) * C), slice(hh * K, (hh + 1) * K)
            qf, kf, vf = q_ref[sl, hl].astype(F32), k_ref[sl, hl].astype(F32), v_ref[sl, hl].astype(F32)
            lf = lf_ref[sl, hl]
            fg = jnp.exp(lf)
            b = jnp.dot(tril, lf, precision=lax.Precision.HIGHEST, preferred_element_type=F32)
            blast = b[C - 1:C, :]
            qb = qf * jnp.exp(b)
            kb = kf * jnp.exp(blast - b)
            o = lax.dot_general(qb.astype(BF16), st.astype(BF16), NT_DIMS, preferred_element_type=F32)
            amat = None
            blk = C // 2
            while blk >= SUBLANES:
                q_parts, k_parts = [], []
                for m in range(C // blk):
                    r = slice(m * blk, (m + 1) * blk)
                    if m % 2 == 1:
                        ref = b[m * blk - 1:m * blk, :]
                        q_parts.append(qf[r] * jnp.exp(b[r] - ref))
                        k_parts.append(jnp.zeros((blk, K), F32))
                    else:
                        ref = b[(m + 1) * blk - 1:(m + 1) * blk, :]
                        q_parts.append(jnp.zeros((blk, K), F32))
                        k_parts.append(kf[r] * jnp.exp(ref - b[r]))
                ql = jnp.concatenate(q_parts, axis=0).astype(BF16)
                kl = jnp.concatenate(k_parts, axis=0).astype(BF16)
                al = lax.dot_general(ql, kl, NT_DIMS, preferred_element_type=F32)
                al = jnp.where((rows // (2 * blk)) == (cols // (2 * blk)), al, 0.0)
                amat = al if amat is None else amat + al
                blk //= 2
            if amat is not None:
                o = o + _dot(amat.astype(BF16), vf.astype(BF16))
            o = o + _gla_band(qf, kf, vf, fg, min(SUBLANES, C), lane_sum)
            st = st * jnp.exp(blast) + lax.dot_general(
                vf.astype(BF16), kb.astype(BF16), TN_DIMS, preferred_element_type=F32)
            on = _rms(o, hn_ref[...]) * g_ref[sl, hl].astype(F32)
            o_ref[sl, hl] = on.astype(BF16)
            return st

        sts = [st_sc[hh] for hh in range(HPS)]
        for c in range(ncc):
            sts = [chunk(c, hh, sts[hh]) for hh in range(HPS)]
        for hh in range(HPS):
            st_sc[hh] = sts[hh]

        @pl.when(ci == nct - 1)
        def _():
            for hh in range(HPS):
                s_ref[0, hh] = sts[hh].T

    tok = lambda b, h, ci: (b * nct + ci, h)
    return pl.pallas_call(
        kern, grid=(B, H // HPS, nct),
        in_specs=[pl.BlockSpec((ct, HPS * K), tok)] * 5 + [pl.BlockSpec(hg_norm.shape, lambda b, h, ci: (0, 0))],
        out_specs=(pl.BlockSpec((ct, HPS * K), tok),
                   pl.BlockSpec((1, HPS, K, K), lambda b, h, ci: (b, h, 0, 0))),
        out_shape=(jax.ShapeDtypeStruct((T, HK), BF16), jax.ShapeDtypeStruct((B, H, K, K), F32)),
        scratch_shapes=[pltpu.VMEM((HPS, K, K), F32)],
        compiler_params=_cparams(32), name="gla_prompt")(qs, kk, logf, vv, gs, hg_norm)


def _gla_sample(qs, kk, logf, vv, gs, hg_norm, s0, DB, Ld, H):
    HK = qs.shape[2]
    K = HK // H
    C = SUBLANES
    assert Ld <= C

    def kern(q_ref, k_ref, lf_ref, v_ref, g_ref, hn_ref, s0_ref, o_ref, s_ref, pad_sc):
        rows = lax.broadcasted_iota(jnp.int32, (C, C), 0)
        cols = lax.broadcasted_iota(jnp.int32, (C, C), 1)
        tril = (rows >= cols).astype(F32)

        def padded(ref, slot):
            pad_sc[slot] = jnp.zeros((C, HK), F32)
            pad_sc[slot, 0:Ld, :] = ref[0].astype(F32)
            return pad_sc[slot]

        qf, kf, lf, vf = padded(q_ref, 0), padded(k_ref, 1), padded(lf_ref, 2), padded(v_ref, 3)
        fg = jnp.exp(lf)
        b = jnp.dot(tril, lf, precision=lax.Precision.HIGHEST, preferred_element_type=F32)
        blast = b[C - 1:C, :]
        qb = (qf * jnp.exp(b)).astype(BF16)
        kb = (kf * jnp.exp(blast - b)).astype(BF16)
        dec = jnp.exp(blast)
        vb = vf.astype(BF16)

        def lane_sum(parts):
            return parts

        for h in range(H):
            hl = slice(h * K, (h + 1) * K)

            def head_sum(parts):
                return [jnp.sum(p, axis=-1, keepdims=True) for p in parts]

            st = s0_ref[0, h].T
            o = lax.dot_general(qb[:, hl], st.astype(BF16), NT_DIMS, preferred_element_type=F32)
            o = o + _gla_band(qf[:, hl], kf[:, hl], vf[:, hl], fg[:, hl], Ld, head_sum)
            st = st * dec[:, hl] + lax.dot_general(vb[:, hl], kb[:, hl], TN_DIMS,
                                                   preferred_element_type=F32)
            s_ref[0, h] = st.T
            on = _rms(o[0:Ld], hn_ref[...]) * g_ref[0, :, hl].astype(F32)
            o_ref[0, :, hl] = on.astype(BF16)

    tok = pl.BlockSpec((1, Ld, HK), lambda b: (b, 0, 0))
    st_spec = pl.BlockSpec((1, H, K, K), lambda b: (b, 0, 0, 0))
    return pl.pallas_call(
        kern, grid=(DB,),
        in_specs=[tok] * 5 + [pl.BlockSpec(hg_norm.shape, lambda b: (0, 0)), st_spec],
        out_specs=(tok, st_spec),
        out_shape=(jax.ShapeDtypeStruct((DB, Ld, HK), BF16), jax.ShapeDtypeStruct(s0.shape, F32)),
        scratch_shapes=[pltpu.VMEM((4, C, HK), F32)],
        compiler_params=_cparams(32), name="gla_sample")(qs, kk, logf, vv, gs, hg_norm, s0)


def _router(h, gain, w_router, b_router):
    T, D = h.shape
    E = w_router.shape[1]
    tm = _tile(T, LINEAR_TM)
    assert TOP_K == 2

    def kern(x_ref, g_ref, w_ref, b_ref, i_ref, o_ref):
        xn = _rms(x_ref[...], g_ref[...])
        logits = jnp.dot(xn, w_ref[...], precision=lax.Precision.HIGHEST,
                         preferred_element_type=F32) + b_ref[...]
        idx = lax.broadcasted_iota(jnp.int32, logits.shape, 1)
        m1 = jnp.max(logits, axis=-1, keepdims=True)
        i1 = jnp.min(jnp.where(logits == m1, idx, E), axis=-1, keepdims=True)
        rest = jnp.where(idx == i1, -jnp.inf, logits)
        m2 = jnp.max(rest, axis=-1, keepdims=True)
        i2 = jnp.min(jnp.where(rest == m2, idx, E), axis=-1, keepdims=True)
        e2 = jnp.exp(m2 - m1)
        g1 = 1.0 / (1.0 + e2)
        slot = lax.broadcasted_iota(jnp.int32, (tm, TOP_K), 1)
        i_ref[...] = jnp.where(slot == 0, i1, i2)
        o_ref[...] = jnp.where(slot == 0, g1, e2 * g1)

    return pl.pallas_call(
        kern, grid=(T // tm,),
        in_specs=[pl.BlockSpec((tm, D), lambda i: (i, 0)),
                  pl.BlockSpec(gain.shape, lambda i: (0, 0)),
                  pl.BlockSpec(w_router.shape, lambda i: (0, 0)),
                  pl.BlockSpec(b_router.shape, lambda i: (0, 0))],
        out_specs=(pl.BlockSpec((tm, TOP_K), lambda i: (i, 0)), pl.BlockSpec((tm, TOP_K), lambda i: (i, 0))),
        out_shape=(jax.ShapeDtypeStruct((T, TOP_K), jnp.int32), jax.ShapeDtypeStruct((T, TOP_K), F32)),
        compiler_params=_cparams(32), name="router")(h, gain, w_router, b_router)


def _route_plan(idx, E, tmx):
    T = idx.shape[0]
    NP = TOP_K * T + E * tmx
    assert NP % tmx == 0
    NT = NP // tmx
    sel = jnp.any(idx[:, :, None] == jnp.arange(E, dtype=jnp.int32), axis=1).astype(jnp.int32)
    incl = jnp.cumsum(sel, axis=0)
    counts = incl[-1]
    padded = ((counts + tmx - 1) // tmx) * tmx
    ends = jnp.cumsum(padded)
    offs = ends - padded
    rank = jnp.take_along_axis(incl - sel, idx, axis=1)
    pos = (offs[idx] + rank).reshape(-1).astype(jnp.int32)
    tok = jnp.repeat(jnp.arange(T, dtype=jnp.int32), TOP_K)
    src = jnp.zeros((NP,), jnp.int32).at[pos].set(tok)
    n_used = (ends[-1] // tmx).astype(jnp.int32)
    tile_start = jnp.minimum(jnp.arange(NT, dtype=jnp.int32), n_used - 1) * tmx
    tile_expert = jnp.minimum(jnp.sum(tile_start[:, None] >= ends[None, :], axis=1), E - 1).astype(jnp.int32)
    return pos, src, tile_expert, n_used.reshape(1)


def _moe_experts(h, gain, src, tile_expert, n_used, w_gu, w_down, tmx):
    T, D = h.shape
    E, F = w_down.shape[0], w_down.shape[1]
    NP = src.shape[0]
    NT = NP // tmx
    tf = _tile(F, MOE_TF, LANES)
    nf = F // tf

    def kern(src_ref, te_ref, nu_ref, h_ref, g_ref, wg_ref, wu_ref, wd_ref, o_ref, xbuf, xn_sc, sem):
        i, f = pl.program_id(0), pl.program_id(1)
        n_used_v = nu_ref[0]

        def row_copy(tile, r):
            return pltpu.make_async_copy(h_ref.at[pl.ds(src_ref[tile * tmx + r], 1)],
                                         xbuf.at[pl.ds(r, 1)], sem)

        def start_gather(tile):
            lax.fori_loop(0, tmx, lambda r, c: (row_copy(tile, r).start(), c)[1], 0, unroll=DMA_LOOP_UNROLL)

        @pl.when((f == 0) & (i == 0))
        def _():
            start_gather(0)

        @pl.when((f == 0) & (i < n_used_v))
        def _():
            lax.fori_loop(0, tmx, lambda r, c: (row_copy(i, r).wait(), c)[1], 0, unroll=DMA_LOOP_UNROLL)
            xn_sc[...] = _rms(xbuf[...], g_ref[...]).astype(BF16)

            @pl.when(i + 1 < n_used_v)
            def _():
                start_gather(i + 1)

        @pl.when(f == 0)
        def _():
            o_ref[...] = jnp.zeros((tmx, D), F32)

        @pl.when(i < n_used_v)
        def _():
            xn = xn_sc[...]
            act = (_silu(_dot(xn, wg_ref[0])) * _dot(xn, wu_ref[0])).astype(BF16)
            o_ref[...] += _dot(act, wd_ref[0])

    def wspec(shape, fn):
        def index_map(i, f, src_r, te_r, nu_r):
            return fn(te_r[i], jnp.where(i < nu_r[0], f, nf - 1))
        return pl.BlockSpec(shape, index_map)

    grid_spec = pltpu.PrefetchScalarGridSpec(
        num_scalar_prefetch=3, grid=(NT, nf),
        in_specs=[pl.BlockSpec(memory_space=pl.ANY),
                  pl.BlockSpec(gain.shape, lambda i, f, *_: (0, 0)),
                  wspec((1, D, tf), lambda e, f: (e, 0, f)),
                  wspec((1, D, tf), lambda e, f: (e, 0, nf + f)),
                  wspec((1, tf, D), lambda e, f: (e, f, 0))],
        out_specs=pl.BlockSpec((tmx, D), lambda i, f, *_: (i, 0)),
        scratch_shapes=[pltpu.VMEM((tmx, D), F32), pltpu.VMEM((tmx, D), BF16),
                        pltpu.SemaphoreType.DMA(())])
    return pl.pallas_call(
        kern, grid_spec=grid_spec, out_shape=jax.ShapeDtypeStruct((NP, D), F32),
        compiler_params=_cparams(48), name="moe_experts")(
            src, tile_expert, n_used, h, gain, w_gu, w_gu, w_down)


def _moe_combine(h, y_sorted, pos, gate, final_gain):
    T, D = h.shape
    tm = _tile(T, MOE_COMBINE_TM)
    nt = T // tm

    def kern(pos_ref, h_ref, y_ref, gt_ref, fg_ref, o_ref, ybuf, sems):
        i = pl.program_id(0)

        def row_copy(step, r, s):
            slot = step % 2
            return pltpu.make_async_copy(y_ref.at[pl.ds(pos_ref[(step * tm + r) * TOP_K + s], 1)],
                                         ybuf.at[slot, s, pl.ds(r, 1)], sems.at[slot])

        def for_rows(step, fn):
            def body(r, c):
                for s in range(TOP_K):
                    fn(row_copy(step, r, s))
                return c
            lax.fori_loop(0, tm, body, 0, unroll=DMA_LOOP_UNROLL)

        @pl.when(i == 0)
        def _():
            for_rows(0, lambda cp: cp.start())

        @pl.when(i + 1 < nt)
        def _():
            for_rows(i + 1, lambda cp: cp.start())

        for_rows(i, lambda cp: cp.wait())
        slot = i % 2
        gt = gt_ref[...]
        y = h_ref[...]
        for s in range(TOP_K):
            y = y + gt[:, s:s + 1] * ybuf[slot, s]
        o_ref[...] = _rms(y, fg_ref[...])

    grid_spec = pltpu.PrefetchScalarGridSpec(
        num_scalar_prefetch=1, grid=(nt,),
        in_specs=[pl.BlockSpec((tm, D), lambda i, *_: (i, 0)),
                  pl.BlockSpec(memory_space=pl.ANY),
                  pl.BlockSpec((tm, TOP_K), lambda i, *_: (i, 0)),
                  pl.BlockSpec(final_gain.shape, lambda i, *_: (0, 0))],
        out_specs=pl.BlockSpec((tm, D), lambda i, *_: (i, 0)),
        scratch_shapes=[pltpu.VMEM((2, TOP_K, tm, D), F32), pltpu.SemaphoreType.DMA((2,))])
    return pl.pallas_call(
        kern, grid_spec=grid_spec, out_shape=jax.ShapeDtypeStruct((T, D), F32),
        compiler_params=_cparams(40), name="moe_combine")(pos, h, y_sorted, gate, final_gain)


def _moe(h, gain, w_router, b_router, w_gu, w_down, final_gain):
    T = h.shape[0]
    E = w_down.shape[0]
    tmx = _tile(max(MOE_MIN_TILE, 2 * TOP_K * T // E), MOE_TM)
    idx, gate = _router(h, gain, w_router, b_router)
    pos, src, tile_expert, n_used = _route_plan(idx, E, tmx)
    y_sorted = _moe_experts(h, gain, src, tile_expert, n_used, w_gu, w_down, tmx)
    return _moe_combine(h, y_sorted, pos, gate, final_gain)


def _rope_rows(pos, rope_dim):
    inv = ROPE_THETA ** (-jnp.arange(0, rope_dim, 2, dtype=F32) / rope_dim)
    ang = pos.astype(F32)[:, None] * inv[None, :]
    z = jnp.zeros((pos.shape[0], LANES - rope_dim), F32)
    cos, sin = jnp.cos(ang), jnp.sin(ang)
    return jnp.concatenate([cos, cos, z], axis=-1), jnp.concatenate([sin, sin, z], axis=-1)


def _rot_cols(w):
    half = w.shape[-1] // 2
    return jnp.concatenate([-w[..., half:], w[..., :half]], axis=-1)


def _pad_lanes(w):
    pad = LANES - w.shape[-1]
    return jnp.pad(w, [(0, 0)] * (w.ndim - 1) + [(0, pad)])


def _prep_even(a, w_in_e, w_q_b, w_kv_b, pool_w, w_out_e, w_ffn_gu, w_ffn_down, dims, H):
    DP, QL, KL, ROPE = dims
    w_in = w_in_e[a]
    k_raw = w_in[:, DP + QL + KL:]
    w_ext = jnp.concatenate([w_in[:, :DP + QL + KL], _pad_lanes(k_raw), _pad_lanes(_rot_cols(k_raw))],
                            axis=-1).astype(BF16)
    wq = w_q_b[a].reshape(QL, H, NOPE_DIM + ROPE)
    wq_rope = wq[..., NOPE_DIM:]
    wq_ext = jnp.concatenate([wq[..., :NOPE_DIM].reshape(QL, H * NOPE_DIM),
                              _pad_lanes(wq_rope).reshape(QL, H * LANES),
                              _pad_lanes(_rot_cols(wq_rope)).reshape(QL, H * LANES)], axis=-1).astype(BF16)
    w_kv = w_kv_b[a].reshape(KL, H, NOPE_DIM + V_DIM)
    wuk_t = jnp.transpose(w_kv[..., :NOPE_DIM], (1, 2, 0)).astype(BF16)
    wuv = jnp.transpose(w_kv[..., NOPE_DIM:], (1, 0, 2)).astype(BF16)
    w_out = w_out_e[a].astype(BF16)
    return dict(w_ext=w_ext, wq_ext=wq_ext, wuk_t=wuk_t, wuv=wuv, pool_w=pool_w[a].astype(BF16),
                w_out_pool=w_out[:DP], w_out_attn=w_out[DP:],
                w_gu=w_ffn_gu[a].astype(BF16), w_down=w_ffn_down[a].astype(BF16))


def kernel(x_prompt, x_sample, cache_ckv, cache_krope, page_table, state_pool, state_hgrn,
           norm_mix_e, w_in_e, q_norm, w_q_b, kv_norm, w_kv_b, pool_w, pool_scale, w_out_e,
           norm_ffn_e, w_ffn_gu, w_ffn_down,
           norm_mix_o, w_in_o, hg_lower_bound, hg_norm, w_out_o, norm_ffn_o, w_router, b_router,
           w_exp_gu, w_exp_down, final_norm):
    B, L, D = x_prompt.shape
    DB, Ld, _ = x_sample.shape
    n_pages = page_table.shape[1]
    PG = cache_ckv.shape[2]
    past_len = n_pages * PG
    KL, ROPE = cache_ckv.shape[3], cache_krope.shape[3]
    DP, PS = state_pool.shape[3], state_pool.shape[2]
    QL = q_norm.shape[1]
    H = w_q_b.shape[2] // (NOPE_DIM + ROPE)
    HG = state_hgrn.shape[2]
    depth = hg_lower_bound.shape[0]
    dims = (DP, QL, KL, ROPE)
    scale = float((NOPE_DIM + ROPE) ** -0.5 * math.log2(math.e))
    cache_krope_t = jnp.swapaxes(cache_krope, 2, 3)
    QW = KL + LANES
    KN = 16
    assert Ld <= KN and ROPE <= LANES

    row = lambda v: v.reshape(1, -1).astype(F32)
    lb_p = jax.nn.softmax(hg_lower_bound.astype(F32), axis=0)
    lower_bounds = jnp.cumsum(lb_p, axis=0) - lb_p[0]

    hp = x_prompt.reshape(B * L, D)
    hs = x_sample.reshape(DB * Ld, D)
    cos_p, sin_p = _rope_rows(jnp.tile(jnp.arange(L), B), ROPE)
    cos_s, sin_s = _rope_rows(jnp.tile(past_len + jnp.arange(Ld), DB), ROPE)

    outs_p = dict(ckv=[], krope=[], pool=[], hgrn=[])
    outs_s = dict(ckv=[], krope=[], pool=[], hgrn=[])
    for l in range(depth):
        a = l // 2
        if l % 2 == 0:
            w = _prep_even(a, w_in_e, w_q_b, w_kv_b, pool_w, w_out_e, w_ffn_gu, w_ffn_down, dims, H)
            g_mix, g_q, g_kv = row(norm_mix_e[a]), row(q_norm[a]), row(kv_norm[a])
            g_ffn, p_scale = row(norm_ffn_e[a]), row(pool_scale[a])

            u, qn, ckv, kpe, kcat, ckv_t = _even_in_proj(hp, g_mix, w["w_ext"], g_q, g_kv, cos_p, sin_p,
                                                         dims, True)
            pool_out, pool_new = _pool_prompt(u, jnp.zeros((B, PS, DP), F32), w["pool_w"], p_scale, B, L)
            qcat = _q_proj(qn, w["wq_ext"], w["wuk_t"], cos_p, sin_p, H, KL, scale, head_major=True)
            attn = _flash_prompt(qcat, kcat, ckv_t, w["wuv"], B, L, H, KL)
            hp = _linear_residual([pool_out, attn], [w["w_out_pool"], w["w_out_attn"]], hp, "out_proj_e")
            hp = _ffn(hp, g_ffn, w["w_gu"], w["w_down"])
            outs_p["ckv"].append(ckv.reshape(B, L, KL))
            outs_p["krope"].append(kpe.reshape(B, L, ROPE))
            outs_p["pool"].append(pool_new)

            u, qn, ckv, kpe, kcat = _even_in_proj(hs, g_mix, w["w_ext"], g_q, g_kv, cos_s, sin_s, dims, False)
            u3 = u.reshape(DB, Ld, DP)
            ext_tm = jnp.transpose(jnp.concatenate([state_pool[a], u3], axis=1), (1, 0, 2))
            pool_tm = _pool_sample(ext_tm, w["pool_w"], p_scale, Ld, past_len)
            pool_out = jnp.transpose(pool_tm, (1, 0, 2)).reshape(DB * Ld, DP)
            qrows = _q_proj(qn, w["wq_ext"], w["wuk_t"], cos_s, sin_s, H, KL, scale, head_major=False)
            q_s = qrows.reshape(DB, Ld * H, QW)
            knew = jnp.pad(kcat.reshape(DB, Ld, QW), ((0, 0), (0, KN - Ld), (0, 0)))
            ctx = _decode_attention(page_table, q_s, knew, cache_ckv, cache_krope_t, a, H, Ld, KL, ROPE)
            attn = _ctx_to_attn_out(ctx.reshape(DB * Ld, H * KL), w["wuv"], H, KL)
            hs = _linear_residual([pool_out, attn], [w["w_out_pool"], w["w_out_attn"]], hs, "out_proj_e")
            hs = _ffn(hs, g_ffn, w["w_gu"], w["w_down"])
            outs_s["ckv"].append(ckv.reshape(DB, Ld, KL))
            outs_s["krope"].append(kpe.reshape(DB, Ld, ROPE))
            outs_s["pool"].append(jnp.concatenate([state_pool[a], u3], axis=1)[:, -PS:])
        else:
            w_in = w_in_o[a].astype(BF16)
            w_out = w_out_o[a].astype(BF16)
            w_gu = w_exp_gu[a].astype(BF16)
            w_dn = w_exp_down[a].astype(BF16)
            g_mix, g_ffn, g_hn = row(norm_mix_o[a]), row(norm_ffn_o[a]), row(hg_norm[a])
            lb = row(lower_bounds[l])
            w_r, b_r = w_router[a].astype(F32), row(b_router[a])
            g_fin = row(final_norm) if l == depth - 1 else None
            assert g_fin is not None

            qs, kk, logf, vv, gs = _hgrn_in_proj(hp, g_mix, w_in, lb)
            on, s_new = _gla_prompt(qs, kk, logf, vv, gs, g_hn, B, L, HG)
            hp = _linear_residual([on], [w_out], hp, "out_proj_o")
            hp = _moe(hp, g_ffn, w_r, b_r, w_gu, w_dn, g_fin)
            outs_p["hgrn"].append(s_new)

            qs, kk, logf, vv, gs = _hgrn_in_proj(hs, g_mix, w_in, lb)
            r3 = lambda t: t.reshape(DB, Ld, -1)
            on, s_new = _gla_sample(r3(qs), r3(kk), r3(logf), r3(vv), r3(gs), g_hn, state_hgrn[a], DB, Ld, HG)
            hs = _linear_residual([on.reshape(DB * Ld, -1)], [w_out], hs, "out_proj_o")
            hs = _moe(hs, g_ffn, w_r, b_r, w_gu, w_dn, g_fin)
            outs_s["hgrn"].append(s_new)

    return (hp.reshape(B, L, D), hs.reshape(DB, Ld, D),
            jnp.stack(outs_p["ckv"]), jnp.stack(outs_p["krope"]), jnp.stack(outs_p["pool"]),
            jnp.stack(outs_p["hgrn"]),
            jnp.stack(outs_s["ckv"]), jnp.stack(outs_s["krope"]), jnp.stack(outs_s["pool"]),
            jnp.stack(outs_s["hgrn"]))
```

```python
import functools
import math

import jax
import jax.numpy as jnp
from jax import lax
from jax.experimental import pallas as pl
from jax.experimental.pallas import tpu as pltpu

F32 = jnp.float32
BF16 = jnp.bfloat16

EPS = 1e-6
POOL_WINDOWS = (2, 4, 8, 16)
NOPE_DIM = 128
V_DIM = 128
ROPE_THETA = 10000.0
TOP_K = 2
LANES = 128
SUBLANES = 8
NEG_BIG = -1e30

LINEAR_TM = 512
FFN_TM = 512
FFN_TF = 512
MOE_TM = 512
MOE_MIN_TILE = 16
MOE_TF = 256
MOE_COMBINE_TM = 256
DMA_LOOP_UNROLL = 8
FLASH_TQ = 128
FLASH_TK = 512
POOL_TL = 512
GLA_CT = 256
GLA_CHUNK = 256
GLA_HEADS_PER_STEP = 4
DECODE_PAGES_PER_STEP = 16
DECODE_SEQS_PER_STEP = 2

NT_DIMS = (((1,), (1,)), ((), ()))
TN_DIMS = (((0,), (0,)), ((), ()))


def _cparams(vmem_mb):
    return pltpu.CompilerParams(vmem_limit_bytes=vmem_mb * 2 ** 20)


def _tile(n, pref, mult=SUBLANES):
    if n <= pref:
        return n
    for t in range(pref, 0, -1):
        if n % t == 0 and t % mult == 0:
            return t
    return n


def _rms(x, g):
    return x * lax.rsqrt(jnp.mean(x * x, axis=-1, keepdims=True) + EPS) * g


def _silu(x):
    return x / (1.0 + jnp.exp(-x))


def _dot(a, b):
    return jnp.dot(a, b, preferred_element_type=F32)


def _fused_linear(xs, ws, *, tm, epilogue, out_shapes, out_specs, gain=None, wspecs=None,
                  row_extras=(), const_extras=(), vmem_mb=48, name="linear"):
    T = xs[0].shape[0]
    grid = (T // tm,)
    n_x, n_w, n_r, n_c = len(xs), len(ws), len(row_extras), len(const_extras)

    def kern(*refs):
        pos = 0
        x_refs = refs[pos:pos + n_x]; pos += n_x
        gain_ref = None
        if gain is not None:
            gain_ref = refs[pos]; pos += 1
        w_refs = refs[pos:pos + n_w]; pos += n_w
        r_refs = refs[pos:pos + n_r]; pos += n_r
        c_refs = refs[pos:pos + n_c]; pos += n_c
        o_refs = refs[pos:]
        acc = None
        x0 = None
        for xr, wr in zip(x_refs, w_refs):
            xv = xr[...]
            if gain_ref is not None:
                x0 = xv
                xv = _rms(xv, gain_ref[...]).astype(BF16)
            d = _dot(xv, wr[...])
            acc = d if acc is None else acc + d
        epilogue(acc, x0, r_refs, c_refs, o_refs)

    in_specs = [pl.BlockSpec((tm, x.shape[1]), lambda i: (i, 0)) for x in xs]
    args = list(xs)
    if gain is not None:
        in_specs.append(pl.BlockSpec(gain.shape, lambda i: (0, 0)))
        args.append(gain)
    if wspecs is None:
        wspecs = [pl.BlockSpec(w.shape, lambda i: (0,) * w.ndim) for w in ws]
    in_specs += list(wspecs)
    args += list(ws)
    for r in row_extras:
        in_specs.append(pl.BlockSpec((tm, r.shape[1]), lambda i: (i, 0)))
        args.append(r)
    for c in const_extras:
        in_specs.append(pl.BlockSpec(c.shape, lambda i, nd=c.ndim: (0,) * nd))
        args.append(c)
    return pl.pallas_call(
        kern, grid=grid, in_specs=in_specs, out_specs=out_specs, out_shape=out_shapes,
        compiler_params=_cparams(vmem_mb), name=name)(*args)


def _even_in_proj(h, gain, w_ext, qn_g, kvn_g, cos_rows, sin_rows, dims, with_ckv_t):
    DP, QL, KL, ROPE = dims
    T, D = h.shape
    tm = _tile(T, LINEAR_TM)
    o_kr = DP + QL + KL

    def epilogue(acc, x0, r_refs, c_refs, o_refs):
        u_ref, qn_ref, ckv_ref, kpe_ref, kcat_ref = o_refs[:5]
        cos, sin = r_refs[0][...], r_refs[1][...]
        u_ref[...] = acc[:, :DP]
        qn_ref[...] = _rms(acc[:, DP:DP + QL], c_refs[0][...]).astype(BF16)
        ckv = _rms(acc[:, DP + QL:o_kr], c_refs[1][...])
        ckv_ref[...] = ckv
        kpe = acc[:, o_kr:o_kr + LANES] * cos + acc[:, o_kr + LANES:o_kr + 2 * LANES] * sin
        kpe_ref[...] = kpe[:, :ROPE]
        kcat_ref[:, :KL] = ckv.astype(BF16)
        kcat_ref[:, KL:] = kpe.astype(BF16)
        if with_ckv_t:
            o_refs[5][...] = ckv.T.astype(BF16)

    out_shapes = (jax.ShapeDtypeStruct((T, DP), F32), jax.ShapeDtypeStruct((T, QL), BF16),
                  jax.ShapeDtypeStruct((T, KL), F32), jax.ShapeDtypeStruct((T, ROPE), F32),
                  jax.ShapeDtypeStruct((T, KL + LANES), BF16))
    out_specs = tuple(pl.BlockSpec((tm, s.shape[1]), lambda i: (i, 0)) for s in out_shapes)
    if with_ckv_t:
        out_shapes += (jax.ShapeDtypeStruct((KL, T), BF16),)
        out_specs += (pl.BlockSpec((KL, tm), lambda i: (0, i)),)
    return _fused_linear([h], [w_ext], tm=tm, gain=gain, epilogue=epilogue,
                         out_shapes=out_shapes, out_specs=out_specs,
                         row_extras=(cos_rows, sin_rows), const_extras=(qn_g, kvn_g),
                         name="even_in_proj")


def _q_proj(qn, wq_ext, wuk_t, cos_rows, sin_rows, H, KL, scale, head_major):
    T = qn.shape[0]
    tm = _tile(T, LINEAR_TM)
    QW = KL + LANES
    r0, r1 = H * NOPE_DIM, 2 * H * NOPE_DIM

    def epilogue(acc, x0, r_refs, c_refs, o_refs):
        (o_ref,) = o_refs
        cos, sin = r_refs[0][...], r_refs[1][...]
        wuk_ref = c_refs[0]
        for h in range(H):
            qn_h = acc[:, h * NOPE_DIM:(h + 1) * NOPE_DIM].astype(BF16)
            q_abs = (_dot(qn_h, wuk_ref[h]) * scale).astype(BF16)
            q_pe = ((acc[:, r0 + h * LANES:r0 + (h + 1) * LANES] * cos
                     + acc[:, r1 + h * LANES:r1 + (h + 1) * LANES] * sin) * scale).astype(BF16)
            if head_major:
                o_ref[h, :, :KL] = q_abs
                o_ref[h, :, KL:] = q_pe
            else:
                o_ref[:, h * QW:h * QW + KL] = q_abs
                o_ref[:, h * QW + KL:(h + 1) * QW] = q_pe

    if head_major:
        out_shape = jax.ShapeDtypeStruct((H, T, QW), BF16)
        out_spec = pl.BlockSpec((H, tm, QW), lambda i: (0, i, 0))
    else:
        out_shape = jax.ShapeDtypeStruct((T, H * QW), BF16)
        out_spec = pl.BlockSpec((tm, H * QW), lambda i: (i, 0))
    (out,) = _fused_linear([qn], [wq_ext], tm=tm, epilogue=epilogue, out_shapes=(out_shape,),
                           out_specs=(out_spec,), row_extras=(cos_rows, sin_rows),
                           const_extras=(wuk_t,), name="q_proj")
    return out


def _pool_group_out(ext_ref, base, tl, pos0, g, w, PG, pw_ref, scale_ref):
    lo, hi = g * PG, (g + 1) * PG
    x = ext_ref[base:base + tl, lo:hi]
    win = x
    for j in range(1, w):
        win = win + ext_ref[base - j:base - j + tl, lo:hi]
    if pos0 is None:
        d = win * (1.0 / w) - x
    else:
        pos = pos0 + lax.broadcasted_iota(jnp.int32, (tl, 1), 0)
        cnt = jnp.minimum(pos + 1, w).astype(F32)
        d = win / cnt - x
    return _dot(d.astype(BF16), pw_ref[g]) * scale_ref[:, lo:hi]


def _pool_prompt(u, past, pool_w, pool_scale, B, L):
    DP = u.shape[1]
    PS = past.shape[1]
    HALO = 16
    G = len(POOL_WINDOWS)
    PG = DP // G
    tl = _tile(L, POOL_TL, HALO)
    nl = L // tl

    def kern(u_ref, halo_ref, past_ref, pw_ref, sc_ref, o_ref, new_ref, ext_ref):
        i = pl.program_id(1)

        @pl.when(i == 0)
        def _():
            ext_ref[0:1, :] = jnp.zeros((1, DP), F32)
            ext_ref[HALO - PS:HALO, :] = past_ref[0]

        @pl.when(i > 0)
        def _():
            ext_ref[0:HALO, :] = halo_ref[...]

        ext_ref[HALO:HALO + tl, :] = u_ref[...]
        for g, w in enumerate(POOL_WINDOWS):
            o_ref[:, g * PG:(g + 1) * PG] = _pool_group_out(
                ext_ref, HALO, tl, i * tl, g, w, PG, pw_ref, sc_ref).astype(BF16)

        @pl.when(i == nl - 1)
        def _():
            new_ref[0] = ext_ref[HALO + tl - PS:HALO + tl, :]

    r = tl // HALO
    return pl.pallas_call(
        kern, grid=(B, nl),
        in_specs=[
            pl.BlockSpec((tl, DP), lambda b, i: (b * nl + i, 0)),
            pl.BlockSpec((HALO, DP), lambda b, i: (jnp.maximum((b * nl + i) * r - 1, 0), 0)),
            pl.BlockSpec((1, PS, DP), lambda b, i: (b, 0, 0)),
            pl.BlockSpec(pool_w.shape, lambda b, i: (0, 0, 0)),
            pl.BlockSpec(pool_scale.shape, lambda b, i: (0, 0)),
        ],
        out_specs=(pl.BlockSpec((tl, DP), lambda b, i: (b * nl + i, 0)),
                   pl.BlockSpec((1, PS, DP), lambda b, i: (b, 0, 0))),
        out_shape=(jax.ShapeDtypeStruct((B * L, DP), BF16), jax.ShapeDtypeStruct((B, PS, DP), F32)),
        scratch_shapes=[pltpu.VMEM((HALO + tl, DP), F32)],
        compiler_params=_cparams(40), name="pool_prompt")(u, u, past, pool_w, pool_scale)


def _pool_sample(ext_tm, pool_w, pool_scale, Ld, start):
    R, DB, DP = ext_tm.shape
    PS = R - Ld
    G = len(POOL_WINDOWS)
    PG = DP // G

    def kern(e_ref, pw_ref, sc_ref, o_ref):
        for t in range(Ld):
            for g, w in enumerate(POOL_WINDOWS):
                lo, hi = g * PG, (g + 1) * PG
                x = e_ref[PS + t, :, lo:hi]
                win = x
                for j in range(1, w):
                    win = win + e_ref[PS + t - j, :, lo:hi]
                cnt = float(min(start + t + 1, w))
                d = win / cnt - x
                o_ref[t, :, lo:hi] = (_dot(d.astype(BF16), pw_ref[g]) * sc_ref[:, lo:hi]).astype(BF16)

    return pl.pallas_call(
        kern, grid=(1,),
        in_specs=[pl.BlockSpec(ext_tm.shape, lambda i: (0, 0, 0)),
                  pl.BlockSpec(pool_w.shape, lambda i: (0, 0, 0)),
                  pl.BlockSpec(pool_scale.shape, lambda i: (0, 0))],
        out_specs=pl.BlockSpec((Ld, DB, DP), lambda i: (0, 0, 0)),
        out_shape=jax.ShapeDtypeStruct((Ld, DB, DP), BF16),
        compiler_params=_cparams(48), name="pool_sample")(ext_tm, pool_w, pool_scale)


def _flash_prompt(qcat, kcat, ckv_t, wuv, B, L, H, KL):
    QW = qcat.shape[2]
    tq = _tile(L, FLASH_TQ, LANES)
    tk = _tile(L, FLASH_TK, LANES)
    assert tk % tq == 0 and tq % LANES == 0
    nq, nk = L // tq, L // tk
    R = H * tq

    def last_needed(qi):
        return (qi * tq + tq - 1) // tk

    pairs = [(qi, ki) for qi in range(nq) for ki in range(last_needed(qi) + 1)]
    qi_tab = jnp.asarray([p[0] for p in pairs], jnp.int32)
    ki_tab = jnp.asarray([p[1] for p in pairs], jnp.int32)

    def kern(qi_ref, ki_ref, q_ref, k_ref, kt_ref, wuv_ref, o_ref, m_sc, l_sc, acc_sc):
        qi, ki = qi_ref[pl.program_id(1)], ki_ref[pl.program_id(1)]

        @pl.when(ki == 0)
        def _():
            m_sc[...] = jnp.full((1, R), NEG_BIG, F32)
            l_sc[...] = jnp.zeros((1, R), F32)
            acc_sc[...] = jnp.zeros((KL, R), F32)

        def update(masked, nkeys):
            q = q_ref[...].reshape(R, QW)
            s = lax.dot_general(k_ref[0:nkeys, :], q, NT_DIMS, preferred_element_type=F32)
            if masked:
                kpos = ki * tk + lax.broadcasted_iota(jnp.int32, (nkeys, R), 0)
                qpos = qi * tq + (lax.broadcasted_iota(jnp.int32, (nkeys, R), 1) % tq)
                s = jnp.where(kpos <= qpos, s, NEG_BIG)
            m_prev = m_sc[...]
            m_new = jnp.maximum(m_prev, jnp.max(s, axis=0, keepdims=True))
            alpha = jnp.exp2(m_prev - m_new)
            p = jnp.exp2(s - m_new)
            l_sc[...] = alpha * l_sc[...] + jnp.sum(p, axis=0, keepdims=True)
            acc_sc[...] = alpha * acc_sc[...] + _dot(kt_ref[:, 0:nkeys], p.astype(BF16))
            m_sc[...] = m_new

        pl.when(ki < last_needed(qi))(functools.partial(update, False, tk))
        for v in range(tk // tq):
            pl.when((ki == last_needed(qi)) & (qi % (tk // tq) == v))(
                functools.partial(update, True, (v + 1) * tq))

        @pl.when(ki == last_needed(qi))
        def _():
            ctx_t = (acc_sc[...] / l_sc[...]).astype(BF16)
            for h in range(H):
                o_ref[:, h * V_DIM:(h + 1) * V_DIM] = lax.dot_general(
                    ctx_t[:, h * tq:(h + 1) * tq], wuv_ref[h], TN_DIMS,
                    preferred_element_type=F32).astype(BF16)

    grid_spec = pltpu.PrefetchScalarGridSpec(
        num_scalar_prefetch=2, grid=(B, len(pairs)),
        in_specs=[
            pl.BlockSpec((H, tq, QW), lambda b, s, qt, kt: (0, b * nq + qt[s], 0)),
            pl.BlockSpec((tk, QW), lambda b, s, qt, kt: (b * nk + kt[s], 0)),
            pl.BlockSpec((KL, tk), lambda b, s, qt, kt: (0, b * nk + kt[s])),
            pl.BlockSpec(wuv.shape, lambda b, s, qt, kt: (0, 0, 0)),
        ],
        out_specs=pl.BlockSpec((tq, H * V_DIM), lambda b, s, qt, kt: (b * nq + qt[s], 0)),
        scratch_shapes=[pltpu.VMEM((1, R), F32), pltpu.VMEM((1, R), F32), pltpu.VMEM((KL, R), F32)])
    return pl.pallas_call(
        kern, grid_spec=grid_spec, out_shape=jax.ShapeDtypeStruct((B * L, H * V_DIM), BF16),
        compiler_params=_cparams(56), name="flash_prompt")(qi_tab, ki_tab, qcat, kcat, ckv_t, wuv)


def _decode_attention(page_table, q_s, knew, cache_ckv, cache_krope_t, a, H, Ld, KL, ROPE):
    DB, R, QW = q_s.shape
    KN = knew.shape[1]
    n_pages = page_table.shape[1]
    PG = cache_ckv.shape[2]
    PP = DECODE_PAGES_PER_STEP
    while n_pages % PP:
        PP //= 2
    nj = n_pages // PP
    SB = DECODE_SEQS_PER_STEP if DB % DECODE_SEQS_PER_STEP == 0 else 1
    NP = SB * PP
    n_steps = (DB // SB) * nj

    def kern(pt_ref, q_ref, kn_ref, ck_hbm, kr_hbm, o_ref, ckbuf, krbuf, sems, m_sc, l_sc, acc_sc):
        bi, j = pl.program_id(0), pl.program_id(1)
        n = bi * nj + j

        def page_copies(step, i):
            sb, p = i // PP, i % PP
            seq = (step // nj) * SB + sb
            page = pt_ref[seq * n_pages + (step % nj) * PP + p]
            slot = step % 2
            return (pltpu.make_async_copy(ck_hbm.at[a, page], ckbuf.at[slot, i], sems.at[slot]),
                    pltpu.make_async_copy(kr_hbm.at[a, page], krbuf.at[slot, i], sems.at[slot]))

        def for_pages(step, fn):
            def body(i, c):
                for cp in page_copies(step, i):
                    fn(cp)
                return c
            lax.fori_loop(0, NP, body, 0, unroll=DMA_LOOP_UNROLL)

        @pl.when(n == 0)
        def _():
            for_pages(0, lambda cp: cp.start())

        @pl.when(n + 1 < n_steps)
        def _():
            for_pages(n + 1, lambda cp: cp.start())

        @pl.when(j == 0)
        def _():
            m_sc[...] = jnp.full((SB, R, 1), NEG_BIG, F32)
            l_sc[...] = jnp.zeros((SB, R, 1), F32)
            acc_sc[...] = jnp.zeros((SB, R, KL), F32)

        for_pages(n, lambda cp: cp.wait())
        slot = n % 2

        def update(sb, s, v):
            m_prev = m_sc[sb]
            m_new = jnp.maximum(m_prev, jnp.max(s, axis=-1, keepdims=True))
            alpha = jnp.exp2(m_prev - m_new)
            p = jnp.exp2(s - m_new)
            l_sc[sb] = alpha * l_sc[sb] + jnp.sum(p, axis=-1, keepdims=True)
            acc_sc[sb] = alpha * acc_sc[sb] + _dot(p.astype(BF16), v)
            m_sc[sb] = m_new

        for sb in range(SB):
            q = q_ref[sb]
            ck = ckbuf[slot, sb * PP:(sb + 1) * PP].reshape(PP * PG, KL).astype(BF16)
            kr_t = jnp.concatenate([krbuf[slot, sb * PP + p].astype(BF16) for p in range(PP)], axis=1)
            s = (lax.dot_general(q[:, :KL], ck, NT_DIMS, preferred_element_type=F32)
                 + _dot(q[:, KL:KL + ROPE], kr_t))
            update(sb, s, ck)

        @pl.when(j == nj - 1)
        def _():
            for sb in range(SB):
                q, kn = q_ref[sb], kn_ref[sb]
                s = lax.dot_general(q, kn, NT_DIMS, preferred_element_type=F32)
                t_idx = lax.broadcasted_iota(jnp.int32, (R, KN), 0) // H
                s_idx = lax.broadcasted_iota(jnp.int32, (R, KN), 1)
                update(sb, jnp.where(s_idx <= t_idx, s, NEG_BIG), kn[:, :KL])
                o_ref[sb] = (acc_sc[sb] / l_sc[sb]).astype(BF16)

    grid_spec = pltpu.PrefetchScalarGridSpec(
        num_scalar_prefetch=1, grid=(DB // SB, nj),
        in_specs=[pl.BlockSpec((SB, R, QW), lambda b, j, pt: (b, 0, 0)),
                  pl.BlockSpec((SB, KN, QW), lambda b, j, pt: (b, 0, 0)),
                  pl.BlockSpec(memory_space=pl.ANY),
                  pl.BlockSpec(memory_space=pl.ANY)],
        out_specs=pl.BlockSpec((SB, R, KL), lambda b, j, pt: (b, 0, 0)),
        scratch_shapes=[pltpu.VMEM((2, NP, PG, KL), F32), pltpu.VMEM((2, NP, ROPE, PG), F32),
                        pltpu.SemaphoreType.DMA((2,)),
                        pltpu.VMEM((SB, R, 1), F32), pltpu.VMEM((SB, R, 1), F32),
                        pltpu.VMEM((SB, R, KL), F32)])
    return pl.pallas_call(
        kern, grid_spec=grid_spec, out_shape=jax.ShapeDtypeStruct((DB, R, KL), BF16),
        compiler_params=_cparams(48), name="decode_attention")(
            page_table.reshape(-1), q_s, knew, cache_ckv, cache_krope_t)


def _ctx_to_attn_out(ctx, wuv, H, KL):
    T = ctx.shape[0]
    tm = _tile(T, LINEAR_TM)

    def kern(c_ref, w_ref, o_ref):
        for h in range(H):
            o_ref[:, h * V_DIM:(h + 1) * V_DIM] = _dot(
                c_ref[:, h * KL:(h + 1) * KL], w_ref[h]).astype(BF16)

    return pl.pallas_call(
        kern, grid=(T // tm,),
        in_specs=[pl.BlockSpec((tm, H * KL), lambda i: (i, 0)),
                  pl.BlockSpec(wuv.shape, lambda i: (0, 0, 0))],
        out_specs=pl.BlockSpec((tm, H * V_DIM), lambda i: (i, 0)),
        out_shape=jax.ShapeDtypeStruct((T, H * V_DIM), BF16),
        compiler_params=_cparams(40), name="ctx_to_attn_out")(ctx, wuv)


def _linear_residual(xs, ws, res, name):
    T, D = res.shape
    tm = _tile(T, LINEAR_TM)

    def epilogue(acc, x0, r_refs, c_refs, o_refs):
        o_refs[0][...] = r_refs[0][...] + acc

    (out,) = _fused_linear(xs, ws, tm=tm, epilogue=epilogue,
                           out_shapes=(jax.ShapeDtypeStruct((T, D), F32),),
                           out_specs=(pl.BlockSpec((tm, D), lambda i: (i, 0)),),
                           row_extras=(res,), name=name)
    return out


def _ffn(h, gain, w_gu, w_down):
    T, D = h.shape
    F = w_down.shape[0]
    tm = _tile(T, FFN_TM)
    tf = _tile(F, FFN_TF, LANES)
    nf = F // tf

    def kern(x_ref, g_ref, wg_ref, wu_ref, wd_ref, o_ref, xn_sc):
        f = pl.program_id(1)

        @pl.when(f == 0)
        def _():
            x = x_ref[...]
            xn_sc[...] = _rms(x, g_ref[...]).astype(BF16)
            o_ref[...] = x

        xn = xn_sc[...]
        act = (_silu(_dot(xn, wg_ref[...])) * _dot(xn, wu_ref[...])).astype(BF16)
        o_ref[...] += _dot(act, wd_ref[...])

    return pl.pallas_call(
        kern, grid=(T // tm, nf),
        in_specs=[pl.BlockSpec((tm, D), lambda i, f: (i, 0)),
                  pl.BlockSpec(gain.shape, lambda i, f: (0, 0)),
                  pl.BlockSpec((D, tf), lambda i, f: (0, f)),
                  pl.BlockSpec((D, tf), lambda i, f: (0, nf + f)),
                  pl.BlockSpec((tf, D), lambda i, f: (f, 0))],
        out_specs=pl.BlockSpec((tm, D), lambda i, f: (i, 0)),
        out_shape=jax.ShapeDtypeStruct((T, D), F32),
        scratch_shapes=[pltpu.VMEM((tm, D), BF16)],
        compiler_params=_cparams(48), name="ffn")(h, gain, w_gu, w_gu, w_down)


def _hgrn_in_proj(h, gain, w_in, lb):
    T, D = h.shape
    HK = w_in.shape[1] // 4
    tm = _tile(T, LINEAR_TM)

    def call(col, epilogue, dtypes, extras=()):
        out_shapes = tuple(jax.ShapeDtypeStruct((T, HK), dt) for dt in dtypes)
        out_specs = tuple(pl.BlockSpec((tm, HK), lambda i: (i, 0)) for _ in dtypes)
        return _fused_linear([h], [w_in], tm=tm, gain=gain, epilogue=epilogue,
                             wspecs=[pl.BlockSpec((D, HK), lambda i: (0, col))],
                             out_shapes=out_shapes, out_specs=out_specs, const_extras=extras,
                             name=f"hgrn_in_proj_{col}")

    def ep_silu(acc, x0, r_refs, c_refs, o_refs):
        o_refs[0][...] = _silu(acc).astype(BF16)

    def ep_ident(acc, x0, r_refs, c_refs, o_refs):
        o_refs[0][...] = acc.astype(BF16)

    def ep_gate(acc, x0, r_refs, c_refs, o_refs):
        k = (1.0 - c_refs[0][...]) / (1.0 + jnp.exp(acc))
        o_refs[0][...] = k.astype(BF16)
        o_refs[1][...] = jnp.log1p(-k)

    (qs,) = call(0, ep_silu, (BF16,))
    kk, logf = call(1, ep_gate, (BF16, F32), extras=(lb,))
    (vv,) = call(2, ep_ident, (BF16,))
    (gs,) = call(3, ep_silu, (BF16,))
    return qs, kk, logf, vv, gs


def _gla_band(qf, kf, vf, fg, n_diag, lane_sum):
    C = qf.shape[0]
    r8 = lax.broadcasted_iota(jnp.int32, qf.shape, 0) % SUBLANES
    gprod = None
    parts = []
    for d in range(n_diag):
        if d == 0:
            p = qf * kf
        else:
            fr = fg if d == 1 else pltpu.roll(fg, d - 1, 0)
            gprod = fr if gprod is None else gprod * fr
            p = jnp.where(r8 >= d, qf * gprod * pltpu.roll(kf, d, 0), 0.0)
        parts.append(p)
    sums = lane_sum(parts)
    out = None
    for d in range(n_diag):
        vr = vf if d == 0 else pltpu.roll(vf, d, 0)
        term = sums[d] * vr
        out = term if out is None else out + term
    return out


def _gla_prompt(qs, kk, logf, vv, gs, hg_norm, B, L, H):
    T, HK = qs.shape
    K = HK // H
    ct = _tile(L, GLA_CT, GLA_CHUNK)
    C = min(GLA_CHUNK, ct)
    nct = L // ct
    ncc = ct // C
    HPS = GLA_HEADS_PER_STEP if H % GLA_HEADS_PER_STEP == 0 else 1

    def kern(q_ref, k_ref, lf_ref, v_ref, g_ref, hn_ref, o_ref, s_ref, st_sc):
        ci = pl.program_id(2)

        @pl.when(ci == 0)
        def _():
            st_sc[...] = jnp.zeros((HPS, K, K), F32)

        rows = lax.broadcasted_iota(jnp.int32, (C, C), 0)
        cols = lax.broadcasted_iota(jnp.int32, (C, C), 1)
        tril = (rows >= cols).astype(F32)
        ones_bf = jnp.ones((K, K), BF16)

        def lane_sum(parts):
            r = _dot(jnp.concatenate([p.astype(BF16) for p in parts], axis=0), ones_bf)
            return [r[d * C:(d + 1) * C] for d in range(len(parts))]

        def chunk(c, hh, st):
            sl, hl = slice(c * C, (c + 1) * C), slice(hh * K, (hh + 1) * K)
            qf, kf, vf = q_ref[sl, hl].astype(F32), k_ref[sl, hl].astype(F32), v_ref[sl, hl].astype(F32)
            lf = lf_ref[sl, hl]
            fg = jnp.exp(lf)
            b = jnp.dot(tril, lf, precision=lax.Precision.HIGHEST, preferred_element_type=F32)
            blast = b[C - 1:C, :]
            qb = qf * jnp.exp(b)
            kb = kf * jnp.exp(blast - b)
            o = lax.dot_general(qb.astype(BF16), st.astype(BF16), NT_DIMS, preferred_element_type=F32)
            amat = None
            blk = C // 2
            while blk >= SUBLANES:
                q_parts, k_parts = [], []
                for m in range(C // blk):
                    r = slice(m * blk, (m + 1) * blk)
                    if m % 2 == 1:
                        ref = b[m * blk - 1:m * blk, :]
                        q_parts.append(qf[r] * jnp.exp(b[r] - ref))
                        k_parts.append(jnp.zeros((blk, K), F32))
                    else:
                        ref = b[(m + 1) * blk - 1:(m + 1) * blk, :]
                        q_parts.append(jnp.zeros((blk, K), F32))
                        k_parts.append(kf[r] * jnp.exp(ref - b[r]))
                ql = jnp.concatenate(q_parts, axis=0).astype(BF16)
                kl = jnp.concatenate(k_parts, axis=0).astype(BF16)
                al = lax.dot_general(ql, kl, NT_DIMS, preferred_element_type=F32)
                al = jnp.where((rows // (2 * blk)) == (cols // (2 * blk)), al, 0.0)
                amat = al if amat is None else amat + al
                blk //= 2
            if amat is not None:
                o = o + _dot(amat.astype(BF16), vf.astype(BF16))
            o = o + _gla_band(qf, kf, vf, fg, min(SUBLANES, C), lane_sum)
            st = st * jnp.exp(blast) + lax.dot_general(
                vf.astype(BF16), kb.astype(BF16), TN_DIMS, preferred_element_type=F32)
            on = _rms(o, hn_ref[...]) * g_ref[sl, hl].astype(F32)
            o_ref[sl, hl] = on.astype(BF16)
            return st

        sts = [st_sc[hh] for hh in range(HPS)]
        for c in range(ncc):
            sts = [chunk(c, hh, sts[hh]) for hh in range(HPS)]
        for hh in range(HPS):
            st_sc[hh] = sts[hh]

        @pl.when(ci == nct - 1)
        def _():
            for hh in range(HPS):
                s_ref[0, hh] = sts[hh].T

    tok = lambda b, h, ci: (b * nct + ci, h)
    return pl.pallas_call(
        kern, grid=(B, H // HPS, nct),
        in_specs=[pl.BlockSpec((ct, HPS * K), tok)] * 5 + [pl.BlockSpec(hg_norm.shape, lambda b, h, ci: (0, 0))],
        out_specs=(pl.BlockSpec((ct, HPS * K), tok),
                   pl.BlockSpec((1, HPS, K, K), lambda b, h, ci: (b, h, 0, 0))),
        out_shape=(jax.ShapeDtypeStruct((T, HK), BF16), jax.ShapeDtypeStruct((B, H, K, K), F32)),
        scratch_shapes=[pltpu.VMEM((HPS, K, K), F32)],
        compiler_params=_cparams(32), name="gla_prompt")(qs, kk, logf, vv, gs, hg_norm)


def _gla_sample(qs, kk, logf, vv, gs, hg_norm, s0, DB, Ld, H):
    HK = qs.shape[2]
    K = HK // H
    C = SUBLANES
    assert Ld <= C

    def kern(q_ref, k_ref, lf_ref, v_ref, g_ref, hn_ref, s0_ref, o_ref, s_ref, pad_sc):
        rows = lax.broadcasted_iota(jnp.int32, (C, C), 0)
        cols = lax.broadcasted_iota(jnp.int32, (C, C), 1)
        tril = (rows >= cols).astype(F32)

        def padded(ref, slot):
            pad_sc[slot] = jnp.zeros((C, HK), F32)
            pad_sc[slot, 0:Ld, :] = ref[0].astype(F32)
            return pad_sc[slot]

        qf, kf, lf, vf = padded(q_ref, 0), padded(k_ref, 1), padded(lf_ref, 2), padded(v_ref, 3)
        fg = jnp.exp(lf)
        b = jnp.dot(tril, lf, precision=lax.Precision.HIGHEST, preferred_element_type=F32)
        blast = b[C - 1:C, :]
        qb = (qf * jnp.exp(b)).astype(BF16)
        kb = (kf * jnp.exp(blast - b)).astype(BF16)
        dec = jnp.exp(blast)
        vb = vf.astype(BF16)

        def head_sum(parts):
            return [jnp.sum(p, axis=-1, keepdims=True) for p in parts]

        eye = (lax.broadcasted_iota(jnp.int32, (K, K), 0) == lax.broadcasted_iota(jnp.int32, (K, K), 1))
        for h in range(H):
            hl = slice(h * K, (h + 1) * K)
            s0 = s0_ref[0, h]
            o = _dot(qb[:, hl], s0.astype(BF16))
            o = o + _gla_band(qf[:, hl], kf[:, hl], vf[:, hl], fg[:, hl], Ld, head_sum)
            dcol = jnp.sum(jnp.where(eye, jnp.broadcast_to(dec[:, hl], (K, K)), 0.0), axis=-1, keepdims=True)
            s_ref[0, h] = s0 * dcol + lax.dot_general(kb[:, hl], vb[:, hl], TN_DIMS,
                                                      preferred_element_type=F32)
            on = _rms(o[0:Ld], hn_ref[...]) * g_ref[0, :, hl].astype(F32)
            o_ref[0, :, hl] = on.astype(BF16)

    tok = pl.BlockSpec((1, Ld, HK), lambda b: (b, 0, 0))
    st_spec = pl.BlockSpec((1, H, K, K), lambda b: (b, 0, 0, 0))
    return pl.pallas_call(
        kern, grid=(DB,),
        in_specs=[tok] * 5 + [pl.BlockSpec(hg_norm.shape, lambda b: (0, 0)), st_spec],
        out_specs=(tok, st_spec),
        out_shape=(jax.ShapeDtypeStruct((DB, Ld, HK), BF16), jax.ShapeDtypeStruct(s0.shape, F32)),
        scratch_shapes=[pltpu.VMEM((4, C, HK), F32)],
        compiler_params=_cparams(32), name="gla_sample")(qs, kk, logf, vv, gs, hg_norm, s0)


def _router(h, gain, w_router, b_router):
    T, D = h.shape
    E = w_router.shape[1]
    tm = _tile(T, LINEAR_TM)
    assert TOP_K == 2

    def kern(x_ref, g_ref, w_ref, b_ref, i_ref, o_ref):
        xn = _rms(x_ref[...], g_ref[...])
        logits = jnp.dot(xn, w_ref[...], precision=lax.Precision.HIGHEST,
                         preferred_element_type=F32) + b_ref[...]
        idx = lax.broadcasted_iota(jnp.int32, logits.shape, 1)
        m1 = jnp.max(logits, axis=-1, keepdims=True)
        i1 = jnp.min(jnp.where(logits == m1, idx, E), axis=-1, keepdims=True)
        rest = jnp.where(idx == i1, -jnp.inf, logits)
        m2 = jnp.max(rest, axis=-1, keepdims=True)
        i2 = jnp.min(jnp.where(rest == m2, idx, E), axis=-1, keepdims=True)
        e2 = jnp.exp(m2 - m1)
        g1 = 1.0 / (1.0 + e2)
        slot = lax.broadcasted_iota(jnp.int32, (tm, TOP_K), 1)
        i_ref[...] = jnp.where(slot == 0, i1, i2)
        o_ref[...] = jnp.where(slot == 0, g1, e2 * g1)

    return pl.pallas_call(
        kern, grid=(T // tm,),
        in_specs=[pl.BlockSpec((tm, D), lambda i: (i, 0)),
                  pl.BlockSpec(gain.shape, lambda i: (0, 0)),
                  pl.BlockSpec(w_router.shape, lambda i: (0, 0)),
                  pl.BlockSpec(b_router.shape, lambda i: (0, 0))],
        out_specs=(pl.BlockSpec((tm, TOP_K), lambda i: (i, 0)), pl.BlockSpec((tm, TOP_K), lambda i: (i, 0))),
        out_shape=(jax.ShapeDtypeStruct((T, TOP_K), jnp.int32), jax.ShapeDtypeStruct((T, TOP_K), F32)),
        compiler_params=_cparams(32), name="router")(h, gain, w_router, b_router)


def _route_plan(idx, E, tmx):
    T = idx.shape[0]
    NP = TOP_K * T + E * tmx
    assert NP % tmx == 0
    NT = NP // tmx
    sel = jnp.any(idx[:, :, None] == jnp.arange(E, dtype=jnp.int32), axis=1).astype(jnp.int32)
    incl = jnp.cumsum(sel, axis=0)
    counts = incl[-1]
    padded = ((counts + tmx - 1) // tmx) * tmx
    ends = jnp.cumsum(padded)
    offs = ends - padded
    rank = jnp.take_along_axis(incl - sel, idx, axis=1)
    pos = (offs[idx] + rank).reshape(-1).astype(jnp.int32)
    tok = jnp.repeat(jnp.arange(T, dtype=jnp.int32), TOP_K)
    src = jnp.zeros((NP,), jnp.int32).at[pos].set(tok)
    n_used = (ends[-1] // tmx).astype(jnp.int32)
    tile_start = jnp.minimum(jnp.arange(NT, dtype=jnp.int32), n_used - 1) * tmx
    tile_expert = jnp.minimum(jnp.sum(tile_start[:, None] >= ends[None, :], axis=1), E - 1).astype(jnp.int32)
    return pos, src, tile_expert, n_used.reshape(1)


def _moe_experts(h, gain, src, tile_expert, n_used, w_gu, w_down, tmx):
    T, D = h.shape
    E, F = w_down.shape[0], w_down.shape[1]
    NP = src.shape[0]
    NT = NP // tmx
    tf = _tile(F, MOE_TF, LANES)
    nf = F // tf

    def kern(src_ref, te_ref, nu_ref, h_ref, g_ref, wg_ref, wu_ref, wd_ref, o_ref, xbuf, xn_sc, sem):
        i, f = pl.program_id(0), pl.program_id(1)
        n_used_v = nu_ref[0]

        def row_copy(tile, r):
            return pltpu.make_async_copy(h_ref.at[pl.ds(src_ref[tile * tmx + r], 1)],
                                         xbuf.at[pl.ds(r, 1)], sem)

        def start_gather(tile):
            lax.fori_loop(0, tmx, lambda r, c: (row_copy(tile, r).start(), c)[1], 0, unroll=DMA_LOOP_UNROLL)

        @pl.when((f == 0) & (i == 0))
        def _():
            start_gather(0)

        @pl.when((f == 0) & (i < n_used_v))
        def _():
            lax.fori_loop(0, tmx, lambda r, c: (row_copy(i, r).wait(), c)[1], 0, unroll=DMA_LOOP_UNROLL)
            xn_sc[...] = _rms(xbuf[...], g_ref[...]).astype(BF16)

            @pl.when(i + 1 < n_used_v)
            def _():
                start_gather(i + 1)

        @pl.when(f == 0)
        def _():
            o_ref[...] = jnp.zeros((tmx, D), F32)

        @pl.when(i < n_used_v)
        def _():
            xn = xn_sc[...]
            act = (_silu(_dot(xn, wg_ref[0])) * _dot(xn, wu_ref[0])).astype(BF16)
            o_ref[...] += _dot(act, wd_ref[0])

    def wspec(shape, fn):
        def index_map(i, f, src_r, te_r, nu_r):
            return fn(te_r[i], jnp.where(i < nu_r[0], f, nf - 1))
        return pl.BlockSpec(shape, index_map)

    grid_spec = pltpu.PrefetchScalarGridSpec(
        num_scalar_prefetch=3, grid=(NT, nf),
        in_specs=[pl.BlockSpec(memory_space=pl.ANY),
                  pl.BlockSpec(gain.shape, lambda i, f, *_: (0, 0)),
                  wspec((1, D, tf), lambda e, f: (e, 0, f)),
                  wspec((1, D, tf), lambda e, f: (e, 0, nf + f)),
                  wspec((1, tf, D), lambda e, f: (e, f, 0))],
        out_specs=pl.BlockSpec((tmx, D), lambda i, f, *_: (i, 0)),
        scratch_shapes=[pltpu.VMEM((tmx, D), F32), pltpu.VMEM((tmx, D), BF16),
                        pltpu.SemaphoreType.DMA(())])
    return pl.pallas_call(
        kern, grid_spec=grid_spec, out_shape=jax.ShapeDtypeStruct((NP, D), F32),
        compiler_params=_cparams(48), name="moe_experts")(
            src, tile_expert, n_used, h, gain, w_gu, w_gu, w_down)


def _moe_combine(h, y_sorted, pos, gate, final_gain):
    T, D = h.shape
    tm = _tile(T, MOE_COMBINE_TM)
    nt = T // tm

    def kern(pos_ref, h_ref, y_ref, gt_ref, fg_ref, o_ref, ybuf, sems):
        i = pl.program_id(0)

        def row_copy(step, r, s):
            slot = step % 2
            return pltpu.make_async_copy(y_ref.at[pl.ds(pos_ref[(step * tm + r) * TOP_K + s], 1)],
                                         ybuf.at[slot, s, pl.ds(r, 1)], sems.at[slot])

        def for_rows(step, fn):
            def body(r, c):
                for s in range(TOP_K):
                    fn(row_copy(step, r, s))
                return c
            lax.fori_loop(0, tm, body, 0, unroll=DMA_LOOP_UNROLL)

        @pl.when(i == 0)
        def _():
            for_rows(0, lambda cp: cp.start())

        @pl.when(i + 1 < nt)
        def _():
            for_rows(i + 1, lambda cp: cp.start())

        for_rows(i, lambda cp: cp.wait())
        slot = i % 2
        gt = gt_ref[...]
        y = h_ref[...]
        for s in range(TOP_K):
            y = y + gt[:, s:s + 1] * ybuf[slot, s]
        o_ref[...] = _rms(y, fg_ref[...])

    grid_spec = pltpu.PrefetchScalarGridSpec(
        num_scalar_prefetch=1, grid=(nt,),
        in_specs=[pl.BlockSpec((tm, D), lambda i, *_: (i, 0)),
                  pl.BlockSpec(memory_space=pl.ANY),
                  pl.BlockSpec((tm, TOP_K), lambda i, *_: (i, 0)),
                  pl.BlockSpec(final_gain.shape, lambda i, *_: (0, 0))],
        out_specs=pl.BlockSpec((tm, D), lambda i, *_: (i, 0)),
        scratch_shapes=[pltpu.VMEM((2, TOP_K, tm, D), F32), pltpu.SemaphoreType.DMA((2,))])
    return pl.pallas_call(
        kern, grid_spec=grid_spec, out_shape=jax.ShapeDtypeStruct((T, D), F32),
        compiler_params=_cparams(40), name="moe_combine")(pos, h, y_sorted, gate, final_gain)


def _moe(h, gain, w_router, b_router, w_gu, w_down, final_gain):
    T = h.shape[0]
    E = w_down.shape[0]
    tmx = _tile(max(MOE_MIN_TILE, 2 * TOP_K * T // E), MOE_TM)
    idx, gate = _router(h, gain, w_router, b_router)
    pos, src, tile_expert, n_used = _route_plan(idx, E, tmx)
    y_sorted = _moe_experts(h, gain, src, tile_expert, n_used, w_gu, w_down, tmx)
    return _moe_combine(h, y_sorted, pos, gate, final_gain)


def _rope_rows(pos, rope_dim):
    inv = ROPE_THETA ** (-jnp.arange(0, rope_dim, 2, dtype=F32) / rope_dim)
    ang = pos.astype(F32)[:, None] * inv[None, :]
    z = jnp.zeros((pos.shape[0], LANES - rope_dim), F32)
    cos, sin = jnp.cos(ang), jnp.sin(ang)
    return jnp.concatenate([cos, cos, z], axis=-1), jnp.concatenate([sin, sin, z], axis=-1)


def _rot_cols(w):
    half = w.shape[-1] // 2
    return jnp.concatenate([-w[..., half:], w[..., :half]], axis=-1)


def _pad_lanes(w):
    pad = LANES - w.shape[-1]
    return jnp.pad(w, [(0, 0)] * (w.ndim - 1) + [(0, pad)])


def _prep_even(a, w_in_e, w_q_b, w_kv_b, pool_w, w_out_e, w_ffn_gu, w_ffn_down, dims, H):
    DP, QL, KL, ROPE = dims
    w_in = w_in_e[a]
    k_raw = w_in[:, DP + QL + KL:]
    w_ext = jnp.concatenate([w_in[:, :DP + QL + KL], _pad_lanes(k_raw), _pad_lanes(_rot_cols(k_raw))],
                            axis=-1).astype(BF16)
    wq = w_q_b[a].reshape(QL, H, NOPE_DIM + ROPE)
    wq_rope = wq[..., NOPE_DIM:]
    wq_ext = jnp.concatenate([wq[..., :NOPE_DIM].reshape(QL, H * NOPE_DIM),
                              _pad_lanes(wq_rope).reshape(QL, H * LANES),
                              _pad_lanes(_rot_cols(wq_rope)).reshape(QL, H * LANES)], axis=-1).astype(BF16)
    w_kv = w_kv_b[a].reshape(KL, H, NOPE_DIM + V_DIM)
    wuk_t = jnp.transpose(w_kv[..., :NOPE_DIM], (1, 2, 0)).astype(BF16)
    wuv = jnp.transpose(w_kv[..., NOPE_DIM:], (1, 0, 2)).astype(BF16)
    w_out = w_out_e[a].astype(BF16)
    return dict(w_ext=w_ext, wq_ext=wq_ext, wuk_t=wuk_t, wuv=wuv, pool_w=pool_w[a].astype(BF16),
                w_out_pool=w_out[:DP], w_out_attn=w_out[DP:],
                w_gu=w_ffn_gu[a].astype(BF16), w_down=w_ffn_down[a].astype(BF16))


def kernel(x_prompt, x_sample, cache_ckv, cache_krope, page_table, state_pool, state_hgrn,
           norm_mix_e, w_in_e, q_norm, w_q_b, kv_norm, w_kv_b, pool_w, pool_scale, w_out_e,
           norm_ffn_e, w_ffn_gu, w_ffn_down,
           norm_mix_o, w_in_o, hg_lower_bound, hg_norm, w_out_o, norm_ffn_o, w_router, b_router,
           w_exp_gu, w_exp_down, final_norm):
    B, L, D = x_prompt.shape
    DB, Ld, _ = x_sample.shape
    n_pages = page_table.shape[1]
    PG = cache_ckv.shape[2]
    past_len = n_pages * PG
    KL, ROPE = cache_ckv.shape[3], cache_krope.shape[3]
    DP, PS = state_pool.shape[3], state_pool.shape[2]
    QL = q_norm.shape[1]
    H = w_q_b.shape[2] // (NOPE_DIM + ROPE)
    HG = state_hgrn.shape[2]
    depth = hg_lower_bound.shape[0]
    dims = (DP, QL, KL, ROPE)
    scale = float((NOPE_DIM + ROPE) ** -0.5 * math.log2(math.e))
    cache_krope_t = jnp.swapaxes(cache_krope, 2, 3)
    QW = KL + LANES
    KN = 16
    assert Ld <= KN and ROPE <= LANES

    row = lambda v: v.reshape(1, -1).astype(F32)
    lb_p = jax.nn.softmax(hg_lower_bound.astype(F32), axis=0)
    lower_bounds = jnp.cumsum(lb_p, axis=0) - lb_p[0]

    hp = x_prompt.reshape(B * L, D)
    hs = x_sample.reshape(DB * Ld, D)
    cos_p, sin_p = _rope_rows(jnp.tile(jnp.arange(L), B), ROPE)
    cos_s, sin_s = _rope_rows(jnp.tile(past_len + jnp.arange(Ld), DB), ROPE)

    outs_p = dict(ckv=[], krope=[], pool=[], hgrn=[])
    outs_s = dict(ckv=[], krope=[], pool=[], hgrn=[])
    for l in range(depth):
        a = l // 2
        if l % 2 == 0:
            w = _prep_even(a, w_in_e, w_q_b, w_kv_b, pool_w, w_out_e, w_ffn_gu, w_ffn_down, dims, H)
            g_mix, g_q, g_kv = row(norm_mix_e[a]), row(q_norm[a]), row(kv_norm[a])
            g_ffn, p_scale = row(norm_ffn_e[a]), row(pool_scale[a])

            u, qn, ckv, kpe, kcat, ckv_t = _even_in_proj(hp, g_mix, w["w_ext"], g_q, g_kv, cos_p, sin_p,
                                                         dims, True)
            pool_out, pool_new = _pool_prompt(u, jnp.zeros((B, PS, DP), F32), w["pool_w"], p_scale, B, L)
            qcat = _q_proj(qn, w["wq_ext"], w["wuk_t"], cos_p, sin_p, H, KL, scale, head_major=True)
            attn = _flash_prompt(qcat, kcat, ckv_t, w["wuv"], B, L, H, KL)
            hp = _linear_residual([pool_out, attn], [w["w_out_pool"], w["w_out_attn"]], hp, "out_proj_e")
            hp = _ffn(hp, g_ffn, w["w_gu"], w["w_down"])
            outs_p["ckv"].append(ckv.reshape(B, L, KL))
            outs_p["krope"].append(kpe.reshape(B, L, ROPE))
            outs_p["pool"].append(pool_new)

            u, qn, ckv, kpe, kcat = _even_in_proj(hs, g_mix, w["w_ext"], g_q, g_kv, cos_s, sin_s, dims, False)
            u3 = u.reshape(DB, Ld, DP)
            ext_tm = jnp.transpose(jnp.concatenate([state_pool[a], u3], axis=1), (1, 0, 2))
            pool_tm = _pool_sample(ext_tm, w["pool_w"], p_scale, Ld, past_len)
            pool_out = jnp.transpose(pool_tm, (1, 0, 2)).reshape(DB * Ld, DP)
            qrows = _q_proj(qn, w["wq_ext"], w["wuk_t"], cos_s, sin_s, H, KL, scale, head_major=False)
            q_s = qrows.reshape(DB, Ld * H, QW)
            knew = jnp.pad(kcat.reshape(DB, Ld, QW), ((0, 0), (0, KN - Ld), (0, 0)))
            ctx = _decode_attention(page_table, q_s, knew, cache_ckv, cache_krope_t, a, H, Ld, KL, ROPE)
            attn = _ctx_to_attn_out(ctx.reshape(DB * Ld, H * KL), w["wuv"], H, KL)
            hs = _linear_residual([pool_out, attn], [w["w_out_pool"], w["w_out_attn"]], hs, "out_proj_e")
            hs = _ffn(hs, g_ffn, w["w_gu"], w["w_down"])
            outs_s["ckv"].append(ckv.reshape(DB, Ld, KL))
            outs_s["krope"].append(kpe.reshape(DB, Ld, ROPE))
            outs_s["pool"].append(jnp.concatenate([state_pool[a], u3], axis=1)[:, -PS:])
        else:
            w_in = w_in_o[a].astype(BF16)
            w_out = w_out_o[a].astype(BF16)
            w_gu = w_exp_gu[a].astype(BF16)
            w_dn = w_exp_down[a].astype(BF16)
            g_mix, g_ffn, g_hn = row(norm_mix_o[a]), row(norm_ffn_o[a]), row(hg_norm[a])
            lb = row(lower_bounds[l])
            w_r, b_r = w_router[a].astype(F32), row(b_router[a])
            g_fin = row(final_norm) if l == depth - 1 else None
            assert g_fin is not None

            qs, kk, logf, vv, gs = _hgrn_in_proj(hp, g_mix, w_in, lb)
            on, s_new = _gla_prompt(qs, kk, logf, vv, gs, g_hn, B, L, HG)
            hp = _linear_residual([on], [w_out], hp, "out_proj_o")
            hp = _moe(hp, g_ffn, w_r, b_r, w_gu, w_dn, g_fin)
            outs_p["hgrn"].append(s_new)

            qs, kk, logf, vv, gs = _hgrn_in_proj(hs, g_mix, w_in, lb)
            r3 = lambda t: t.reshape(DB, Ld, -1)
            on, s_new = _gla_sample(r3(qs), r3(kk), r3(logf), r3(vv), r3(gs), g_hn, state_hgrn[a], DB, Ld, HG)
            hs = _linear_residual([on.reshape(DB * Ld, -1)], [w_out], hs, "out_proj_o")
            hs = _moe(hs, g_ffn, w_r, b_r, w_gu, w_dn, g_fin)
            outs_s["hgrn"].append(s_new)

    return (hp.reshape(B, L, D), hs.reshape(DB, Ld, D),
            jnp.stack(outs_p["ckv"]), jnp.stack(outs_p["krope"]), jnp.stack(outs_p["pool"]),
            jnp.stack(outs_p["hgrn"]),
            jnp.stack(outs_s["ckv"]), jnp.stack(outs_s["krope"]), jnp.stack(outs_s["pool"]),
            jnp.stack(outs_s["hgrn"]))
```

```python
import functools
import math

import jax
import jax.numpy as jnp
from jax import lax
from jax.experimental import pallas as pl
from jax.experimental.pallas import tpu as pltpu

F32 = jnp.float32
BF16 = jnp.bfloat16

EPS = 1e-6
POOL_WINDOWS = (2, 4, 8, 16)
NOPE_DIM = 128
V_DIM = 128
ROPE_THETA = 10000.0
TOP_K = 2
LANES = 128
SUBLANES = 8
NEG_BIG = -1e30

LINEAR_TM = 512
FFN_TM = 512
FFN_TF = 512
MOE_TM = 512
MOE_MIN_TILE = 16
MOE_TF = 256
MOE_COMBINE_TM = 256
MOE_GATHER_ISSUE_STEPS = 8
DMA_LOOP_UNROLL = 8
FLASH_TQ = 128
FLASH_TK = 512
POOL_TL = 512
GLA_CT = 256
GLA_CHUNK = 256
GLA_HEADS_PER_STEP = 4
DECODE_PAGES_PER_STEP = 16
DECODE_SEQS_PER_STEP = 2

NT_DIMS = (((1,), (1,)), ((), ()))
TN_DIMS = (((0,), (0,)), ((), ()))


def _cparams(vmem_mb):
    return pltpu.CompilerParams(vmem_limit_bytes=vmem_mb * 2 ** 20)


def _tile(n, pref, mult=SUBLANES):
    if n <= pref:
        return n
    for t in range(pref, 0, -1):
        if n % t == 0 and t % mult == 0:
            return t
    return n


def _rms(x, g):
    return x * lax.rsqrt(jnp.mean(x * x, axis=-1, keepdims=True) + EPS) * g


def _silu(x):
    return x / (1.0 + jnp.exp(-x))


def _dot(a, b):
    return jnp.dot(a, b, preferred_element_type=F32)


def _fused_linear(xs, ws, *, tm, epilogue, out_shapes, out_specs, gain=None, wspecs=None,
                  row_extras=(), const_extras=(), vmem_mb=48, name="linear"):
    T = xs[0].shape[0]
    grid = (T // tm,)
    n_x, n_w, n_r, n_c = len(xs), len(ws), len(row_extras), len(const_extras)

    def kern(*refs):
        pos = 0
        x_refs = refs[pos:pos + n_x]; pos += n_x
        gain_ref = None
        if gain is not None:
            gain_ref = refs[pos]; pos += 1
        w_refs = refs[pos:pos + n_w]; pos += n_w
        r_refs = refs[pos:pos + n_r]; pos += n_r
        c_refs = refs[pos:pos + n_c]; pos += n_c
        o_refs = refs[pos:]
        acc = None
        x0 = None
        for xr, wr in zip(x_refs, w_refs):
            xv = xr[...]
            if gain_ref is not None:
                x0 = xv
                xv = _rms(xv, gain_ref[...]).astype(BF16)
            d = _dot(xv, wr[...])
            acc = d if acc is None else acc + d
        epilogue(acc, x0, r_refs, c_refs, o_refs)

    in_specs = [pl.BlockSpec((tm, x.shape[1]), lambda i: (i, 0)) for x in xs]
    args = list(xs)
    if gain is not None:
        in_specs.append(pl.BlockSpec(gain.shape, lambda i: (0, 0)))
        args.append(gain)
    if wspecs is None:
        wspecs = [pl.BlockSpec(w.shape, lambda i: (0,) * w.ndim) for w in ws]
    in_specs += list(wspecs)
    args += list(ws)
    for r in row_extras:
        in_specs.append(pl.BlockSpec((tm, r.shape[1]), lambda i: (i, 0)))
        args.append(r)
    for c in const_extras:
        in_specs.append(pl.BlockSpec(c.shape, lambda i, nd=c.ndim: (0,) * nd))
        args.append(c)
    return pl.pallas_call(
        kern, grid=grid, in_specs=in_specs, out_specs=out_specs, out_shape=out_shapes,
        compiler_params=_cparams(vmem_mb), name=name)(*args)


def _even_in_proj(h, gain, w_ext, qn_g, kvn_g, cos_rows, sin_rows, dims, with_ckv_t):
    DP, QL, KL, ROPE = dims
    T, D = h.shape
    tm = _tile(T, LINEAR_TM)
    o_kr = DP + QL + KL

    def epilogue(acc, x0, r_refs, c_refs, o_refs):
        u_ref, qn_ref, ckv_ref, kpe_ref, kcat_ref = o_refs[:5]
        cos, sin = r_refs[0][...], r_refs[1][...]
        u_ref[...] = acc[:, :DP]
        qn_ref[...] = _rms(acc[:, DP:DP + QL], c_refs[0][...]).astype(BF16)
        ckv = _rms(acc[:, DP + QL:o_kr], c_refs[1][...])
        ckv_ref[...] = ckv
        kpe = acc[:, o_kr:o_kr + LANES] * cos + acc[:, o_kr + LANES:o_kr + 2 * LANES] * sin
        kpe_ref[...] = kpe[:, :ROPE]
        kcat_ref[:, :KL] = ckv.astype(BF16)
        kcat_ref[:, KL:] = kpe.astype(BF16)
        if with_ckv_t:
            o_refs[5][...] = ckv.T.astype(BF16)

    out_shapes = (jax.ShapeDtypeStruct((T, DP), F32), jax.ShapeDtypeStruct((T, QL), BF16),
                  jax.ShapeDtypeStruct((T, KL), F32), jax.ShapeDtypeStruct((T, ROPE), F32),
                  jax.ShapeDtypeStruct((T, KL + LANES), BF16))
    out_specs = tuple(pl.BlockSpec((tm, s.shape[1]), lambda i: (i, 0)) for s in out_shapes)
    if with_ckv_t:
        out_shapes += (jax.ShapeDtypeStruct((KL, T), BF16),)
        out_specs += (pl.BlockSpec((KL, tm), lambda i: (0, i)),)
    return _fused_linear([h], [w_ext], tm=tm, gain=gain, epilogue=epilogue,
                         out_shapes=out_shapes, out_specs=out_specs,
                         row_extras=(cos_rows, sin_rows), const_extras=(qn_g, kvn_g),
                         name="even_in_proj")


def _q_proj(qn, wq_ext, wuk_t, cos_rows, sin_rows, H, KL, scale, head_major):
    T = qn.shape[0]
    tm = _tile(T, LINEAR_TM)
    QW = KL + LANES
    r0, r1 = H * NOPE_DIM, 2 * H * NOPE_DIM

    def epilogue(acc, x0, r_refs, c_refs, o_refs):
        (o_ref,) = o_refs
        cos, sin = r_refs[0][...], r_refs[1][...]
        wuk_ref = c_refs[0]
        for h in range(H):
            qn_h = acc[:, h * NOPE_DIM:(h + 1) * NOPE_DIM].astype(BF16)
            q_abs = (_dot(qn_h, wuk_ref[h]) * scale).astype(BF16)
            q_pe = ((acc[:, r0 + h * LANES:r0 + (h + 1) * LANES] * cos
                     + acc[:, r1 + h * LANES:r1 + (h + 1) * LANES] * sin) * scale).astype(BF16)
            if head_major:
                o_ref[h, :, :KL] = q_abs
                o_ref[h, :, KL:] = q_pe
            else:
                o_ref[:, h * QW:h * QW + KL] = q_abs
                o_ref[:, h * QW + KL:(h + 1) * QW] = q_pe

    if head_major:
        out_shape = jax.ShapeDtypeStruct((H, T, QW), BF16)
        out_spec = pl.BlockSpec((H, tm, QW), lambda i: (0, i, 0))
    else:
        out_shape = jax.ShapeDtypeStruct((T, H * QW), BF16)
        out_spec = pl.BlockSpec((tm, H * QW), lambda i: (i, 0))
    (out,) = _fused_linear([qn], [wq_ext], tm=tm, epilogue=epilogue, out_shapes=(out_shape,),
                           out_specs=(out_spec,), row_extras=(cos_rows, sin_rows),
                           const_extras=(wuk_t,), name="q_proj")
    return out


def _pool_group_out(ext_ref, base, tl, pos0, g, w, PG, pw_ref, scale_ref):
    lo, hi = g * PG, (g + 1) * PG
    x = ext_ref[base:base + tl, lo:hi]
    win = x
    for j in range(1, w):
        win = win + ext_ref[base - j:base - j + tl, lo:hi]
    if pos0 is None:
        d = win * (1.0 / w) - x
    else:
        pos = pos0 + lax.broadcasted_iota(jnp.int32, (tl, 1), 0)
        cnt = jnp.minimum(pos + 1, w).astype(F32)
        d = win / cnt - x
    return _dot(d.astype(BF16), pw_ref[g]) * scale_ref[:, lo:hi]


def _pool_prompt(u, past, pool_w, pool_scale, B, L):
    DP = u.shape[1]
    PS = past.shape[1]
    HALO = 16
    G = len(POOL_WINDOWS)
    PG = DP // G
    tl = _tile(L, POOL_TL, HALO)
    nl = L // tl

    def kern(u_ref, halo_ref, past_ref, pw_ref, sc_ref, o_ref, new_ref, ext_ref):
        i = pl.program_id(1)

        @pl.when(i == 0)
        def _():
            ext_ref[0:1, :] = jnp.zeros((1, DP), F32)
            ext_ref[HALO - PS:HALO, :] = past_ref[0]

        @pl.when(i > 0)
        def _():
            ext_ref[0:HALO, :] = halo_ref[...]

        ext_ref[HALO:HALO + tl, :] = u_ref[...]
        for g, w in enumerate(POOL_WINDOWS):
            o_ref[:, g * PG:(g + 1) * PG] = _pool_group_out(
                ext_ref, HALO, tl, i * tl, g, w, PG, pw_ref, sc_ref).astype(BF16)

        @pl.when(i == nl - 1)
        def _():
            new_ref[0] = ext_ref[HALO + tl - PS:HALO + tl, :]

    r = tl // HALO
    return pl.pallas_call(
        kern, grid=(B, nl),
        in_specs=[
            pl.BlockSpec((tl, DP), lambda b, i: (b * nl + i, 0)),
            pl.BlockSpec((HALO, DP), lambda b, i: (jnp.maximum((b * nl + i) * r - 1, 0), 0)),
            pl.BlockSpec((1, PS, DP), lambda b, i: (b, 0, 0)),
            pl.BlockSpec(pool_w.shape, lambda b, i: (0, 0, 0)),
            pl.BlockSpec(pool_scale.shape, lambda b, i: (0, 0)),
        ],
        out_specs=(pl.BlockSpec((tl, DP), lambda b, i: (b * nl + i, 0)),
                   pl.BlockSpec((1, PS, DP), lambda b, i: (b, 0, 0))),
        out_shape=(jax.ShapeDtypeStruct((B * L, DP), BF16), jax.ShapeDtypeStruct((B, PS, DP), F32)),
        scratch_shapes=[pltpu.VMEM((HALO + tl, DP), F32)],
        compiler_params=_cparams(40), name="pool_prompt")(u, u, past, pool_w, pool_scale)


def _pool_sample(ext_tm, pool_w, pool_scale, Ld, start):
    R, DB, DP = ext_tm.shape
    PS = R - Ld
    G = len(POOL_WINDOWS)
    PG = DP // G

    def kern(e_ref, pw_ref, sc_ref, o_ref):
        for t in range(Ld):
            for g, w in enumerate(POOL_WINDOWS):
                lo, hi = g * PG, (g + 1) * PG
                x = e_ref[PS + t, :, lo:hi]
                win = x
                for j in range(1, w):
                    win = win + e_ref[PS + t - j, :, lo:hi]
                cnt = float(min(start + t + 1, w))
                d = win / cnt - x
                o_ref[t, :, lo:hi] = (_dot(d.astype(BF16), pw_ref[g]) * sc_ref[:, lo:hi]).astype(BF16)

    return pl.pallas_call(
        kern, grid=(1,),
        in_specs=[pl.BlockSpec(ext_tm.shape, lambda i: (0, 0, 0)),
                  pl.BlockSpec(pool_w.shape, lambda i: (0, 0, 0)),
                  pl.BlockSpec(pool_scale.shape, lambda i: (0, 0))],
        out_specs=pl.BlockSpec((Ld, DB, DP), lambda i: (0, 0, 0)),
        out_shape=jax.ShapeDtypeStruct((Ld, DB, DP), BF16),
        compiler_params=_cparams(48), name="pool_sample")(ext_tm, pool_w, pool_scale)


def _flash_prompt(qcat, kcat, ckv_t, wuv, B, L, H, KL):
    QW = qcat.shape[2]
    tq = _tile(L, FLASH_TQ, LANES)
    tk = _tile(L, FLASH_TK, LANES)
    assert tk % tq == 0 and tq % LANES == 0
    nq, nk = L // tq, L // tk
    R = H * tq

    def last_needed(qi):
        return (qi * tq + tq - 1) // tk

    pairs = [(qi, ki) for qi in range(nq) for ki in range(last_needed(qi) + 1)]
    qi_tab = jnp.asarray([p[0] for p in pairs], jnp.int32)
    ki_tab = jnp.asarray([p[1] for p in pairs], jnp.int32)

    def kern(qi_ref, ki_ref, q_ref, k_ref, kt_ref, wuv_ref, o_ref, m_sc, l_sc, acc_sc):
        qi, ki = qi_ref[pl.program_id(1)], ki_ref[pl.program_id(1)]

        @pl.when(ki == 0)
        def _():
            m_sc[...] = jnp.full((1, R), NEG_BIG, F32)
            l_sc[...] = jnp.zeros((1, R), F32)
            acc_sc[...] = jnp.zeros((KL, R), F32)

        def update(masked, nkeys):
            q = q_ref[...].reshape(R, QW)
            s = lax.dot_general(k_ref[0:nkeys, :], q, NT_DIMS, preferred_element_type=F32)
            if masked:
                kpos = ki * tk + lax.broadcasted_iota(jnp.int32, (nkeys, R), 0)
                qpos = qi * tq + (lax.broadcasted_iota(jnp.int32, (nkeys, R), 1) % tq)
                s = jnp.where(kpos <= qpos, s, NEG_BIG)
            m_prev = m_sc[...]
            m_new = jnp.maximum(m_prev, jnp.max(s, axis=0, keepdims=True))
            alpha = jnp.exp2(m_prev - m_new)
            p = jnp.exp2(s - m_new)
            l_sc[...] = alpha * l_sc[...] + jnp.sum(p, axis=0, keepdims=True)
            acc_sc[...] = alpha * acc_sc[...] + _dot(kt_ref[:, 0:nkeys], p.astype(BF16))
            m_sc[...] = m_new

        pl.when(ki < last_needed(qi))(functools.partial(update, False, tk))
        for v in range(tk // tq):
            pl.when((ki == last_needed(qi)) & (qi % (tk // tq) == v))(
                functools.partial(update, True, (v + 1) * tq))

        @pl.when(ki == last_needed(qi))
        def _():
            ctx_t = (acc_sc[...] / l_sc[...]).astype(BF16)
            for h in range(H):
                o_ref[:, h * V_DIM:(h + 1) * V_DIM] = lax.dot_general(
                    ctx_t[:, h * tq:(h + 1) * tq], wuv_ref[h], TN_DIMS,
                    preferred_element_type=F32).astype(BF16)

    grid_spec = pltpu.PrefetchScalarGridSpec(
        num_scalar_prefetch=2, grid=(B, len(pairs)),
        in_specs=[
            pl.BlockSpec((H, tq, QW), lambda b, s, qt, kt: (0, b * nq + qt[s], 0)),
            pl.BlockSpec((tk, QW), lambda b, s, qt, kt: (b * nk + kt[s], 0)),
            pl.BlockSpec((KL, tk), lambda b, s, qt, kt: (0, b * nk + kt[s])),
            pl.BlockSpec(wuv.shape, lambda b, s, qt, kt: (0, 0, 0)),
        ],
        out_specs=pl.BlockSpec((tq, H * V_DIM), lambda b, s, qt, kt: (b * nq + qt[s], 0)),
        scratch_shapes=[pltpu.VMEM((1, R), F32), pltpu.VMEM((1, R), F32), pltpu.VMEM((KL, R), F32)])
    return pl.pallas_call(
        kern, grid_spec=grid_spec, out_shape=jax.ShapeDtypeStruct((B * L, H * V_DIM), BF16),
        compiler_params=_cparams(56), name="flash_prompt")(qi_tab, ki_tab, qcat, kcat, ckv_t, wuv)


def _decode_attention(page_table, q_s, knew, cache_ckv, cache_krope_t, a, H, Ld, KL, ROPE):
    DB, R, QW = q_s.shape
    KN = knew.shape[1]
    n_pages = page_table.shape[1]
    PG = cache_ckv.shape[2]
    PP = DECODE_PAGES_PER_STEP
    while n_pages % PP:
        PP //= 2
    nj = n_pages // PP
    SB = DECODE_SEQS_PER_STEP if DB % DECODE_SEQS_PER_STEP == 0 else 1
    NP = SB * PP
    n_steps = (DB // SB) * nj

    def kern(pt_ref, q_ref, kn_ref, ck_hbm, kr_hbm, o_ref, ckbuf, krbuf, sems, m_sc, l_sc, acc_sc):
        bi, j = pl.program_id(0), pl.program_id(1)
        n = bi * nj + j

        def page_copies(step, i):
            sb, p = i // PP, i % PP
            seq = (step // nj) * SB + sb
            page = pt_ref[seq * n_pages + (step % nj) * PP + p]
            slot = step % 2
            return (pltpu.make_async_copy(ck_hbm.at[a, page], ckbuf.at[slot, i], sems.at[slot]),
                    pltpu.make_async_copy(kr_hbm.at[a, page], krbuf.at[slot, i], sems.at[slot]))

        def for_pages(step, fn):
            def body(i, c):
                for cp in page_copies(step, i):
                    fn(cp)
                return c
            lax.fori_loop(0, NP, body, 0, unroll=DMA_LOOP_UNROLL)

        @pl.when(n == 0)
        def _():
            for_pages(0, lambda cp: cp.start())

        @pl.when(n + 1 < n_steps)
        def _():
            for_pages(n + 1, lambda cp: cp.start())

        @pl.when(j == 0)
        def _():
            m_sc[...] = jnp.full((SB, R, 1), NEG_BIG, F32)
            l_sc[...] = jnp.zeros((SB, R, 1), F32)
            acc_sc[...] = jnp.zeros((SB, R, KL), F32)

        for_pages(n, lambda cp: cp.wait())
        slot = n % 2

        def update(sb, s, v):
            m_prev = m_sc[sb]
            m_new = jnp.maximum(m_prev, jnp.max(s, axis=-1, keepdims=True))
            alpha = jnp.exp2(m_prev - m_new)
            p = jnp.exp2(s - m_new)
            l_sc[sb] = alpha * l_sc[sb] + jnp.sum(p, axis=-1, keepdims=True)
            acc_sc[sb] = alpha * acc_sc[sb] + _dot(p.astype(BF16), v)
            m_sc[sb] = m_new

        for sb in range(SB):
            q = q_ref[sb]
            ck = ckbuf[slot, sb * PP:(sb + 1) * PP].reshape(PP * PG, KL).astype(BF16)
            kr_t = jnp.concatenate([krbuf[slot, sb * PP + p].astype(BF16) for p in range(PP)], axis=1)
            s = (lax.dot_general(q[:, :KL], ck, NT_DIMS, preferred_element_type=F32)
                 + _dot(q[:, KL:KL + ROPE], kr_t))
            update(sb, s, ck)

        @pl.when(j == nj - 1)
        def _():
            for sb in range(SB):
                q, kn = q_ref[sb], kn_ref[sb]
                s = lax.dot_general(q, kn, NT_DIMS, preferred_element_type=F32)
                t_idx = lax.broadcasted_iota(jnp.int32, (R, KN), 0) // H
                s_idx = lax.broadcasted_iota(jnp.int32, (R, KN), 1)
                update(sb, jnp.where(s_idx <= t_idx, s, NEG_BIG), kn[:, :KL])
                o_ref[sb] = (acc_sc[sb] / l_sc[sb]).astype(BF16)

    grid_spec = pltpu.PrefetchScalarGridSpec(
        num_scalar_prefetch=1, grid=(DB // SB, nj),
        in_specs=[pl.BlockSpec((SB, R, QW), lambda b, j, pt: (b, 0, 0)),
                  pl.BlockSpec((SB, KN, QW), lambda b, j, pt: (b, 0, 0)),
                  pl.BlockSpec(memory_space=pl.ANY),
                  pl.BlockSpec(memory_space=pl.ANY)],
        out_specs=pl.BlockSpec((SB, R, KL), lambda b, j, pt: (b, 0, 0)),
        scratch_shapes=[pltpu.VMEM((2, NP, PG, KL), F32), pltpu.VMEM((2, NP, ROPE, PG), F32),
                        pltpu.SemaphoreType.DMA((2,)),
                        pltpu.VMEM((SB, R, 1), F32), pltpu.VMEM((SB, R, 1), F32),
                        pltpu.VMEM((SB, R, KL), F32)])
    return pl.pallas_call(
        kern, grid_spec=grid_spec, out_shape=jax.ShapeDtypeStruct((DB, R, KL), BF16),
        compiler_params=_cparams(48), name="decode_attention")(
            page_table.reshape(-1), q_s, knew, cache_ckv, cache_krope_t)


def _ctx_to_attn_out(ctx, wuv, H, KL):
    T = ctx.shape[0]
    tm = _tile(T, LINEAR_TM)

    def kern(c_ref, w_ref, o_ref):
        for h in range(H):
            o_ref[:, h * V_DIM:(h + 1) * V_DIM] = _dot(
                c_ref[:, h * KL:(h + 1) * KL], w_ref[h]).astype(BF16)

    return pl.pallas_call(
        kern, grid=(T // tm,),
        in_specs=[pl.BlockSpec((tm, H * KL), lambda i: (i, 0)),
                  pl.BlockSpec(wuv.shape, lambda i: (0, 0, 0))],
        out_specs=pl.BlockSpec((tm, H * V_DIM), lambda i: (i, 0)),
        out_shape=jax.ShapeDtypeStruct((T, H * V_DIM), BF16),
        compiler_params=_cparams(40), name="ctx_to_attn_out")(ctx, wuv)


def _linear_residual(xs, ws, res, name):
    T, D = res.shape
    tm = _tile(T, LINEAR_TM)

    def epilogue(acc, x0, r_refs, c_refs, o_refs):
        o_refs[0][...] = r_refs[0][...] + acc

    (out,) = _fused_linear(xs, ws, tm=tm, epilogue=epilogue,
                           out_shapes=(jax.ShapeDtypeStruct((T, D), F32),),
                           out_specs=(pl.BlockSpec((tm, D), lambda i: (i, 0)),),
                           row_extras=(res,), name=name)
    return out


def _ffn(h, gain, w_gu, w_down):
    T, D = h.shape
    F = w_down.shape[0]
    tm = _tile(T, FFN_TM)
    tf = _tile(F, FFN_TF, LANES)
    nf = F // tf

    def kern(x_ref, g_ref, wg_ref, wu_ref, wd_ref, o_ref, xn_sc):
        f = pl.program_id(1)

        @pl.when(f == 0)
        def _():
            x = x_ref[...]
            xn_sc[...] = _rms(x, g_ref[...]).astype(BF16)
            o_ref[...] = x

        xn = xn_sc[...]
        act = (_silu(_dot(xn, wg_ref[...])) * _dot(xn, wu_ref[...])).astype(BF16)
        o_ref[...] += _dot(act, wd_ref[...])

    return pl.pallas_call(
        kern, grid=(T // tm, nf),
        in_specs=[pl.BlockSpec((tm, D), lambda i, f: (i, 0)),
                  pl.BlockSpec(gain.shape, lambda i, f: (0, 0)),
                  pl.BlockSpec((D, tf), lambda i, f: (0, f)),
                  pl.BlockSpec((D, tf), lambda i, f: (0, nf + f)),
                  pl.BlockSpec((tf, D), lambda i, f: (f, 0))],
        out_specs=pl.BlockSpec((tm, D), lambda i, f: (i, 0)),
        out_shape=jax.ShapeDtypeStruct((T, D), F32),
        scratch_shapes=[pltpu.VMEM((tm, D), BF16)],
        compiler_params=_cparams(48), name="ffn")(h, gain, w_gu, w_gu, w_down)


def _hgrn_in_proj(h, gain, w_in, lb):
    T, D = h.shape
    HK = w_in.shape[1] // 4
    tm = _tile(T, LINEAR_TM)

    def call(col, epilogue, dtypes, extras=()):
        out_shapes = tuple(jax.ShapeDtypeStruct((T, HK), dt) for dt in dtypes)
        out_specs = tuple(pl.BlockSpec((tm, HK), lambda i: (i, 0)) for _ in dtypes)
        return _fused_linear([h], [w_in], tm=tm, gain=gain, epilogue=epilogue,
                             wspecs=[pl.BlockSpec((D, HK), lambda i: (0, col))],
                             out_shapes=out_shapes, out_specs=out_specs, const_extras=extras,
                             name=f"hgrn_in_proj_{col}")

    def ep_silu(acc, x0, r_refs, c_refs, o_refs):
        o_refs[0][...] = _silu(acc).astype(BF16)

    def ep_ident(acc, x0, r_refs, c_refs, o_refs):
        o_refs[0][...] = acc.astype(BF16)

    def ep_gate(acc, x0, r_refs, c_refs, o_refs):
        k = (1.0 - c_refs[0][...]) / (1.0 + jnp.exp(acc))
        o_refs[0][...] = k.astype(BF16)
        o_refs[1][...] = jnp.log1p(-k)

    (qs,) = call(0, ep_silu, (BF16,))
    kk, logf = call(1, ep_gate, (BF16, F32), extras=(lb,))
    (vv,) = call(2, ep_ident, (BF16,))
    (gs,) = call(3, ep_silu, (BF16,))
    return qs, kk, logf, vv, gs


def _gla_band(qf, kf, vf, fg, n_diag, lane_sum):
    C = qf.shape[0]
    r8 = lax.broadcasted_iota(jnp.int32, qf.shape, 0) % SUBLANES
    gprod = None
    parts = []
    for d in range(n_diag):
        if d == 0:
            p = qf * kf
        else:
            fr = fg if d == 1 else pltpu.roll(fg, d - 1, 0)
            gprod = fr if gprod is None else gprod * fr
            p = jnp.where(r8 >= d, qf * gprod * pltpu.roll(kf, d, 0), 0.0)
        parts.append(p)
    sums = lane_sum(parts)
    out = None
    for d in range(n_diag):
        vr = vf if d == 0 else pltpu.roll(vf, d, 0)
        term = sums[d] * vr
        out = term if out is None else out + term
    return out


def _gla_prompt(qs, kk, logf, vv, gs, hg_norm, B, L, H):
    T, HK = qs.shape
    K = HK // H
    ct = _tile(L, GLA_CT, GLA_CHUNK)
    C = min(GLA_CHUNK, ct)
    nct = L // ct
    ncc = ct // C
    HPS = GLA_HEADS_PER_STEP if H % GLA_HEADS_PER_STEP == 0 else 1

    def kern(q_ref, k_ref, lf_ref, v_ref, g_ref, hn_ref, o_ref, s_ref, st_sc):
        ci = pl.program_id(2)

        @pl.when(ci == 0)
        def _():
            st_sc[...] = jnp.zeros((HPS, K, K), F32)

        rows = lax.broadcasted_iota(jnp.int32, (C, C), 0)
        cols = lax.broadcasted_iota(jnp.int32, (C, C), 1)
        tril = (rows >= cols).astype(F32)
        ones_bf = jnp.ones((K, K), BF16)

        def lane_sum(parts):
            r = _dot(jnp.concatenate([p.astype(BF16) for p in parts], axis=0), ones_bf)
            return [r[d * C:(d + 1) * C] for d in range(len(parts))]

        def chunk(c, hh, st):
            sl, hl = slice(c * C, (c + 1) * C), slice(hh * K, (hh + 1) * K)
            qf, kf, vf = q_ref[sl, hl].astype(F32), k_ref[sl, hl].astype(F32), v_ref[sl, hl].astype(F32)
            lf = lf_ref[sl, hl]
            fg = jnp.exp(lf)
            b = jnp.dot(tril, lf, precision=lax.Precision.HIGHEST, preferred_element_type=F32)
            blast = b[C - 1:C, :]
            qb = qf * jnp.exp(b)
            kb = kf * jnp.exp(blast - b)
            o = lax.dot_general(qb.astype(BF16), st.astype(BF16), NT_DIMS, preferred_element_type=F32)
            amat = None
            blk = C // 2
            while blk >= SUBLANES:
                q_parts, k_parts = [], []
                for m in range(C // blk):
                    r = slice(m * blk, (m + 1) * blk)
                    if m % 2 == 1:
                        ref = b[m * blk - 1:m * blk, :]
                        q_parts.append(qf[r] * jnp.exp(b[r] - ref))
                        k_parts.append(jnp.zeros((blk, K), F32))
                    else:
                        ref = b[(m + 1) * blk - 1:(m + 1) * blk, :]
                        q_parts.append(jnp.zeros((blk, K), F32))
                        k_parts.append(kf[r] * jnp.exp(ref - b[r]))
                ql = jnp.concatenate(q_parts, axis=0).astype(BF16)
                kl = jnp.concatenate(k_parts, axis=0).astype(BF16)
                al = lax.dot_general(ql, kl, NT_DIMS, preferred_element_type=F32)
                al = jnp.where((rows // (2 * blk)) == (cols // (2 * blk)), al, 0.0)
                amat = al if amat is None else amat + al
                blk //= 2
            if amat is not None:
                o = o + _dot(amat.astype(BF16), vf.astype(BF16))
            o = o + _gla_band(qf, kf, vf, fg, min(SUBLANES, C), lane_sum)
            st = st * jnp.exp(blast) + lax.dot_general(
                vf.astype(BF16), kb.astype(BF16), TN_DIMS, preferred_element_type=F32)
            on = _rms(o, hn_ref[...]) * g_ref[sl, hl].astype(F32)
            o_ref[sl, hl] = on.astype(BF16)
            return st

        sts = [st_sc[hh] for hh in range(HPS)]
        for c in range(ncc):
            sts = [chunk(c, hh, sts[hh]) for hh in range(HPS)]
        for hh in range(HPS):
            st_sc[hh] = sts[hh]

        @pl.when(ci == nct - 1)
        def _():
            for hh in range(HPS):
                s_ref[0, hh] = sts[hh].T

    tok = lambda b, h, ci: (b * nct + ci, h)
    return pl.pallas_call(
        kern, grid=(B, H // HPS, nct),
        in_specs=[pl.BlockSpec((ct, HPS * K), tok)] * 5 + [pl.BlockSpec(hg_norm.shape, lambda b, h, ci: (0, 0))],
        out_specs=(pl.BlockSpec((ct, HPS * K), tok),
                   pl.BlockSpec((1, HPS, K, K), lambda b, h, ci: (b, h, 0, 0))),
        out_shape=(jax.ShapeDtypeStruct((T, HK), BF16), jax.ShapeDtypeStruct((B, H, K, K), F32)),
        scratch_shapes=[pltpu.VMEM((HPS, K, K), F32)],
        compiler_params=_cparams(32), name="gla_prompt")(qs, kk, logf, vv, gs, hg_norm)


def _gla_sample(qs, kk, logf, vv, gs, hg_norm, s0, DB, Ld, H):
    HK = qs.shape[2]
    K = HK // H
    C = SUBLANES
    assert Ld <= C

    def kern(q_ref, k_ref, lf_ref, v_ref, g_ref, hn_ref, s0_ref, o_ref, s_ref, pad_sc):
        rows = lax.broadcasted_iota(jnp.int32, (C, C), 0)
        cols = lax.broadcasted_iota(jnp.int32, (C, C), 1)
        tril = (rows >= cols).astype(F32)

        def padded(ref, slot):
            pad_sc[slot] = jnp.zeros((C, HK), F32)
            pad_sc[slot, 0:Ld, :] = ref[0].astype(F32)
            return pad_sc[slot]

        qf, kf, lf, vf = padded(q_ref, 0), padded(k_ref, 1), padded(lf_ref, 2), padded(v_ref, 3)
        fg = jnp.exp(lf)
        b = jnp.dot(tril, lf, precision=lax.Precision.HIGHEST, preferred_element_type=F32)
        blast = b[C - 1:C, :]
        qb = (qf * jnp.exp(b)).astype(BF16)
        kb = (kf * jnp.exp(blast - b)).astype(BF16)
        dec = jnp.exp(blast)
        vb = vf.astype(BF16)

        def head_sum(parts):
            return [jnp.sum(p, axis=-1, keepdims=True) for p in parts]

        eye = (lax.broadcasted_iota(jnp.int32, (K, K), 0) == lax.broadcasted_iota(jnp.int32, (K, K), 1))
        for h in range(H):
            hl = slice(h * K, (h + 1) * K)
            s0 = s0_ref[0, h]
            o = _dot(qb[:, hl], s0.astype(BF16))
            o = o + _gla_band(qf[:, hl], kf[:, hl], vf[:, hl], fg[:, hl], Ld, head_sum)
            dcol = jnp.sum(jnp.where(eye, jnp.broadcast_to(dec[:, hl], (K, K)), 0.0), axis=-1, keepdims=True)
            s_ref[0, h] = s0 * dcol + lax.dot_general(kb[:, hl], vb[:, hl], TN_DIMS,
                                                      preferred_element_type=F32)
            on = _rms(o[0:Ld], hn_ref[...]) * g_ref[0, :, hl].astype(F32)
            o_ref[0, :, hl] = on.astype(BF16)

    tok = pl.BlockSpec((1, Ld, HK), lambda b: (b, 0, 0))
    st_spec = pl.BlockSpec((1, H, K, K), lambda b: (b, 0, 0, 0))
    return pl.pallas_call(
        kern, grid=(DB,),
        in_specs=[tok] * 5 + [pl.BlockSpec(hg_norm.shape, lambda b: (0, 0)), st_spec],
        out_specs=(tok, st_spec),
        out_shape=(jax.ShapeDtypeStruct((DB, Ld, HK), BF16), jax.ShapeDtypeStruct(s0.shape, F32)),
        scratch_shapes=[pltpu.VMEM((4, C, HK), F32)],
        compiler_params=_cparams(32), name="gla_sample")(qs, kk, logf, vv, gs, hg_norm, s0)


def _router(h, gain, w_router, b_router):
    T, D = h.shape
    E = w_router.shape[1]
    tm = _tile(T, LINEAR_TM)
    assert TOP_K == 2

    def kern(x_ref, g_ref, w_ref, b_ref, i_ref, o_ref):
        xn = _rms(x_ref[...], g_ref[...])
        logits = jnp.dot(xn, w_ref[...], precision=lax.Precision.HIGHEST,
                         preferred_element_type=F32) + b_ref[...]
        idx = lax.broadcasted_iota(jnp.int32, logits.shape, 1)
        m1 = jnp.max(logits, axis=-1, keepdims=True)
        i1 = jnp.min(jnp.where(logits == m1, idx, E), axis=-1, keepdims=True)
        rest = jnp.where(idx == i1, -jnp.inf, logits)
        m2 = jnp.max(rest, axis=-1, keepdims=True)
        i2 = jnp.min(jnp.where(rest == m2, idx, E), axis=-1, keepdims=True)
        e2 = jnp.exp(m2 - m1)
        g1 = 1.0 / (1.0 + e2)
        slot = lax.broadcasted_iota(jnp.int32, (tm, TOP_K), 1)
        i_ref[...] = jnp.where(slot == 0, i1, i2)
        o_ref[...] = jnp.where(slot == 0, g1, e2 * g1)

    return pl.pallas_call(
        kern, grid=(T // tm,),
        in_specs=[pl.BlockSpec((tm, D), lambda i: (i, 0)),
                  pl.BlockSpec(gain.shape, lambda i: (0, 0)),
                  pl.BlockSpec(w_router.shape, lambda i: (0, 0)),
                  pl.BlockSpec(b_router.shape, lambda i: (0, 0))],
        out_specs=(pl.BlockSpec((tm, TOP_K), lambda i: (i, 0)), pl.BlockSpec((tm, TOP_K), lambda i: (i, 0))),
        out_shape=(jax.ShapeDtypeStruct((T, TOP_K), jnp.int32), jax.ShapeDtypeStruct((T, TOP_K), F32)),
        compiler_params=_cparams(32), name="router")(h, gain, w_router, b_router)


def _route_plan(idx, E, tmx):
    T = idx.shape[0]
    NP = TOP_K * T + E * tmx
    assert NP % tmx == 0
    NT = NP // tmx
    sel = jnp.any(idx[:, :, None] == jnp.arange(E, dtype=jnp.int32), axis=1).astype(jnp.int32)
    incl = jnp.cumsum(sel, axis=0)
    counts = incl[-1]
    padded = ((counts + tmx - 1) // tmx) * tmx
    ends = jnp.cumsum(padded)
    offs = ends - padded
    rank = jnp.take_along_axis(incl - sel, idx, axis=1)
    pos = (offs[idx] + rank).reshape(-1).astype(jnp.int32)
    tok = jnp.repeat(jnp.arange(T, dtype=jnp.int32), TOP_K)
    src = jnp.zeros((NP,), jnp.int32).at[pos].set(tok)
    n_used = (ends[-1] // tmx).astype(jnp.int32)
    tile_start = jnp.minimum(jnp.arange(NT, dtype=jnp.int32), n_used - 1) * tmx
    tile_expert = jnp.minimum(jnp.sum(tile_start[:, None] >= ends[None, :], axis=1), E - 1).astype(jnp.int32)
    return pos, src, tile_expert, n_used.reshape(1)


def _moe_experts(h, gain, src, tile_expert, n_used, w_gu, w_down, tmx):
    T, D = h.shape
    E, F = w_down.shape[0], w_down.shape[1]
    NP = src.shape[0]
    NT = NP // tmx
    tf = _tile(F, MOE_TF, LANES)
    nf = F // tf
    n_issue = max(d for d in range(0, min(MOE_GATHER_ISSUE_STEPS, nf - 1) + 1) if d == 0 or tmx % d == 0)
    rows_per_issue = tmx // n_issue if n_issue else 0

    def kern(src_ref, te_ref, nu_ref, h_ref, g_ref, wg_ref, wu_ref, wd_ref, o_ref, xbuf, xn_sc, sem):
        i, f = pl.program_id(0), pl.program_id(1)
        n_used_v = nu_ref[0]

        def row_copy(tile, r):
            return pltpu.make_async_copy(h_ref.at[pl.ds(src_ref[tile * tmx + r], 1)],
                                         xbuf.at[pl.ds(r, 1)], sem)

        def start_gather(tile):
            lax.fori_loop(0, tmx, lambda r, c: (row_copy(tile, r).start(), c)[1], 0, unroll=DMA_LOOP_UNROLL)

        @pl.when((f == 0) & (i == 0))
        def _():
            start_gather(0)

        @pl.when((f == 0) & (i < n_used_v))
        def _():
            lax.fori_loop(0, tmx, lambda r, c: (row_copy(i, r).wait(), c)[1], 0, unroll=DMA_LOOP_UNROLL)
            xn_sc[...] = _rms(xbuf[...], g_ref[...]).astype(BF16)
            if n_issue == 0:
                @pl.when(i + 1 < n_used_v)
                def _():
                    start_gather(i + 1)

        @pl.when(f == 0)
        def _():
            o_ref[...] = jnp.zeros((tmx, D), F32)

        def matmul_step(issue):
            if issue:
                for r in range(rows_per_issue):
                    row_copy(i + 1, (f - 1) * rows_per_issue + r).start()
            xn = xn_sc[...]
            act = (_silu(_dot(xn, wg_ref[0])) * _dot(xn, wu_ref[0])).astype(BF16)
            o_ref[...] += _dot(act, wd_ref[0])

        issue_now = (f >= 1) & (f <= n_issue) & (i + 1 < n_used_v)
        pl.when((i < n_used_v) & issue_now)(functools.partial(matmul_step, True))
        pl.when((i < n_used_v) & jnp.logical_not(issue_now))(functools.partial(matmul_step, False))

    def wspec(shape, fn):
        def index_map(i, f, src_r, te_r, nu_r):
            return fn(te_r[i], jnp.where(i < nu_r[0], f, nf - 1))
        return pl.BlockSpec(shape, index_map)

    grid_spec = pltpu.PrefetchScalarGridSpec(
        num_scalar_prefetch=3, grid=(NT, nf),
        in_specs=[pl.BlockSpec(memory_space=pl.ANY),
                  pl.BlockSpec(gain.shape, lambda i, f, *_: (0, 0)),
                  wspec((1, D, tf), lambda e, f: (e, 0, f)),
                  wspec((1, D, tf), lambda e, f: (e, 0, nf + f)),
                  wspec((1, tf, D), lambda e, f: (e, f, 0))],
        out_specs=pl.BlockSpec((tmx, D), lambda i, f, *_: (i, 0)),
        scratch_shapes=[pltpu.VMEM((tmx, D), F32), pltpu.VMEM((tmx, D), BF16),
                        pltpu.SemaphoreType.DMA(())])
    return pl.pallas_call(
        kern, grid_spec=grid_spec, out_shape=jax.ShapeDtypeStruct((NP, D), F32),
        compiler_params=_cparams(48), name="moe_experts")(
            src, tile_expert, n_used, h, gain, w_gu, w_gu, w_down)


def _moe_combine(h, y_sorted, pos, gate, final_gain):
    T, D = h.shape
    tm = _tile(T, MOE_COMBINE_TM)
    nt = T // tm

    def kern(pos_ref, h_ref, y_ref, gt_ref, fg_ref, o_ref, ybuf, sems):
        i = pl.program_id(0)

        def row_copy(step, r, s):
            slot = step % 2
            return pltpu.make_async_copy(y_ref.at[pl.ds(pos_ref[(step * tm + r) * TOP_K + s], 1)],
                                         ybuf.at[slot, s, pl.ds(r, 1)], sems.at[slot])

        def for_rows(step, fn):
            def body(r, c):
                for s in range(TOP_K):
                    fn(row_copy(step, r, s))
                return c
            lax.fori_loop(0, tm, body, 0, unroll=DMA_LOOP_UNROLL)

        @pl.when(i == 0)
        def _():
            for_rows(0, lambda cp: cp.start())

        @pl.when(i + 1 < nt)
        def _():
            for_rows(i + 1, lambda cp: cp.start())

        for_rows(i, lambda cp: cp.wait())
        slot = i % 2
        gt = gt_ref[...]
        y = h_ref[...]
        for s in range(TOP_K):
            y = y + gt[:, s:s + 1] * ybuf[slot, s]
        o_ref[...] = _rms(y, fg_ref[...])

    grid_spec = pltpu.PrefetchScalarGridSpec(
        num_scalar_prefetch=1, grid=(nt,),
        in_specs=[pl.BlockSpec((tm, D), lambda i, *_: (i, 0)),
                  pl.BlockSpec(memory_space=pl.ANY),
                  pl.BlockSpec((tm, TOP_K), lambda i, *_: (i, 0)),
                  pl.BlockSpec(final_gain.shape, lambda i, *_: (0, 0))],
        out_specs=pl.BlockSpec((tm, D), lambda i, *_: (i, 0)),
        scratch_shapes=[pltpu.VMEM((2, TOP_K, tm, D), F32), pltpu.SemaphoreType.DMA((2,))])
    return pl.pallas_call(
        kern, grid_spec=grid_spec, out_shape=jax.ShapeDtypeStruct((T, D), F32),
        compiler_params=_cparams(40), name="moe_combine")(pos, h, y_sorted, gate, final_gain)


def _moe(h, gain, w_router, b_router, w_gu, w_down, final_gain):
    T = h.shape[0]
    E = w_down.shape[0]
    tmx = _tile(max(MOE_MIN_TILE, 2 * TOP_K * T // E), MOE_TM)
    idx, gate = _router(h, gain, w_router, b_router)
    pos, src, tile_expert, n_used = _route_plan(idx, E, tmx)
    y_sorted = _moe_experts(h, gain, src, tile_expert, n_used, w_gu, w_down, tmx)
    return _moe_combine(h, y_sorted, pos, gate, final_gain)


def _rope_rows(pos, rope_dim):
    inv = ROPE_THETA ** (-jnp.arange(0, rope_dim, 2, dtype=F32) / rope_dim)
    ang = pos.astype(F32)[:, None] * inv[None, :]
    z = jnp.zeros((pos.shape[0], LANES - rope_dim), F32)
    cos, sin = jnp.cos(ang), jnp.sin(ang)
    return jnp.concatenate([cos, cos, z], axis=-1), jnp.concatenate([sin, sin, z], axis=-1)


def _rot_cols(w):
    half = w.shape[-1] // 2
    return jnp.concatenate([-w[..., half:], w[..., :half]], axis=-1)


def _pad_lanes(w):
    pad = LANES - w.shape[-1]
    return jnp.pad(w, [(0, 0)] * (w.ndim - 1) + [(0, pad)])


def _prep_even(a, w_in_e, w_q_b, w_kv_b, pool_w, w_out_e, w_ffn_gu, w_ffn_down, dims, H):
    DP, QL, KL, ROPE = dims
    w_in = w_in_e[a]
    k_raw = w_in[:, DP + QL + KL:]
    w_ext = jnp.concatenate([w_in[:, :DP + QL + KL], _pad_lanes(k_raw), _pad_lanes(_rot_cols(k_raw))],
                            axis=-1).astype(BF16)
    wq = w_q_b[a].reshape(QL, H, NOPE_DIM + ROPE)
    wq_rope = wq[..., NOPE_DIM:]
    wq_ext = jnp.concatenate([wq[..., :NOPE_DIM].reshape(QL, H * NOPE_DIM),
                              _pad_lanes(wq_rope).reshape(QL, H * LANES),
                              _pad_lanes(_rot_cols(wq_rope)).reshape(QL, H * LANES)], axis=-1).astype(BF16)
    w_kv = w_kv_b[a].reshape(KL, H, NOPE_DIM + V_DIM)
    wuk_t = jnp.transpose(w_kv[..., :NOPE_DIM], (1, 2, 0)).astype(BF16)
    wuv = jnp.transpose(w_kv[..., NOPE_DIM:], (1, 0, 2)).astype(BF16)
    w_out = w_out_e[a].astype(BF16)
    return dict(w_ext=w_ext, wq_ext=wq_ext, wuk_t=wuk_t, wuv=wuv, pool_w=pool_w[a].astype(BF16),
                w_out_pool=w_out[:DP], w_out_attn=w_out[DP:],
                w_gu=w_ffn_gu[a].astype(BF16), w_down=w_ffn_down[a].astype(BF16))


def kernel(x_prompt, x_sample, cache_ckv, cache_krope, page_table, state_pool, state_hgrn,
           norm_mix_e, w_in_e, q_norm, w_q_b, kv_norm, w_kv_b, pool_w, pool_scale, w_out_e,
           norm_ffn_e, w_ffn_gu, w_ffn_down,
           norm_mix_o, w_in_o, hg_lower_bound, hg_norm, w_out_o, norm_ffn_o, w_router, b_router,
           w_exp_gu, w_exp_down, final_norm):
    B, L, D = x_prompt.shape
    DB, Ld, _ = x_sample.shape
    n_pages = page_table.shape[1]
    PG = cache_ckv.shape[2]
    past_len = n_pages * PG
    KL, ROPE = cache_ckv.shape[3], cache_krope.shape[3]
    DP, PS = state_pool.shape[3], state_pool.shape[2]
    QL = q_norm.shape[1]
    H = w_q_b.shape[2] // (NOPE_DIM + ROPE)
    HG = state_hgrn.shape[2]
    depth = hg_lower_bound.shape[0]
    dims = (DP, QL, KL, ROPE)
    scale = float((NOPE_DIM + ROPE) ** -0.5 * math.log2(math.e))
    cache_krope_t = jnp.swapaxes(cache_krope, 2, 3)
    QW = KL + LANES
    KN = 16
    assert Ld <= KN and ROPE <= LANES

    row = lambda v: v.reshape(1, -1).astype(F32)
    lb_p = jax.nn.softmax(hg_lower_bound.astype(F32), axis=0)
    lower_bounds = jnp.cumsum(lb_p, axis=0) - lb_p[0]

    hp = x_prompt.reshape(B * L, D)
    hs = x_sample.reshape(DB * Ld, D)
    cos_p, sin_p = _rope_rows(jnp.tile(jnp.arange(L), B), ROPE)
    cos_s, sin_s = _rope_rows(jnp.tile(past_len + jnp.arange(Ld), DB), ROPE)

    outs_p = dict(ckv=[], krope=[], pool=[], hgrn=[])
    outs_s = dict(ckv=[], krope=[], pool=[], hgrn=[])
    for l in range(depth):
        a = l // 2
        if l % 2 == 0:
            w = _prep_even(a, w_in_e, w_q_b, w_kv_b, pool_w, w_out_e, w_ffn_gu, w_ffn_down, dims, H)
            g_mix, g_q, g_kv = row(norm_mix_e[a]), row(q_norm[a]), row(kv_norm[a])
            g_ffn, p_scale = row(norm_ffn_e[a]), row(pool_scale[a])

            u, qn, ckv, kpe, kcat, ckv_t = _even_in_proj(hp, g_mix, w["w_ext"], g_q, g_kv, cos_p, sin_p,
                                                         dims, True)
            pool_out, pool_new = _pool_prompt(u, jnp.zeros((B, PS, DP), F32), w["pool_w"], p_scale, B, L)
            qcat = _q_proj(qn, w["wq_ext"], w["wuk_t"], cos_p, sin_p, H, KL, scale, head_major=True)
            attn = _flash_prompt(qcat, kcat, ckv_t, w["wuv"], B, L, H, KL)
            hp = _linear_residual([pool_out, attn], [w["w_out_pool"], w["w_out_attn"]], hp, "out_proj_e")
            hp = _ffn(hp, g_ffn, w["w_gu"], w["w_down"])
            outs_p["ckv"].append(ckv.reshape(B, L, KL))
            outs_p["krope"].append(kpe.reshape(B, L, ROPE))
            outs_p["pool"].append(pool_new)

            u, qn, ckv, kpe, kcat = _even_in_proj(hs, g_mix, w["w_ext"], g_q, g_kv, cos_s, sin_s, dims, False)
            u3 = u.reshape(DB, Ld, DP)
            ext_tm = jnp.transpose(jnp.concatenate([state_pool[a], u3], axis=1), (1, 0, 2))
            pool_tm = _pool_sample(ext_tm, w["pool_w"], p_scale, Ld, past_len)
            pool_out = jnp.transpose(pool_tm, (1, 0, 2)).reshape(DB * Ld, DP)
            qrows = _q_proj(qn, w["wq_ext"], w["wuk_t"], cos_s, sin_s, H, KL, scale, head_major=False)
            q_s = qrows.reshape(DB, Ld * H, QW)
            knew = jnp.pad(kcat.reshape(DB, Ld, QW), ((0, 0), (0, KN - Ld), (0, 0)))
            ctx = _decode_attention(page_table, q_s, knew, cache_ckv, cache_krope_t, a, H, Ld, KL, ROPE)
            attn = _ctx_to_attn_out(ctx.reshape(DB * Ld, H * KL), w["wuv"], H, KL)
            hs = _linear_residual([pool_out, attn], [w["w_out_pool"], w["w_out_attn"]], hs, "out_proj_e")
            hs = _ffn(hs, g_ffn, w["w_gu"], w["w_down"])
            outs_s["ckv"].append(ckv.reshape(DB, Ld, KL))
            outs_s["krope"].append(kpe.reshape(DB, Ld, ROPE))
            outs_s["pool"].append(jnp.concatenate([state_pool[a], u3], axis=1)[:, -PS:])
        else:
            w_in = w_in_o[a].astype(BF16)
            w_out = w_out_o[a].astype(BF16)
            w_gu = w_exp_gu[a].astype(BF16)
            w_dn = w_exp_down[a].astype(BF16)
            g_mix, g_ffn, g_hn = row(norm_mix_o[a]), row(norm_ffn_o[a]), row(hg_norm[a])
            lb = row(lower_bounds[l])
            w_r, b_r = w_router[a].astype(F32), row(b_router[a])
            g_fin = row(final_norm) if l == depth - 1 else None
            assert g_fin is not None

            qs, kk, logf, vv, gs = _hgrn_in_proj(hp, g_mix, w_in, lb)
            on, s_new = _gla_prompt(qs, kk, logf, vv, gs, g_hn, B, L, HG)
            hp = _linear_residual([on], [w_out], hp, "out_proj_o")
            hp = _moe(hp, g_ffn, w_r, b_r, w_gu, w_dn, g_fin)
            outs_p["hgrn"].append(s_new)

            qs, kk, logf, vv, gs = _hgrn_in_proj(hs, g_mix, w_in, lb)
            r3 = lambda t: t.reshape(DB, Ld, -1)
            on, s_new = _gla_sample(r3(qs), r3(kk), r3(logf), r3(vv), r3(gs), g_hn, state_hgrn[a], DB, Ld, HG)
            hs = _linear_residual([on.reshape(DB * Ld, -1)], [w_out], hs, "out_proj_o")
            hs = _moe(hs, g_ffn, w_r, b_r, w_gu, w_dn, g_fin)
            outs_s["hgrn"].append(s_new)

    return (hp.reshape(B, L, D), hs.reshape(DB, Ld, D),
            jnp.stack(outs_p["ckv"]), jnp.stack(outs_p["krope"]), jnp.stack(outs_p["pool"]),
            jnp.stack(outs_p["hgrn"]),
            jnp.stack(outs_s["ckv"]), jnp.stack(outs_s["krope"]), jnp.stack(outs_s["pool"]),
            jnp.stack(outs_s["hgrn"]))
```

```python
import functools
import math

import jax
import jax.numpy as jnp
from jax import lax
from jax.experimental import pallas as pl
from jax.experimental.pallas import tpu as pltpu

F32 = jnp.float32
BF16 = jnp.bfloat16

EPS = 1e-6
POOL_WINDOWS = (2, 4, 8, 16)
NOPE_DIM = 128
V_DIM = 128
ROPE_THETA = 10000.0
TOP_K = 2
LANES = 128
SUBLANES = 8
NEG_BIG = -1e30

LINEAR_TM = 512
FFN_TM = 512
FFN_TF = 512
MOE_TM = 512
MOE_MIN_TILE = 16
MOE_TF = 256
MOE_COMBINE_TM = 256
MOE_GATHER_ISSUE_STEPS = 8
DMA_LOOP_UNROLL = 8
FLASH_TQ = 128
FLASH_TK = 512
POOL_TL = 512
GLA_CT = 256
GLA_CHUNK = 128
GLA_HEADS_PER_STEP = 8
DECODE_PAGES_PER_STEP = 16
DECODE_SEQS_PER_STEP = 2

NT_DIMS = (((1,), (1,)), ((), ()))
TN_DIMS = (((0,), (0,)), ((), ()))


def _cparams(vmem_mb):
    return pltpu.CompilerParams(vmem_limit_bytes=vmem_mb * 2 ** 20)


def _tile(n, pref, mult=SUBLANES):
    if n <= pref:
        return n
    for t in range(pref, 0, -1):
        if n % t == 0 and t % mult == 0:
            return t
    return n


def _rms(x, g):
    return x * lax.rsqrt(jnp.mean(x * x, axis=-1, keepdims=True) + EPS) * g


def _silu(x):
    return x / (1.0 + jnp.exp(-x))


def _dot(a, b):
    return jnp.dot(a, b, preferred_element_type=F32)


def _fused_linear(xs, ws, *, tm, epilogue, out_shapes, out_specs, gain=None, wspecs=None,
                  row_extras=(), const_extras=(), vmem_mb=48, name="linear"):
    T = xs[0].shape[0]
    grid = (T // tm,)
    n_x, n_w, n_r, n_c = len(xs), len(ws), len(row_extras), len(const_extras)

    def kern(*refs):
        pos = 0
        x_refs = refs[pos:pos + n_x]; pos += n_x
        gain_ref = None
        if gain is not None:
            gain_ref = refs[pos]; pos += 1
        w_refs = refs[pos:pos + n_w]; pos += n_w
        r_refs = refs[pos:pos + n_r]; pos += n_r
        c_refs = refs[pos:pos + n_c]; pos += n_c
        o_refs = refs[pos:]
        acc = None
        x0 = None
        for xr, wr in zip(x_refs, w_refs):
            xv = xr[...]
            if gain_ref is not None:
                x0 = xv
                xv = _rms(xv, gain_ref[...]).astype(BF16)
            d = _dot(xv, wr[...])
            acc = d if acc is None else acc + d
        epilogue(acc, x0, r_refs, c_refs, o_refs)

    in_specs = [pl.BlockSpec((tm, x.shape[1]), lambda i: (i, 0)) for x in xs]
    args = list(xs)
    if gain is not None:
        in_specs.append(pl.BlockSpec(gain.shape, lambda i: (0, 0)))
        args.append(gain)
    if wspecs is None:
        wspecs = [pl.BlockSpec(w.shape, lambda i: (0,) * w.ndim) for w in ws]
    in_specs += list(wspecs)
    args += list(ws)
    for r in row_extras:
        in_specs.append(pl.BlockSpec((tm, r.shape[1]), lambda i: (i, 0)))
        args.append(r)
    for c in const_extras:
        in_specs.append(pl.BlockSpec(c.shape, lambda i, nd=c.ndim: (0,) * nd))
        args.append(c)
    return pl.pallas_call(
        kern, grid=grid, in_specs=in_specs, out_specs=out_specs, out_shape=out_shapes,
        compiler_params=_cparams(vmem_mb), name=name)(*args)


def _even_in_proj(h, gain, w_ext, qn_g, kvn_g, cos_rows, sin_rows, dims, with_ckv_t):
    DP, QL, KL, ROPE = dims
    T, D = h.shape
    tm = _tile(T, LINEAR_TM)
    o_kr = DP + QL + KL

    def epilogue(acc, x0, r_refs, c_refs, o_refs):
        u_ref, qn_ref, ckv_ref, kpe_ref, kcat_ref = o_refs[:5]
        cos, sin = r_refs[0][...], r_refs[1][...]
        u_ref[...] = acc[:, :DP]
        qn_ref[...] = _rms(acc[:, DP:DP + QL], c_refs[0][...]).astype(BF16)
        ckv = _rms(acc[:, DP + QL:o_kr], c_refs[1][...])
        ckv_ref[...] = ckv
        kpe = acc[:, o_kr:o_kr + LANES] * cos + acc[:, o_kr + LANES:o_kr + 2 * LANES] * sin
        kpe_ref[...] = kpe[:, :ROPE]
        kcat_ref[:, :KL] = ckv.astype(BF16)
        kcat_ref[:, KL:] = kpe.astype(BF16)
        if with_ckv_t:
            o_refs[5][...] = ckv.T.astype(BF16)

    out_shapes = (jax.ShapeDtypeStruct((T, DP), F32), jax.ShapeDtypeStruct((T, QL), BF16),
                  jax.ShapeDtypeStruct((T, KL), F32), jax.ShapeDtypeStruct((T, ROPE), F32),
                  jax.ShapeDtypeStruct((T, KL + LANES), BF16))
    out_specs = tuple(pl.BlockSpec((tm, s.shape[1]), lambda i: (i, 0)) for s in out_shapes)
    if with_ckv_t:
        out_shapes += (jax.ShapeDtypeStruct((KL, T), BF16),)
        out_specs += (pl.BlockSpec((KL, tm), lambda i: (0, i)),)
    return _fused_linear([h], [w_ext], tm=tm, gain=gain, epilogue=epilogue,
                         out_shapes=out_shapes, out_specs=out_specs,
                         row_extras=(cos_rows, sin_rows), const_extras=(qn_g, kvn_g),
                         name="even_in_proj")


def _q_proj(qn, wq_ext, wuk_t, cos_rows, sin_rows, H, KL, scale, head_major):
    T = qn.shape[0]
    tm = _tile(T, LINEAR_TM)
    QW = KL + LANES
    r0, r1 = H * NOPE_DIM, 2 * H * NOPE_DIM

    def epilogue(acc, x0, r_refs, c_refs, o_refs):
        (o_ref,) = o_refs
        cos, sin = r_refs[0][...], r_refs[1][...]
        wuk_ref = c_refs[0]
        for h in range(H):
            qn_h = acc[:, h * NOPE_DIM:(h + 1) * NOPE_DIM].astype(BF16)
            q_abs = (_dot(qn_h, wuk_ref[h]) * scale).astype(BF16)
            q_pe = ((acc[:, r0 + h * LANES:r0 + (h + 1) * LANES] * cos
                     + acc[:, r1 + h * LANES:r1 + (h + 1) * LANES] * sin) * scale).astype(BF16)
            if head_major:
                o_ref[h, :, :KL] = q_abs
                o_ref[h, :, KL:] = q_pe
            else:
                o_ref[:, h * QW:h * QW + KL] = q_abs
                o_ref[:, h * QW + KL:(h + 1) * QW] = q_pe

    if head_major:
        out_shape = jax.ShapeDtypeStruct((H, T, QW), BF16)
        out_spec = pl.BlockSpec((H, tm, QW), lambda i: (0, i, 0))
    else:
        out_shape = jax.ShapeDtypeStruct((T, H * QW), BF16)
        out_spec = pl.BlockSpec((tm, H * QW), lambda i: (i, 0))
    (out,) = _fused_linear([qn], [wq_ext], tm=tm, epilogue=epilogue, out_shapes=(out_shape,),
                           out_specs=(out_spec,), row_extras=(cos_rows, sin_rows),
                           const_extras=(wuk_t,), name="q_proj")
    return out


def _pool_group_out(ext_ref, base, tl, pos0, g, w, PG, pw_ref, scale_ref):
    lo, hi = g * PG, (g + 1) * PG
    x = ext_ref[base:base + tl, lo:hi]
    win = x
    for j in range(1, w):
        win = win + ext_ref[base - j:base - j + tl, lo:hi]
    if pos0 is None:
        d = win * (1.0 / w) - x
    else:
        pos = pos0 + lax.broadcasted_iota(jnp.int32, (tl, 1), 0)
        cnt = jnp.minimum(pos + 1, w).astype(F32)
        d = win / cnt - x
    return _dot(d.astype(BF16), pw_ref[g]) * scale_ref[:, lo:hi]


def _pool_prompt(u, past, pool_w, pool_scale, B, L):
    DP = u.shape[1]
    PS = past.shape[1]
    HALO = 16
    G = len(POOL_WINDOWS)
    PG = DP // G
    tl = _tile(L, POOL_TL, HALO)
    nl = L // tl

    def kern(u_ref, halo_ref, past_ref, pw_ref, sc_ref, o_ref, new_ref, ext_ref):
        i = pl.program_id(1)

        @pl.when(i == 0)
        def _():
            ext_ref[0:1, :] = jnp.zeros((1, DP), F32)
            ext_ref[HALO - PS:HALO, :] = past_ref[0]

        @pl.when(i > 0)
        def _():
            ext_ref[0:HALO, :] = halo_ref[...]

        ext_ref[HALO:HALO + tl, :] = u_ref[...]
        for g, w in enumerate(POOL_WINDOWS):
            o_ref[:, g * PG:(g + 1) * PG] = _pool_group_out(
                ext_ref, HALO, tl, i * tl, g, w, PG, pw_ref, sc_ref).astype(BF16)

        @pl.when(i == nl - 1)
        def _():
            new_ref[0] = ext_ref[HALO + tl - PS:HALO + tl, :]

    r = tl // HALO
    return pl.pallas_call(
        kern, grid=(B, nl),
        in_specs=[
            pl.BlockSpec((tl, DP), lambda b, i: (b * nl + i, 0)),
            pl.BlockSpec((HALO, DP), lambda b, i: (jnp.maximum((b * nl + i) * r - 1, 0), 0)),
            pl.BlockSpec((1, PS, DP), lambda b, i: (b, 0, 0)),
            pl.BlockSpec(pool_w.shape, lambda b, i: (0, 0, 0)),
            pl.BlockSpec(pool_scale.shape, lambda b, i: (0, 0)),
        ],
        out_specs=(pl.BlockSpec((tl, DP), lambda b, i: (b * nl + i, 0)),
                   pl.BlockSpec((1, PS, DP), lambda b, i: (b, 0, 0))),
        out_shape=(jax.ShapeDtypeStruct((B * L, DP), BF16), jax.ShapeDtypeStruct((B, PS, DP), F32)),
        scratch_shapes=[pltpu.VMEM((HALO + tl, DP), F32)],
        compiler_params=_cparams(40), name="pool_prompt")(u, u, past, pool_w, pool_scale)


def _pool_sample(ext_tm, pool_w, pool_scale, Ld, start):
    R, DB, DP = ext_tm.shape
    PS = R - Ld
    G = len(POOL_WINDOWS)
    PG = DP // G

    def kern(e_ref, pw_ref, sc_ref, o_ref):
        for t in range(Ld):
            for g, w in enumerate(POOL_WINDOWS):
                lo, hi = g * PG, (g + 1) * PG
                x = e_ref[PS + t, :, lo:hi]
                win = x
                for j in range(1, w):
                    win = win + e_ref[PS + t - j, :, lo:hi]
                cnt = float(min(start + t + 1, w))
                d = win / cnt - x
                o_ref[t, :, lo:hi] = (_dot(d.astype(BF16), pw_ref[g]) * sc_ref[:, lo:hi]).astype(BF16)

    return pl.pallas_call(
        kern, grid=(1,),
        in_specs=[pl.BlockSpec(ext_tm.shape, lambda i: (0, 0, 0)),
                  pl.BlockSpec(pool_w.shape, lambda i: (0, 0, 0)),
                  pl.BlockSpec(pool_scale.shape, lambda i: (0, 0))],
        out_specs=pl.BlockSpec((Ld, DB, DP), lambda i: (0, 0, 0)),
        out_shape=jax.ShapeDtypeStruct((Ld, DB, DP), BF16),
        compiler_params=_cparams(48), name="pool_sample")(ext_tm, pool_w, pool_scale)


def _flash_prompt(qcat, kcat, ckv_t, wuv, B, L, H, KL):
    QW = qcat.shape[2]
    tq = _tile(L, FLASH_TQ, LANES)
    tk = _tile(L, FLASH_TK, LANES)
    assert tk % tq == 0 and tq % LANES == 0
    nq, nk = L // tq, L // tk
    R = H * tq

    def last_needed(qi):
        return (qi * tq + tq - 1) // tk

    pairs = [(qi, ki) for qi in range(nq) for ki in range(last_needed(qi) + 1)]
    qi_tab = jnp.asarray([p[0] for p in pairs], jnp.int32)
    ki_tab = jnp.asarray([p[1] for p in pairs], jnp.int32)

    def kern(qi_ref, ki_ref, q_ref, k_ref, kt_ref, wuv_ref, o_ref, m_sc, l_sc, acc_sc):
        qi, ki = qi_ref[pl.program_id(1)], ki_ref[pl.program_id(1)]

        @pl.when(ki == 0)
        def _():
            m_sc[...] = jnp.full((1, R), NEG_BIG, F32)
            l_sc[...] = jnp.zeros((1, R), F32)
            acc_sc[...] = jnp.zeros((KL, R), F32)

        def update(masked, nkeys):
            q = q_ref[...].reshape(R, QW)
            s = lax.dot_general(k_ref[0:nkeys, :], q, NT_DIMS, preferred_element_type=F32)
            if masked:
                kpos = ki * tk + lax.broadcasted_iota(jnp.int32, (nkeys, R), 0)
                qpos = qi * tq + (lax.broadcasted_iota(jnp.int32, (nkeys, R), 1) % tq)
                s = jnp.where(kpos <= qpos, s, NEG_BIG)
            m_prev = m_sc[...]
            m_new = jnp.maximum(m_prev, jnp.max(s, axis=0, keepdims=True))
            alpha = jnp.exp2(m_prev - m_new)
            p = jnp.exp2(s - m_new)
            l_sc[...] = alpha * l_sc[...] + jnp.sum(p, axis=0, keepdims=True)
            acc_sc[...] = alpha * acc_sc[...] + _dot(kt_ref[:, 0:nkeys], p.astype(BF16))
            m_sc[...] = m_new

        pl.when(ki < last_needed(qi))(functools.partial(update, False, tk))
        for v in range(tk // tq):
            pl.when((ki == last_needed(qi)) & (qi % (tk // tq) == v))(
                functools.partial(update, True, (v + 1) * tq))

        @pl.when(ki == last_needed(qi))
        def _():
            ctx_t = (acc_sc[...] / l_sc[...]).astype(BF16)
            for h in range(H):
                o_ref[:, h * V_DIM:(h + 1) * V_DIM] = lax.dot_general(
                    ctx_t[:, h * tq:(h + 1) * tq], wuv_ref[h], TN_DIMS,
                    preferred_element_type=F32).astype(BF16)

    grid_spec = pltpu.PrefetchScalarGridSpec(
        num_scalar_prefetch=2, grid=(B, len(pairs)),
        in_specs=[
            pl.BlockSpec((H, tq, QW), lambda b, s, qt, kt: (0, b * nq + qt[s], 0)),
            pl.BlockSpec((tk, QW), lambda b, s, qt, kt: (b * nk + kt[s], 0)),
            pl.BlockSpec((KL, tk), lambda b, s, qt, kt: (0, b * nk + kt[s])),
            pl.BlockSpec(wuv.shape, lambda b, s, qt, kt: (0, 0, 0)),
        ],
        out_specs=pl.BlockSpec((tq, H * V_DIM), lambda b, s, qt, kt: (b * nq + qt[s], 0)),
        scratch_shapes=[pltpu.VMEM((1, R), F32), pltpu.VMEM((1, R), F32), pltpu.VMEM((KL, R), F32)])
    return pl.pallas_call(
        kern, grid_spec=grid_spec, out_shape=jax.ShapeDtypeStruct((B * L, H * V_DIM), BF16),
        compiler_params=_cparams(56), name="flash_prompt")(qi_tab, ki_tab, qcat, kcat, ckv_t, wuv)


def _decode_attention(page_table, q_s, knew, cache_ckv, cache_krope_t, a, H, Ld, KL, ROPE):
    DB, R, QW = q_s.shape
    KN = knew.shape[1]
    n_pages = page_table.shape[1]
    PG = cache_ckv.shape[2]
    PP = DECODE_PAGES_PER_STEP
    while n_pages % PP:
        PP //= 2
    nj = n_pages // PP
    SB = DECODE_SEQS_PER_STEP if DB % DECODE_SEQS_PER_STEP == 0 else 1
    NP = SB * PP
    n_steps = (DB // SB) * nj

    def kern(pt_ref, q_ref, kn_ref, ck_hbm, kr_hbm, o_ref, ckbuf, krbuf, sems, m_sc, l_sc, acc_sc):
        bi, j = pl.program_id(0), pl.program_id(1)
        n = bi * nj + j

        def page_copies(step, i):
            sb, p = i // PP, i % PP
            seq = (step // nj) * SB + sb
            page = pt_ref[seq * n_pages + (step % nj) * PP + p]
            slot = step % 2
            return (pltpu.make_async_copy(ck_hbm.at[a, page], ckbuf.at[slot, i], sems.at[slot]),
                    pltpu.make_async_copy(kr_hbm.at[a, page], krbuf.at[slot, i], sems.at[slot]))

        def for_pages(step, fn):
            def body(i, c):
                for cp in page_copies(step, i):
                    fn(cp)
                return c
            lax.fori_loop(0, NP, body, 0, unroll=DMA_LOOP_UNROLL)

        @pl.when(n == 0)
        def _():
            for_pages(0, lambda cp: cp.start())

        @pl.when(n + 1 < n_steps)
        def _():
            for_pages(n + 1, lambda cp: cp.start())

        @pl.when(j == 0)
        def _():
            m_sc[...] = jnp.full((SB, R, 1), NEG_BIG, F32)
            l_sc[...] = jnp.zeros((SB, R, 1), F32)
            acc_sc[...] = jnp.zeros((SB, R, KL), F32)

        for_pages(n, lambda cp: cp.wait())
        slot = n % 2

        def update(sb, s, v):
            m_prev = m_sc[sb]
            m_new = jnp.maximum(m_prev, jnp.max(s, axis=-1, keepdims=True))
            alpha = jnp.exp2(m_prev - m_new)
            p = jnp.exp2(s - m_new)
            l_sc[sb] = alpha * l_sc[sb] + jnp.sum(p, axis=-1, keepdims=True)
            acc_sc[sb] = alpha * acc_sc[sb] + _dot(p.astype(BF16), v)
            m_sc[sb] = m_new

        for sb in range(SB):
            q = q_ref[sb]
            ck = ckbuf[slot, sb * PP:(sb + 1) * PP].reshape(PP * PG, KL).astype(BF16)
            kr_t = jnp.concatenate([krbuf[slot, sb * PP + p].astype(BF16) for p in range(PP)], axis=1)
            s = (lax.dot_general(q[:, :KL], ck, NT_DIMS, preferred_element_type=F32)
                 + _dot(q[:, KL:KL + ROPE], kr_t))
            update(sb, s, ck)

        @pl.when(j == nj - 1)
        def _():
            for sb in range(SB):
                q, kn = q_ref[sb], kn_ref[sb]
                s = lax.dot_general(q, kn, NT_DIMS, preferred_element_type=F32)
                t_idx = lax.broadcasted_iota(jnp.int32, (R, KN), 0) // H
                s_idx = lax.broadcasted_iota(jnp.int32, (R, KN), 1)
                update(sb, jnp.where(s_idx <= t_idx, s, NEG_BIG), kn[:, :KL])
                o_ref[sb] = (acc_sc[sb] / l_sc[sb]).astype(BF16)

    grid_spec = pltpu.PrefetchScalarGridSpec(
        num_scalar_prefetch=1, grid=(DB // SB, nj),
        in_specs=[pl.BlockSpec((SB, R, QW), lambda b, j, pt: (b, 0, 0)),
                  pl.BlockSpec((SB, KN, QW), lambda b, j, pt: (b, 0, 0)),
                  pl.BlockSpec(memory_space=pl.ANY),
                  pl.BlockSpec(memory_space=pl.ANY)],
        out_specs=pl.BlockSpec((SB, R, KL), lambda b, j, pt: (b, 0, 0)),
        scratch_shapes=[pltpu.VMEM((2, NP, PG, KL), F32), pltpu.VMEM((2, NP, ROPE, PG), F32),
                        pltpu.SemaphoreType.DMA((2,)),
                        pltpu.VMEM((SB, R, 1), F32), pltpu.VMEM((SB, R, 1), F32),
                        pltpu.VMEM((SB, R, KL), F32)])
    return pl.pallas_call(
        kern, grid_spec=grid_spec, out_shape=jax.ShapeDtypeStruct((DB, R, KL), BF16),
        compiler_params=_cparams(48), name="decode_attention")(
            page_table.reshape(-1), q_s, knew, cache_ckv, cache_krope_t)


def _ctx_to_attn_out(ctx, wuv, H, KL):
    T = ctx.shape[0]
    tm = _tile(T, LINEAR_TM)

    def kern(c_ref, w_ref, o_ref):
        for h in range(H):
            o_ref[:, h * V_DIM:(h + 1) * V_DIM] = _dot(
                c_ref[:, h * KL:(h + 1) * KL], w_ref[h]).astype(BF16)

    return pl.pallas_call(
        kern, grid=(T // tm,),
        in_specs=[pl.BlockSpec((tm, H * KL), lambda i: (i, 0)),
                  pl.BlockSpec(wuv.shape, lambda i: (0, 0, 0))],
        out_specs=pl.BlockSpec((tm, H * V_DIM), lambda i: (i, 0)),
        out_shape=jax.ShapeDtypeStruct((T, H * V_DIM), BF16),
        compiler_params=_cparams(40), name="ctx_to_attn_out")(ctx, wuv)


def _linear_residual(xs, ws, res, name):
    T, D = res.shape
    tm = _tile(T, LINEAR_TM)

    def epilogue(acc, x0, r_refs, c_refs, o_refs):
        o_refs[0][...] = r_refs[0][...] + acc

    (out,) = _fused_linear(xs, ws, tm=tm, epilogue=epilogue,
                           out_shapes=(jax.ShapeDtypeStruct((T, D), F32),),
                           out_specs=(pl.BlockSpec((tm, D), lambda i: (i, 0)),),
                           row_extras=(res,), name=name)
    return out


def _ffn(h, gain, w_gu, w_down):
    T, D = h.shape
    F = w_down.shape[0]
    tm = _tile(T, FFN_TM)
    tf = _tile(F, FFN_TF, LANES)
    nf = F // tf

    def kern(x_ref, g_ref, wg_ref, wu_ref, wd_ref, o_ref, xn_sc):
        f = pl.program_id(1)

        @pl.when(f == 0)
        def _():
            x = x_ref[...]
            xn_sc[...] = _rms(x, g_ref[...]).astype(BF16)
            o_ref[...] = x

        xn = xn_sc[...]
        act = (_silu(_dot(xn, wg_ref[...])) * _dot(xn, wu_ref[...])).astype(BF16)
        o_ref[...] += _dot(act, wd_ref[...])

    return pl.pallas_call(
        kern, grid=(T // tm, nf),
        in_specs=[pl.BlockSpec((tm, D), lambda i, f: (i, 0)),
                  pl.BlockSpec(gain.shape, lambda i, f: (0, 0)),
                  pl.BlockSpec((D, tf), lambda i, f: (0, f)),
                  pl.BlockSpec((D, tf), lambda i, f: (0, nf + f)),
                  pl.BlockSpec((tf, D), lambda i, f: (f, 0))],
        out_specs=pl.BlockSpec((tm, D), lambda i, f: (i, 0)),
        out_shape=jax.ShapeDtypeStruct((T, D), F32),
        scratch_shapes=[pltpu.VMEM((tm, D), BF16)],
        compiler_params=_cparams(48), name="ffn")(h, gain, w_gu, w_gu, w_down)


def _hgrn_in_proj(h, gain, w_in, lb):
    T, D = h.shape
    HK = w_in.shape[1] // 4
    tm = _tile(T, LINEAR_TM)

    def call(col, epilogue, dtypes, extras=()):
        out_shapes = tuple(jax.ShapeDtypeStruct((T, HK), dt) for dt in dtypes)
        out_specs = tuple(pl.BlockSpec((tm, HK), lambda i: (i, 0)) for _ in dtypes)
        return _fused_linear([h], [w_in], tm=tm, gain=gain, epilogue=epilogue,
                             wspecs=[pl.BlockSpec((D, HK), lambda i: (0, col))],
                             out_shapes=out_shapes, out_specs=out_specs, const_extras=extras,
                             name=f"hgrn_in_proj_{col}")

    def ep_silu(acc, x0, r_refs, c_refs, o_refs):
        o_refs[0][...] = _silu(acc).astype(BF16)

    def ep_ident(acc, x0, r_refs, c_refs, o_refs):
        o_refs[0][...] = acc.astype(BF16)

    def ep_gate(acc, x0, r_refs, c_refs, o_refs):
        k = (1.0 - c_refs[0][...]) / (1.0 + jnp.exp(acc))
        o_refs[0][...] = k.astype(BF16)
        o_refs[1][...] = jnp.log1p(-k)

    (qs,) = call(0, ep_silu, (BF16,))
    kk, logf = call(1, ep_gate, (BF16, F32), extras=(lb,))
    (vv,) = call(2, ep_ident, (BF16,))
    (gs,) = call(3, ep_silu, (BF16,))
    return qs, kk, logf, vv, gs


def _gla_band(qf, kf, vf, fg, n_diag, lane_sum):
    C = qf.shape[0]
    r8 = lax.broadcasted_iota(jnp.int32, qf.shape, 0) % SUBLANES
    gprod = None
    parts = []
    for d in range(n_diag):
        if d == 0:
            p = qf * kf
        else:
            fr = fg if d == 1 else pltpu.roll(fg, d - 1, 0)
            gprod = fr if gprod is None else gprod * fr
            p = jnp.where(r8 >= d, qf * gprod * pltpu.roll(kf, d, 0), 0.0)
        parts.append(p)
    sums = lane_sum(parts)
    out = None
    for d in range(n_diag):
        vr = vf if d == 0 else pltpu.roll(vf, d, 0)
        term = sums[d] * vr
        out = term if out is None else out + term
    return out


def _gla_prompt(qs, kk, logf, vv, gs, hg_norm, B, L, H):
    T, HK = qs.shape
    K = HK // H
    ct = _tile(L, GLA_CT, GLA_CHUNK)
    C = min(GLA_CHUNK, ct)
    nct = L // ct
    ncc = ct // C
    HPS = GLA_HEADS_PER_STEP if H % GLA_HEADS_PER_STEP == 0 else 1

    def kern(q_ref, k_ref, lf_ref, v_ref, g_ref, hn_ref, o_ref, s_ref, st_sc):
        ci = pl.program_id(2)

        @pl.when(ci == 0)
        def _():
            st_sc[...] = jnp.zeros((HPS, K, K), F32)

        rows = lax.broadcasted_iota(jnp.int32, (C, C), 0)
        cols = lax.broadcasted_iota(jnp.int32, (C, C), 1)
        tril = (rows >= cols).astype(F32)
        ones_bf = jnp.ones((K, K), BF16)

        def lane_sum(parts):
            r = _dot(jnp.concatenate([p.astype(BF16) for p in parts], axis=0), ones_bf)
            return [r[d * C:(d + 1) * C] for d in range(len(parts))]

        def chunk(c, hh, st):
            sl, hl = slice(c * C, (c + 1) * C), slice(hh * K, (hh + 1) * K)
            qf, kf, vf = q_ref[sl, hl].astype(F32), k_ref[sl, hl].astype(F32), v_ref[sl, hl].astype(F32)
            lf = lf_ref[sl, hl]
            fg = jnp.exp(lf)
            b = jnp.dot(tril, lf, precision=lax.Precision.HIGHEST, preferred_element_type=F32)
            blast = b[C - 1:C, :]
            qb = qf * jnp.exp(b)
            kb = kf * jnp.exp(blast - b)
            o = lax.dot_general(qb.astype(BF16), st.astype(BF16), NT_DIMS, preferred_element_type=F32)
            amat = None
            blk = C // 2
            while blk >= SUBLANES:
                q_parts, k_parts = [], []
                for m in range(C // blk):
                    r = slice(m * blk, (m + 1) * blk)
                    if m % 2 == 1:
                        ref = b[m * blk - 1:m * blk, :]
                        q_parts.append(qf[r] * jnp.exp(b[r] - ref))
                        k_parts.append(jnp.zeros((blk, K), F32))
                    else:
                        ref = b[(m + 1) * blk - 1:(m + 1) * blk, :]
                        q_parts.append(jnp.zeros((blk, K), F32))
                        k_parts.append(kf[r] * jnp.exp(ref - b[r]))
                ql = jnp.concatenate(q_parts, axis=0).astype(BF16)
                kl = jnp.concatenate(k_parts, axis=0).astype(BF16)
                al = lax.dot_general(ql, kl, NT_DIMS, preferred_element_type=F32)
                al = jnp.where((rows // (2 * blk)) == (cols // (2 * blk)), al, 0.0)
                amat = al if amat is None else amat + al
                blk //= 2
            if amat is not None:
                o = o + _dot(amat.astype(BF16), vf.astype(BF16))
            o = o + _gla_band(qf, kf, vf, fg, min(SUBLANES, C), lane_sum)
            st = st * jnp.exp(blast) + lax.dot_general(
                vf.astype(BF16), kb.astype(BF16), TN_DIMS, preferred_element_type=F32)
            on = _rms(o, hn_ref[...]) * g_ref[sl, hl].astype(F32)
            o_ref[sl, hl] = on.astype(BF16)
            return st

        sts = [st_sc[hh] for hh in range(HPS)]
        for c in range(ncc):
            sts = [chunk(c, hh, sts[hh]) for hh in range(HPS)]
        for hh in range(HPS):
            st_sc[hh] = sts[hh]

        @pl.when(ci == nct - 1)
        def _():
            for hh in range(HPS):
                s_ref[0, hh] = sts[hh].T

    tok = lambda b, h, ci: (b * nct + ci, h)
    return pl.pallas_call(
        kern, grid=(B, H // HPS, nct),
        in_specs=[pl.BlockSpec((ct, HPS * K), tok)] * 5 + [pl.BlockSpec(hg_norm.shape, lambda b, h, ci: (0, 0))],
        out_specs=(pl.BlockSpec((ct, HPS * K), tok),
                   pl.BlockSpec((1, HPS, K, K), lambda b, h, ci: (b, h, 0, 0))),
        out_shape=(jax.ShapeDtypeStruct((T, HK), BF16), jax.ShapeDtypeStruct((B, H, K, K), F32)),
        scratch_shapes=[pltpu.VMEM((HPS, K, K), F32)],
        compiler_params=_cparams(32), name="gla_prompt")(qs, kk, logf, vv, gs, hg_norm)


def _gla_sample(qs, kk, logf, vv, gs, hg_norm, s0, DB, Ld, H):
    HK = qs.shape[2]
    K = HK // H
    C = SUBLANES
    assert Ld <= C

    def kern(q_ref, k_ref, lf_ref, v_ref, g_ref, hn_ref, s0_ref, o_ref, s_ref, pad_sc):
        rows = lax.broadcasted_iota(jnp.int32, (C, C), 0)
        cols = lax.broadcasted_iota(jnp.int32, (C, C), 1)
        tril = (rows >= cols).astype(F32)

        def padded(ref, slot):
            pad_sc[slot] = jnp.zeros((C, HK), F32)
            pad_sc[slot, 0:Ld, :] = ref[0].astype(F32)
            return pad_sc[slot]

        qf, kf, lf, vf = padded(q_ref, 0), padded(k_ref, 1), padded(lf_ref, 2), padded(v_ref, 3)
        fg = jnp.exp(lf)
        b = jnp.dot(tril, lf, precision=lax.Precision.HIGHEST, preferred_element_type=F32)
        blast = b[C - 1:C, :]
        qb = (qf * jnp.exp(b)).astype(BF16)
        kb = (kf * jnp.exp(blast - b)).astype(BF16)
        dec = jnp.exp(blast)
        vb = vf.astype(BF16)

        def head_sum(parts):
            return [jnp.sum(p, axis=-1, keepdims=True) for p in parts]

        eye = (lax.broadcasted_iota(jnp.int32, (K, K), 0) == lax.broadcasted_iota(jnp.int32, (K, K), 1))
        for h in range(H):
            hl = slice(h * K, (h + 1) * K)
            s0 = s0_ref[0, h]
            o = _dot(qb[:, hl], s0.astype(BF16))
            o = o + _gla_band(qf[:, hl], kf[:, hl], vf[:, hl], fg[:, hl], Ld, head_sum)
            dcol = jnp.sum(jnp.where(eye, jnp.broadcast_to(dec[:, hl], (K, K)), 0.0), axis=-1, keepdims=True)
            s_ref[0, h] = s0 * dcol + lax.dot_general(kb[:, hl], vb[:, hl], TN_DIMS,
                                                      preferred_element_type=F32)
            on = _rms(o[0:Ld], hn_ref[...]) * g_ref[0, :, hl].astype(F32)
            o_ref[0, :, hl] = on.astype(BF16)

    tok = pl.BlockSpec((1, Ld, HK), lambda b: (b, 0, 0))
    st_spec = pl.BlockSpec((1, H, K, K), lambda b: (b, 0, 0, 0))
    return pl.pallas_call(
        kern, grid=(DB,),
        in_specs=[tok] * 5 + [pl.BlockSpec(hg_norm.shape, lambda b: (0, 0)), st_spec],
        out_specs=(tok, st_spec),
        out_shape=(jax.ShapeDtypeStruct((DB, Ld, HK), BF16), jax.ShapeDtypeStruct(s0.shape, F32)),
        scratch_shapes=[pltpu.VMEM((4, C, HK), F32)],
        compiler_params=_cparams(32), name="gla_sample")(qs, kk, logf, vv, gs, hg_norm, s0)


def _router(h, gain, w_router, b_router):
    T, D = h.shape
    E = w_router.shape[1]
    tm = _tile(T, LINEAR_TM)
    assert TOP_K == 2

    def kern(x_ref, g_ref, w_ref, b_ref, i_ref, o_ref):
        xn = _rms(x_ref[...], g_ref[...])
        logits = jnp.dot(xn, w_ref[...], precision=lax.Precision.HIGHEST,
                         preferred_element_type=F32) + b_ref[...]
        idx = lax.broadcasted_iota(jnp.int32, logits.shape, 1)
        m1 = jnp.max(logits, axis=-1, keepdims=True)
        i1 = jnp.min(jnp.where(logits == m1, idx, E), axis=-1, keepdims=True)
        rest = jnp.where(idx == i1, -jnp.inf, logits)
        m2 = jnp.max(rest, axis=-1, keepdims=True)
        i2 = jnp.min(jnp.where(rest == m2, idx, E), axis=-1, keepdims=True)
        e2 = jnp.exp(m2 - m1)
        g1 = 1.0 / (1.0 + e2)
        slot = lax.broadcasted_iota(jnp.int32, (tm, TOP_K), 1)
        i_ref[...] = jnp.where(slot == 0, i1, i2)
        o_ref[...] = jnp.where(slot == 0, g1, e2 * g1)

    return pl.pallas_call(
        kern, grid=(T // tm,),
        in_specs=[pl.BlockSpec((tm, D), lambda i: (i, 0)),
                  pl.BlockSpec(gain.shape, lambda i: (0, 0)),
                  pl.BlockSpec(w_router.shape, lambda i: (0, 0)),
                  pl.BlockSpec(b_router.shape, lambda i: (0, 0))],
        out_specs=(pl.BlockSpec((tm, TOP_K), lambda i: (i, 0)), pl.BlockSpec((tm, TOP_K), lambda i: (i, 0))),
        out_shape=(jax.ShapeDtypeStruct((T, TOP_K), jnp.int32), jax.ShapeDtypeStruct((T, TOP_K), F32)),
        compiler_params=_cparams(32), name="router")(h, gain, w_router, b_router)


def _route_plan(idx, E, tmx):
    T = idx.shape[0]
    NP = TOP_K * T + E * tmx
    assert NP % tmx == 0
    NT = NP // tmx
    sel = jnp.any(idx[:, :, None] == jnp.arange(E, dtype=jnp.int32), axis=1).astype(jnp.int32)
    incl = jnp.cumsum(sel, axis=0)
    counts = incl[-1]
    padded = ((counts + tmx - 1) // tmx) * tmx
    ends = jnp.cumsum(padded)
    offs = ends - padded
    rank = jnp.take_along_axis(incl - sel, idx, axis=1)
    pos = (offs[idx] + rank).reshape(-1).astype(jnp.int32)
    tok = jnp.repeat(jnp.arange(T, dtype=jnp.int32), TOP_K)
    src = jnp.zeros((NP,), jnp.int32).at[pos].set(tok)
    n_used = (ends[-1] // tmx).astype(jnp.int32)
    tile_start = jnp.minimum(jnp.arange(NT, dtype=jnp.int32), n_used - 1) * tmx
    tile_expert = jnp.minimum(jnp.sum(tile_start[:, None] >= ends[None, :], axis=1), E - 1).astype(jnp.int32)
    return pos, src, tile_expert, n_used.reshape(1)


def _moe_experts(h, gain, src, tile_expert, n_used, w_gu, w_down, tmx):
    T, D = h.shape
    E, F = w_down.shape[0], w_down.shape[1]
    NP = src.shape[0]
    NT = NP // tmx
    tf = _tile(F, MOE_TF, LANES)
    nf = F // tf
    n_issue = max(d for d in range(0, min(MOE_GATHER_ISSUE_STEPS, nf - 1) + 1) if d == 0 or tmx % d == 0)
    rows_per_issue = tmx // n_issue if n_issue else 0

    def kern(src_ref, te_ref, nu_ref, h_ref, g_ref, wg_ref, wu_ref, wd_ref, o_ref, xbuf, xn_sc, sem):
        i, f = pl.program_id(0), pl.program_id(1)
        n_used_v = nu_ref[0]

        def row_copy(tile, r):
            return pltpu.make_async_copy(h_ref.at[pl.ds(src_ref[tile * tmx + r], 1)],
                                         xbuf.at[pl.ds(r, 1)], sem)

        def start_gather(tile):
            lax.fori_loop(0, tmx, lambda r, c: (row_copy(tile, r).start(), c)[1], 0, unroll=DMA_LOOP_UNROLL)

        @pl.when((f == 0) & (i == 0))
        def _():
            start_gather(0)

        @pl.when((f == 0) & (i < n_used_v))
        def _():
            lax.fori_loop(0, tmx, lambda r, c: (row_copy(i, r).wait(), c)[1], 0, unroll=DMA_LOOP_UNROLL)
            xn_sc[...] = _rms(xbuf[...], g_ref[...]).astype(BF16)
            if n_issue == 0:
                @pl.when(i + 1 < n_used_v)
                def _():
                    start_gather(i + 1)

        @pl.when(f == 0)
        def _():
            o_ref[...] = jnp.zeros((tmx, D), F32)

        def matmul_step(issue):
            if issue:
                for r in range(rows_per_issue):
                    row_copy(i + 1, (f - 1) * rows_per_issue + r).start()
            xn = xn_sc[...]
            act = (_silu(_dot(xn, wg_ref[0])) * _dot(xn, wu_ref[0])).astype(BF16)
            o_ref[...] += _dot(act, wd_ref[0])

        issue_now = (f >= 1) & (f <= n_issue) & (i + 1 < n_used_v)
        pl.when((i < n_used_v) & issue_now)(functools.partial(matmul_step, True))
        pl.when((i < n_used_v) & jnp.logical_not(issue_now))(functools.partial(matmul_step, False))

    def wspec(shape, fn):
        def index_map(i, f, src_r, te_r, nu_r):
            return fn(te_r[i], jnp.where(i < nu_r[0], f, nf - 1))
        return pl.BlockSpec(shape, index_map)

    grid_spec = pltpu.PrefetchScalarGridSpec(
        num_scalar_prefetch=3, grid=(NT, nf),
        in_specs=[pl.BlockSpec(memory_space=pl.ANY),
                  pl.BlockSpec(gain.shape, lambda i, f, *_: (0, 0)),
                  wspec((1, D, tf), lambda e, f: (e, 0, f)),
                  wspec((1, D, tf), lambda e, f: (e, 0, nf + f)),
                  wspec((1, tf, D), lambda e, f: (e, f, 0))],
        out_specs=pl.BlockSpec((tmx, D), lambda i, f, *_: (i, 0)),
        scratch_shapes=[pltpu.VMEM((tmx, D), F32), pltpu.VMEM((tmx, D), BF16),
                        pltpu.SemaphoreType.DMA(())])
    return pl.pallas_call(
        kern, grid_spec=grid_spec, out_shape=jax.ShapeDtypeStruct((NP, D), F32),
        compiler_params=_cparams(48), name="moe_experts")(
            src, tile_expert, n_used, h, gain, w_gu, w_gu, w_down)


def _moe_combine(h, y_sorted, pos, gate, final_gain):
    T, D = h.shape
    tm = _tile(T, MOE_COMBINE_TM)
    nt = T // tm

    def kern(pos_ref, h_ref, y_ref, gt_ref, fg_ref, o_ref, ybuf, sems):
        i = pl.program_id(0)

        def row_copy(step, r, s):
            slot = step % 2
            return pltpu.make_async_copy(y_ref.at[pl.ds(pos_ref[(step * tm + r) * TOP_K + s], 1)],
                                         ybuf.at[slot, s, pl.ds(r, 1)], sems.at[slot])

        def for_rows(step, fn):
            def body(r, c):
                for s in range(TOP_K):
                    fn(row_copy(step, r, s))
                return c
            lax.fori_loop(0, tm, body, 0, unroll=DMA_LOOP_UNROLL)

        @pl.when(i == 0)
        def _():
            for_rows(0, lambda cp: cp.start())

        @pl.when(i + 1 < nt)
        def _():
            for_rows(i + 1, lambda cp: cp.start())

        for_rows(i, lambda cp: cp.wait())
        slot = i % 2
        gt = gt_ref[...]
        y = h_ref[...]
        for s in range(TOP_K):
            y = y + gt[:, s:s + 1] * ybuf[slot, s]
        o_ref[...] = _rms(y, fg_ref[...])

    grid_spec = pltpu.PrefetchScalarGridSpec(
        num_scalar_prefetch=1, grid=(nt,),
        in_specs=[pl.BlockSpec((tm, D), lambda i, *_: (i, 0)),
                  pl.BlockSpec(memory_space=pl.ANY),
                  pl.BlockSpec((tm, TOP_K), lambda i, *_: (i, 0)),
                  pl.BlockSpec(final_gain.shape, lambda i, *_: (0, 0))],
        out_specs=pl.BlockSpec((tm, D), lambda i, *_: (i, 0)),
        scratch_shapes=[pltpu.VMEM((2, TOP_K, tm, D), F32), pltpu.SemaphoreType.DMA((2,))])
    return pl.pallas_call(
        kern, grid_spec=grid_spec, out_shape=jax.ShapeDtypeStruct((T, D), F32),
        compiler_params=_cparams(40), name="moe_combine")(pos, h, y_sorted, gate, final_gain)


def _moe(h, gain, w_router, b_router, w_gu, w_down, final_gain):
    T = h.shape[0]
    E = w_down.shape[0]
    tmx = _tile(max(MOE_MIN_TILE, 2 * TOP_K * T // E), MOE_TM)
    idx, gate = _router(h, gain, w_router, b_router)
    pos, src, tile_expert, n_used = _route_plan(idx, E, tmx)
    y_sorted = _moe_experts(h, gain, src, tile_expert, n_used, w_gu, w_down, tmx)
    return _moe_combine(h, y_sorted, pos, gate, final_gain)


def _rope_rows(pos, rope_dim):
    inv = ROPE_THETA ** (-jnp.arange(0, rope_dim, 2, dtype=F32) / rope_dim)
    ang = pos.astype(F32)[:, None] * inv[None, :]
    z = jnp.zeros((pos.shape[0], LANES - rope_dim), F32)
    cos, sin = jnp.cos(ang), jnp.sin(ang)
    return jnp.concatenate([cos, cos, z], axis=-1), jnp.concatenate([sin, sin, z], axis=-1)


def _rot_cols(w):
    half = w.shape[-1] // 2
    return jnp.concatenate([-w[..., half:], w[..., :half]], axis=-1)


def _pad_lanes(w):
    pad = LANES - w.shape[-1]
    return jnp.pad(w, [(0, 0)] * (w.ndim - 1) + [(0, pad)])


def _prep_even(a, w_in_e, w_q_b, w_kv_b, pool_w, w_out_e, w_ffn_gu, w_ffn_down, dims, H):
    DP, QL, KL, ROPE = dims
    w_in = w_in_e[a]
    k_raw = w_in[:, DP + QL + KL:]
    w_ext = jnp.concatenate([w_in[:, :DP + QL + KL], _pad_lanes(k_raw), _pad_lanes(_rot_cols(k_raw))],
                            axis=-1).astype(BF16)
    wq = w_q_b[a].reshape(QL, H, NOPE_DIM + ROPE)
    wq_rope = wq[..., NOPE_DIM:]
    wq_ext = jnp.concatenate([wq[..., :NOPE_DIM].reshape(QL, H * NOPE_DIM),
                              _pad_lanes(wq_rope).reshape(QL, H * LANES),
                              _pad_lanes(_rot_cols(wq_rope)).reshape(QL, H * LANES)], axis=-1).astype(BF16)
    w_kv = w_kv_b[a].reshape(KL, H, NOPE_DIM + V_DIM)
    wuk_t = jnp.transpose(w_kv[..., :NOPE_DIM], (1, 2, 0)).astype(BF16)
    wuv = jnp.transpose(w_kv[..., NOPE_DIM:], (1, 0, 2)).astype(BF16)
    w_out = w_out_e[a].astype(BF16)
    return dict(w_ext=w_ext, wq_ext=wq_ext, wuk_t=wuk_t, wuv=wuv, pool_w=pool_w[a].astype(BF16),
                w_out_pool=w_out[:DP], w_out_attn=w_out[DP:],
                w_gu=w_ffn_gu[a].astype(BF16), w_down=w_ffn_down[a].astype(BF16))


def kernel(x_prompt, x_sample, cache_ckv, cache_krope, page_table, state_pool, state_hgrn,
           norm_mix_e, w_in_e, q_norm, w_q_b, kv_norm, w_kv_b, pool_w, pool_scale, w_out_e,
           norm_ffn_e, w_ffn_gu, w_ffn_down,
           norm_mix_o, w_in_o, hg_lower_bound, hg_norm, w_out_o, norm_ffn_o, w_router, b_router,
           w_exp_gu, w_exp_down, final_norm):
    B, L, D = x_prompt.shape
    DB, Ld, _ = x_sample.shape
    n_pages = page_table.shape[1]
    PG = cache_ckv.shape[2]
    past_len = n_pages * PG
    KL, ROPE = cache_ckv.shape[3], cache_krope.shape[3]
    DP, PS = state_pool.shape[3], state_pool.shape[2]
    QL = q_norm.shape[1]
    H = w_q_b.shape[2] // (NOPE_DIM + ROPE)
    HG = state_hgrn.shape[2]
    depth = hg_lower_bound.shape[0]
    dims = (DP, QL, KL, ROPE)
    scale = float((NOPE_DIM + ROPE) ** -0.5 * math.log2(math.e))
    cache_krope_t = jnp.swapaxes(cache_krope, 2, 3)
    QW = KL + LANES
    KN = 16
    assert Ld <= KN and ROPE <= LANES

    row = lambda v: v.reshape(1, -1).astype(F32)
    lb_p = jax.nn.softmax(hg_lower_bound.astype(F32), axis=0)
    lower_bounds = jnp.cumsum(lb_p, axis=0) - lb_p[0]

    hp = x_prompt.reshape(B * L, D)
    hs = x_sample.reshape(DB * Ld, D)
    cos_p, sin_p = _rope_rows(jnp.tile(jnp.arange(L), B), ROPE)
    cos_s, sin_s = _rope_rows(jnp.tile(past_len + jnp.arange(Ld), DB), ROPE)

    outs_p = dict(ckv=[], krope=[], pool=[], hgrn=[])
    outs_s = dict(ckv=[], krope=[], pool=[], hgrn=[])
    for l in range(depth):
        a = l // 2
        if l % 2 == 0:
            w = _prep_even(a, w_in_e, w_q_b, w_kv_b, pool_w, w_out_e, w_ffn_gu, w_ffn_down, dims, H)
            g_mix, g_q, g_kv = row(norm_mix_e[a]), row(q_norm[a]), row(kv_norm[a])
            g_ffn, p_scale = row(norm_ffn_e[a]), row(pool_scale[a])

            u, qn, ckv, kpe, kcat, ckv_t = _even_in_proj(hp, g_mix, w["w_ext"], g_q, g_kv, cos_p, sin_p,
                                                         dims, True)
            pool_out, pool_new = _pool_prompt(u, jnp.zeros((B, PS, DP), F32), w["pool_w"], p_scale, B, L)
            qcat = _q_proj(qn, w["wq_ext"], w["wuk_t"], cos_p, sin_p, H, KL, scale, head_major=True)
            attn = _flash_prompt(qcat, kcat, ckv_t, w["wuv"], B, L, H, KL)
            hp = _linear_residual([pool_out, attn], [w["w_out_pool"], w["w_out_attn"]], hp, "out_proj_e")
            hp = _ffn(hp, g_ffn, w["w_gu"], w["w_down"])
            outs_p["ckv"].append(ckv.reshape(B, L, KL))
            outs_p["krope"].append(kpe.reshape(B, L, ROPE))
            outs_p["pool"].append(pool_new)

            u, qn, ckv, kpe, kcat = _even_in_proj(hs, g_mix, w["w_ext"], g_q, g_kv, cos_s, sin_s, dims, False)
            u3 = u.reshape(DB, Ld, DP)
            ext_tm = jnp.transpose(jnp.concatenate([state_pool[a], u3], axis=1), (1, 0, 2))
            pool_tm = _pool_sample(ext_tm, w["pool_w"], p_scale, Ld, past_len)
            pool_out = jnp.transpose(pool_tm, (1, 0, 2)).reshape(DB * Ld, DP)
            qrows = _q_proj(qn, w["wq_ext"], w["wuk_t"], cos_s, sin_s, H, KL, scale, head_major=False)
            q_s = qrows.reshape(DB, Ld * H, QW)
            knew = jnp.pad(kcat.reshape(DB, Ld, QW), ((0, 0), (0, KN - Ld), (0, 0)))
            ctx = _decode_attention(page_table, q_s, knew, cache_ckv, cache_krope_t, a, H, Ld, KL, ROPE)
            attn = _ctx_to_attn_out(ctx.reshape(DB * Ld, H * KL), w["wuv"], H, KL)
            hs = _linear_residual([pool_out, attn], [w["w_out_pool"], w["w_out_attn"]], hs, "out_proj_e")
            hs = _ffn(hs, g_ffn, w["w_gu"], w["w_down"])
            outs_s["ckv"].append(ckv.reshape(DB, Ld, KL))
            outs_s["krope"].append(kpe.reshape(DB, Ld, ROPE))
            outs_s["pool"].append(jnp.concatenate([state_pool[a], u3], axis=1)[:, -PS:])
        else:
            w_in = w_in_o[a].astype(BF16)
            w_out = w_out_o[a].astype(BF16)
            w_gu = w_exp_gu[a].astype(BF16)
            w_dn = w_exp_down[a].astype(BF16)
            g_mix, g_ffn, g_hn = row(norm_mix_o[a]), row(norm_ffn_o[a]), row(hg_norm[a])
            lb = row(lower_bounds[l])
            w_r, b_r = w_router[a].astype(F32), row(b_router[a])
            g_fin = row(final_norm) if l == depth - 1 else None
            assert g_fin is not None

            qs, kk, logf, vv, gs = _hgrn_in_proj(hp, g_mix, w_in, lb)
            on, s_new = _gla_prompt(qs, kk, logf, vv, gs, g_hn, B, L, HG)
            hp = _linear_residual([on], [w_out], hp, "out_proj_o")
            hp = _moe(hp, g_ffn, w_r, b_r, w_gu, w_dn, g_fin)
            outs_p["hgrn"].append(s_new)

            qs, kk, logf, vv, gs = _hgrn_in_proj(hs, g_mix, w_in, lb)
            r3 = lambda t: t.reshape(DB, Ld, -1)
            on, s_new = _gla_sample(r3(qs), r3(kk), r3(logf), r3(vv), r3(gs), g_hn, state_hgrn[a], DB, Ld, HG)
            hs = _linear_residual([on.reshape(DB * Ld, -1)], [w_out], hs, "out_proj_o")
            hs = _moe(hs, g_ffn, w_r, b_r, w_gu, w_dn, g_fin)
            outs_s["hgrn"].append(s_new)

    return (hp.reshape(B, L, D), hs.reshape(DB, Ld, D),
            jnp.stack(outs_p["ckv"]), jnp.stack(outs_p["krope"]), jnp.stack(outs_p["pool"]),
            jnp.stack(outs_p["hgrn"]),
            jnp.stack(outs_s["ckv"]), jnp.stack(outs_s["krope"]), jnp.stack(outs_s["pool"]),
            jnp.stack(outs_s["hgrn"]))
```
